```python
import math
import jax
import jax.numpy as jnp
from jax import lax
import numpy as np

D_MODEL = 1024
BATCH = 2
SEQ = 16384
DEPTH = 4

N_MIXERS = 4
PLE_DIM = 256
EPS = 1e-6

DSA_PATTERNS = ((128, 1), (512, 4), (2048, 16))
DSA_GROUPS = len(DSA_PATTERNS)
DSA_HEADS = 8
DSA_HEAD_DIM = D_MODEL // DSA_HEADS
DSA_IN = DSA_GROUPS * 3 * DSA_HEADS * DSA_HEAD_DIM

MLA_HEADS = D_MODEL // 64
MLA_Q_LORA = D_MODEL // 4
MLA_KV_LORA = D_MODEL // 8
MLA_NOPE = 64
MLA_ROPE = 32
MLA_V = 64
MLA_IN = MLA_Q_LORA + MLA_KV_LORA + MLA_ROPE
MLA_THETA = 10000.0
Q_BLOCK = 128

RET_HEADS = D_MODEL // 256
RET_QK = 256
RET_V = 512
RET_IN = RET_HEADS * (2 * RET_QK + 2 * RET_V)
RET_CHUNK = 128
RET_THETA = 10000.0

SSD_D_INNER = 2 * D_MODEL
SSD_HEAD_DIM = 64
SSD_HEADS = SSD_D_INNER // SSD_HEAD_DIM
SSD_GROUPS = 4
SSD_HEADS_PER_GROUP = SSD_HEADS // SSD_GROUPS
SSD_STATE = 128
SSD_CONV = 4
SSD_CONV_DIM = SSD_D_INNER + 2 * SSD_GROUPS * SSD_STATE
SSD_IN = SSD_D_INNER + SSD_CONV_DIM + SSD_HEADS
SSD_CHUNK = 128

MOE_GROUPS = 4
MOE_EXPERTS_PER_GROUP = 8
MOE_EXPERTS = MOE_GROUPS * MOE_EXPERTS_PER_GROUP
MOE_TOPK = 2
MOE_D_FF = D_MODEL // 2
MOE_BLOCK = 256

kernel_name = 'hybrid_dsa_mla_retnet_ssd_hmoe'


def _rmsnorm(x, g):
    xf = x.astype(jnp.float32)
    y = xf * lax.rsqrt(jnp.mean(xf * xf, axis=-1, keepdims=True) + EPS)
    return (y * g.astype(jnp.float32)).astype(x.dtype)


def _rope(x, theta):
    S, d = x.shape[1], x.shape[-1]
    half = d // 2
    inv = theta ** (-jnp.arange(half, dtype=jnp.float32) / half)
    ang = jnp.arange(S, dtype=jnp.float32)[:, None] * inv
    cos = jnp.cos(ang)[:, None, :]
    sin = jnp.sin(ang)[:, None, :]
    xf = x.astype(jnp.float32)
    x1, x2 = xf[..., :half], xf[..., half:]
    return jnp.concatenate([x1 * cos - x2 * sin, x1 * sin + x2 * cos], axis=-1).astype(x.dtype)


def _alibi_slopes(n):
    return 2.0 ** (-8.0 * jnp.arange(1, n + 1, dtype=jnp.float32) / n)


def _causal_conv(x, w, b):
    k, c = w.shape
    y = lax.conv_general_dilated(x, w[:, None, :].astype(x.dtype), window_strides=(1,),
                                 padding=[(k - 1, 0)], dimension_numbers=('NWC', 'WIO', 'NWC'),
                                 feature_group_count=c)
    return y + b.astype(x.dtype)


def _dilated_group(q, k, v, window, dil, slopes):
    Bn, S, H, dh = q.shape
    W = window // dil
    L = S // dil
    nb = -(-L // W)
    Lp = nb * W

    def to_res(t):
        t = t.reshape(Bn, L, dil, H, dh).transpose(0, 2, 3, 1, 4)
        t = jnp.pad(t, ((0, 0), (0, 0), (0, 0), (0, Lp - L), (0, 0)))
        return t.reshape(Bn, dil, H, nb, W, dh)

    def with_prev(t):
        prev = jnp.concatenate([jnp.zeros_like(t[:, :, :, :1]), t[:, :, :, :-1]], axis=3)
        return jnp.concatenate([prev, t], axis=4)

    def from_res(t):
        t = t.reshape(Bn, dil, H, Lp, *t.shape[5:])[:, :, :, :L]
        t = jnp.moveaxis(t, 3, 1)
        return t.reshape(Bn, S, H, *t.shape[4:])

    qb = to_res(q)
    kb = with_prev(to_res(k))
    vb = with_prev(to_res(v))
    s = jnp.einsum('brhnqd,brhnkd->brhnqk', qb, kb).astype(jnp.float32) * (dh ** -0.5)
    qi = jnp.arange(W)[:, None]
    kj = jnp.arange(2 * W)[None, :]
    steps = W + qi - kj
    key_step = jnp.arange(nb)[:, None, None] * W - W + kj
    valid = (steps >= 0) & (steps <= W) & (key_step >= 0)
    alibi = -slopes.astype(jnp.float32)[:, None, None, None] * (dil * steps).astype(jnp.float32)
    s = jnp.where(valid, s + alibi, -jnp.inf)
    m = jnp.max(s, axis=-1)
    e = jnp.exp(s - m[..., None])
    den = jnp.sum(e, axis=-1)
    o = jnp.einsum('brhnqk,brhnkd->brhnqd', e, vb.astype(jnp.float32)) / den[..., None]
    return from_res(o), from_res(m), from_res(den)


def _dilated_attention(h, w_in, w_out):
    Bn, S, _ = h.shape
    qkv = (h @ w_in).reshape(Bn, S, DSA_GROUPS, 3, DSA_HEADS, DSA_HEAD_DIM)
    slopes = _alibi_slopes(DSA_GROUPS * DSA_HEADS).reshape(DSA_GROUPS, DSA_HEADS)
    outs, maxes, dens = [], [], []
    for g, (window, dil) in enumerate(DSA_PATTERNS):
        o, m, den = _dilated_group(qkv[:, :, g, 0], qkv[:, :, g, 1], qkv[:, :, g, 2], window, dil, slopes[g])
        outs.append(o)
        maxes.append(m)
        dens.append(den)
    o = jnp.stack(outs)
    m = jnp.stack(maxes)
    den = jnp.stack(dens)
    wgt = den * jnp.exp(m - jnp.max(m, axis=0, keepdims=True))
    o = jnp.sum(wgt[..., None] * o, axis=0) / jnp.sum(wgt, axis=0)[..., None]
    return o.reshape(Bn, S, DSA_HEADS * DSA_HEAD_DIM).astype(h.dtype) @ w_out


def _mla(h, w_in, q_norm, w_qb, kv_norm, w_kvb, w_out):
    Bn, S, _ = h.shape
    lat = h @ w_in
    q_lat = lat[..., :MLA_Q_LORA]
    kv_lat = lat[..., MLA_Q_LORA:MLA_Q_LORA + MLA_KV_LORA]
    k_rope = lat[..., MLA_Q_LORA + MLA_KV_LORA:]
    q = (_rmsnorm(q_lat, q_norm) @ w_qb).reshape(Bn, S, MLA_HEADS, MLA_NOPE + MLA_ROPE)
    kv = (_rmsnorm(kv_lat, kv_norm) @ w_kvb).reshape(Bn, S, MLA_HEADS, MLA_NOPE + MLA_V)
    q_nope = q[..., :MLA_NOPE]
    q_rope = _rope(q[..., MLA_NOPE:], MLA_THETA)
    k_nope = kv[..., :MLA_NOPE]
    v32 = kv[..., MLA_NOPE:].astype(jnp.float32)
    k_rope = _rope(k_rope[:, :, None, :], MLA_THETA)[:, :, 0]
    scale = (MLA_NOPE + MLA_ROPE) ** -0.5
    nq = S // Q_BLOCK
    kpos = jnp.arange(S)

    def blocks(t):
        return t.reshape(Bn, nq, Q_BLOCK, *t.shape[2:]).swapaxes(0, 1)

    def attend(args):
        bi, qn, qr = args
        s = (jnp.einsum('bqhd,bkhd->bhqk', qn, k_nope)
             + jnp.einsum('bqhr,bkr->bhqk', qr, k_rope)).astype(jnp.float32) * scale
        qpos = bi * Q_BLOCK + jnp.arange(Q_BLOCK)
        s = jnp.where(kpos[None, :] <= qpos[:, None], s, -jnp.inf)
        pr = jax.nn.softmax(s, axis=-1)
        return jnp.einsum('bhqk,bkhd->bqhd', pr, v32)

    o = lax.map(attend, (jnp.arange(nq), blocks(q_nope), blocks(q_rope)))
    o = o.swapaxes(0, 1).reshape(Bn, S, MLA_HEADS * MLA_V)
    return o.astype(h.dtype) @ w_out


def _retention(h, w_in, gn_w, w_out):
    Bn, S, _ = h.shape
    proj = h @ w_in
    nqk = RET_HEADS * RET_QK
    nv = RET_HEADS * RET_V
    q = proj[..., :nqk].reshape(Bn, S, RET_HEADS, RET_QK)
    k = proj[..., nqk:2 * nqk].reshape(Bn, S, RET_HEADS, RET_QK)
    v = proj[..., 2 * nqk:2 * nqk + nv].reshape(Bn, S, RET_HEADS, RET_V).astype(jnp.float32)
    g = proj[..., 2 * nqk + nv:]
    q = _rope(q, RET_THETA).astype(jnp.float32)
    k = _rope(k, RET_THETA).astype(jnp.float32) * (RET_QK ** -0.5)
    log_gamma = jnp.log1p(-(2.0 ** (-5.0 - jnp.arange(RET_HEADS, dtype=jnp.float32))))
    idx = jnp.arange(RET_CHUNK, dtype=jnp.float32)
    diff = idx[:, None] - idx[None, :]
    intra = jnp.exp(jnp.where(diff >= 0, log_gamma[:, None, None] * diff, -jnp.inf))
    q_dec = jnp.exp(log_gamma[:, None] * (idx + 1.0))
    k_dec = jnp.exp(log_gamma[:, None] * (RET_CHUNK - 1.0 - idx))
    c_dec = jnp.exp(log_gamma * RET_CHUNK)
    nc = S // RET_CHUNK

    def chunks(t):
        return t.reshape(Bn, nc, RET_CHUNK, RET_HEADS, -1).transpose(1, 0, 3, 2, 4)

    def step(state, inp):
        qc, kc, vc = inp
        o = jnp.einsum('bhij,bhjv->bhiv', jnp.einsum('bhid,bhjd->bhij', qc, kc) * intra, vc)
        o = o + jnp.einsum('bhid,bhdv->bhiv', qc * q_dec[..., None], state)
        state = state * c_dec[:, None, None] + jnp.einsum('bhjd,bhjv->bhdv', kc * k_dec[..., None], vc)
        return state, o

    state0 = jnp.zeros((Bn, RET_HEADS, RET_QK, RET_V), jnp.float32)
    _, o = lax.scan(step, state0, (chunks(q), chunks(k), chunks(v)))
    o = o.transpose(1, 0, 3, 2, 4).reshape(Bn, S, RET_HEADS, RET_V)
    mu = jnp.mean(o, axis=-1, keepdims=True)
    var = jnp.mean(jnp.square(o - mu), axis=-1, keepdims=True)
    o = ((o - mu) * lax.rsqrt(var + EPS)).reshape(Bn, S, nv) * gn_w.astype(jnp.float32)
    return (jax.nn.silu(g.astype(jnp.float32)) * o).astype(h.dtype) @ w_out


def _ssd(h, w_in, conv_w, conv_b, dt_bias, A_log, D_skip, norm_w, w_out):
    Bn, S, _ = h.shape
    G, Hg, P, N, C = SSD_GROUPS, SSD_HEADS_PER_GROUP, SSD_HEAD_DIM, SSD_STATE, SSD_CHUNK
    proj = h @ w_in
    z = proj[..., :SSD_D_INNER]
    xbc = jax.nn.silu(_causal_conv(proj[..., SSD_D_INNER:SSD_D_INNER + SSD_CONV_DIM], conv_w, conv_b))
    dt = proj[..., SSD_D_INNER + SSD_CONV_DIM:]
    xs = xbc[..., :SSD_D_INNER].reshape(Bn, S, G, Hg, P).astype(jnp.float32)
    bm = xbc[..., SSD_D_INNER:SSD_D_INNER + G * N].reshape(Bn, S, G, N).astype(jnp.float32)
    cm = xbc[..., SSD_D_INNER + G * N:].reshape(Bn, S, G, N).astype(jnp.float32)
    dt = jax.nn.softplus(dt.astype(jnp.float32) + dt_bias.astype(jnp.float32)).reshape(Bn, S, G, Hg)
    a_h = -jnp.exp(A_log.astype(jnp.float32)).reshape(G, Hg)
    nc = S // C
    causal = jnp.tril(jnp.ones((C, C), dtype=bool))

    def chunks(t):
        return t.reshape(Bn, nc, C, *t.shape[2:]).swapaxes(0, 1)

    def step(state, inp):
        xc, dtc, bc, cc = inp
        a_cs = jnp.cumsum(dtc * a_h, axis=1)
        seg = a_cs[:, :, None] - a_cs[:, None, :]
        decay = jnp.exp(jnp.where(causal[None, :, :, None, None], seg, -jnp.inf))
        cb = jnp.einsum('bign,bjgn->bijg', cc, bc)
        y = jnp.einsum('bijg,bijgh,bjgh,bjghp->bighp', cb, decay, dtc, xc)
        y = y + jnp.einsum('bign,bghpn,bigh->bighp', cc, state, jnp.exp(a_cs))
        a_end = a_cs[:, -1]
        w_end = jnp.exp(a_end[:, None] - a_cs) * dtc
        state = state * jnp.exp(a_end)[..., None, None] + jnp.einsum('bjgn,bjgh,bjghp->bghpn', bc, w_end, xc)
        return state, y

    state0 = jnp.zeros((Bn, G, Hg, P, N), jnp.float32)
    _, y = lax.scan(step, state0, (chunks(xs), chunks(dt), chunks(bm), chunks(cm)))
    y = y.swapaxes(0, 1).reshape(Bn, S, G, Hg, P) + D_skip.astype(jnp.float32).reshape(G, Hg, 1) * xs
    y = y.reshape(Bn, S, SSD_D_INNER) * jax.nn.silu(z.astype(jnp.float32))
    yg = y.reshape(Bn, S, G, SSD_D_INNER // G)
    yg = yg * lax.rsqrt(jnp.mean(yg * yg, axis=-1, keepdims=True) + EPS)
    y = yg.reshape(Bn, S, SSD_D_INNER) * norm_w.astype(jnp.float32)
    return y.astype(h.dtype) @ w_out


def _hier_moe(h, w_group, b_group, w_expert, b_expert, w_gate, w_up, w_down):
    Bn, S, D = h.shape
    T = Bn * S
    xt = h.reshape(T, D)
    gp = jax.nn.softmax((xt @ w_group + b_group).astype(jnp.float32), axis=-1)
    gval, gidx = lax.top_k(gp, 1)
    el = (xt @ w_expert + b_expert).astype(jnp.float32).reshape(T, MOE_GROUPS, MOE_EXPERTS_PER_GROUP)
    el = jnp.take_along_axis(el, gidx[:, :, None], axis=1)[:, 0]
    ev, eidx = lax.top_k(el, MOE_TOPK)
    ew = jax.nn.softmax(ev, axis=-1) * gval
    expert = (gidx * MOE_EXPERTS_PER_GROUP + eidx).reshape(-1)
    tok = jnp.repeat(jnp.arange(T, dtype=jnp.int32), MOE_TOPK)
    wts = ew.reshape(-1)
    A = T * MOE_TOPK
    order = jnp.argsort(expert)
    e_s, tok_s, w_s = expert[order], tok[order], wts[order]
    counts = jax.ops.segment_sum(jnp.ones((A,), jnp.int32), expert, num_segments=MOE_EXPERTS)
    padded = (counts + MOE_BLOCK - 1) // MOE_BLOCK * MOE_BLOCK
    pad_end = jnp.cumsum(padded)
    pad_start = pad_end - padded
    start = jnp.cumsum(counts) - counts
    pos = pad_start[e_s] + (jnp.arange(A, dtype=jnp.int32) - start[e_s])
    n_blocks = -(-A // MOE_BLOCK) + MOE_EXPERTS
    P = n_blocks * MOE_BLOCK
    tok_pad = jnp.zeros((P,), jnp.int32).at[pos].set(tok_s)
    w_pad = jnp.zeros((P,), xt.dtype).at[pos].set(w_s.astype(xt.dtype))
    blk_expert = jnp.minimum(jnp.searchsorted(pad_end, jnp.arange(n_blocks) * MOE_BLOCK, side='right'),
                             MOE_EXPERTS - 1)

    def run_block(args):
        e, tk, w = args
        xb = xt[tk]
        hb = jax.nn.silu(xb @ w_gate[e]) * (xb @ w_up[e])
        return (hb @ w_down[e]) * w[:, None]

    yb = lax.map(run_block, (blk_expert, tok_pad.reshape(n_blocks, MOE_BLOCK), w_pad.reshape(n_blocks, MOE_BLOCK)))
    y = jnp.zeros_like(xt).at[tok_pad].add(yb.reshape(P, D))
    return y.reshape(Bn, S, D)


def setup_inputs(seed: int = 0) -> dict:
    key = jax.random.key(seed)
    keys = jax.random.split(key, 40)
    ks = iter([keys[i] for i in range(40)])
    f32 = jnp.float32

    def w(shape, fan_in, scale=1.0):
        return jax.random.normal(next(ks), shape, f32) * (scale * fan_in ** -0.5)

    def gain(shape):
        return 1.0 + 0.02 * jax.random.normal(next(ks), shape, f32)

    def small(shape, s=0.02):
        return s * jax.random.normal(next(ks), shape, f32)

    out_s = 0.5
    x = jax.random.normal(next(ks), (BATCH, SEQ, D_MODEL), f32)
    p = jax.random.normal(next(ks), (DEPTH, BATCH, SEQ, PLE_DIM), f32)
    norm_mix = gain((DEPTH, D_MODEL))
    norm_moe = gain((DEPTH, D_MODEL))
    norm_ple = gain((DEPTH, D_MODEL))
    norm_final = gain((D_MODEL,))
    dsa_w_in = w((D_MODEL, DSA_IN), D_MODEL)
    dsa_w_out = w((DSA_HEADS * DSA_HEAD_DIM, D_MODEL), DSA_HEADS * DSA_HEAD_DIM, out_s)
    mla_w_in = w((D_MODEL, MLA_IN), D_MODEL)
    mla_q_norm = gain((MLA_Q_LORA,))
    mla_w_qb = w((MLA_Q_LORA, MLA_HEADS * (MLA_NOPE + MLA_ROPE)), MLA_Q_LORA)
    mla_kv_norm = gain((MLA_KV_LORA,))
    mla_w_kvb = w((MLA_KV_LORA, MLA_HEADS * (MLA_NOPE + MLA_V)), MLA_KV_LORA)
    mla_w_out = w((MLA_HEADS * MLA_V, D_MODEL), MLA_HEADS * MLA_V, out_s)
    ret_w_in = w((D_MODEL, RET_IN), D_MODEL)
    ret_gn_w = gain((RET_HEADS * RET_V,))
    ret_w_out = w((RET_HEADS * RET_V, D_MODEL), RET_HEADS * RET_V, out_s)
    ssd_w_in = w((D_MODEL, SSD_IN), D_MODEL)
    ssd_conv_w = w((SSD_CONV, SSD_CONV_DIM), SSD_CONV)
    ssd_conv_b = small((SSD_CONV_DIM,))
    dt0 = jnp.exp(jax.random.uniform(next(ks), (SSD_HEADS,), f32, math.log(1e-3), math.log(1e-1)))
    ssd_dt_bias = dt0 + jnp.log(-jnp.expm1(-dt0))
    ssd_A_log = jnp.log(jax.random.uniform(next(ks), (SSD_HEADS,), f32, 1.0, 16.0))
    ssd_D = gain((SSD_HEADS,))
    ssd_norm_w = gain((SSD_D_INNER,))
    ssd_w_out = w((SSD_D_INNER, D_MODEL), SSD_D_INNER, out_s)
    moe_w_group = w((DEPTH, D_MODEL, MOE_GROUPS), D_MODEL)
    moe_b_group = small((DEPTH, MOE_GROUPS), 0.01)
    moe_w_expert = w((DEPTH, D_MODEL, MOE_EXPERTS), D_MODEL)
    moe_b_expert = small((DEPTH, MOE_EXPERTS), 0.01)
    moe_w_gate = w((DEPTH, MOE_EXPERTS, D_MODEL, MOE_D_FF), D_MODEL)
    moe_w_up = w((DEPTH, MOE_EXPERTS, D_MODEL, MOE_D_FF), D_MODEL)
    moe_w_down = w((DEPTH, MOE_EXPERTS, MOE_D_FF, D_MODEL), MOE_D_FF, out_s)
    ple_w = w((DEPTH, PLE_DIM, D_MODEL), PLE_DIM, out_s)
    ple_w_gate = w((DEPTH, D_MODEL, D_MODEL), D_MODEL)
    return {'x': x, 'p': p, 'norm_mix': norm_mix, 'norm_moe': norm_moe, 'norm_ple': norm_ple,
            'norm_final': norm_final, 'dsa_w_in': dsa_w_in, 'dsa_w_out': dsa_w_out,
            'mla_w_in': mla_w_in, 'mla_q_norm': mla_q_norm, 'mla_w_qb': mla_w_qb,
            'mla_kv_norm': mla_kv_norm, 'mla_w_kvb': mla_w_kvb, 'mla_w_out': mla_w_out,
            'ret_w_in': ret_w_in, 'ret_gn_w': ret_gn_w, 'ret_w_out': ret_w_out,
            'ssd_w_in': ssd_w_in, 'ssd_conv_w': ssd_conv_w, 'ssd_conv_b': ssd_conv_b,
            'ssd_dt_bias': ssd_dt_bias, 'ssd_A_log': ssd_A_log, 'ssd_D': ssd_D,
            'ssd_norm_w': ssd_norm_w, 'ssd_w_out': ssd_w_out,
            'moe_w_group': moe_w_group, 'moe_b_group': moe_b_group,
            'moe_w_expert': moe_w_expert, 'moe_b_expert': moe_b_expert,
            'moe_w_gate': moe_w_gate, 'moe_w_up': moe_w_up, 'moe_w_down': moe_w_down,
            'ple_w': ple_w, 'ple_w_gate': ple_w_gate}


def reference(x, p, norm_mix, norm_moe, norm_ple, norm_final, dsa_w_in, dsa_w_out,
              mla_w_in, mla_q_norm, mla_w_qb, mla_kv_norm, mla_w_kvb, mla_w_out,
              ret_w_in, ret_gn_w, ret_w_out,
              ssd_w_in, ssd_conv_w, ssd_conv_b, ssd_dt_bias, ssd_A_log, ssd_D, ssd_norm_w, ssd_w_out,
              moe_w_group, moe_b_group, moe_w_expert, moe_b_expert, moe_w_gate, moe_w_up, moe_w_down,
              ple_w, ple_w_gate):
    for i in range(DEPTH):
        h = _rmsnorm(x, norm_mix[i])
        kind = i % N_MIXERS
        if kind == 0:
            y = _dilated_attention(h, dsa_w_in, dsa_w_out)
        elif kind == 1:
            y = _mla(h, mla_w_in, mla_q_norm, mla_w_qb, mla_kv_norm, mla_w_kvb, mla_w_out)
        elif kind == 2:
            y = _retention(h, ret_w_in, ret_gn_w, ret_w_out)
        else:
            y = _ssd(h, ssd_w_in, ssd_conv_w, ssd_conv_b, ssd_dt_bias, ssd_A_log, ssd_D, ssd_norm_w, ssd_w_out)
        x = x + y
        h = _rmsnorm(x, norm_moe[i])
        x = x + _hier_moe(h, moe_w_group[i], moe_b_group[i], moe_w_expert[i], moe_b_expert[i],
                          moe_w_gate[i], moe_w_up[i], moe_w_down[i])
        gate = jax.nn.sigmoid(_rmsnorm(x, norm_ple[i]) @ ple_w_gate[i])
        x = x + (p[i] @ ple_w[i]) * gate
    return _rmsnorm(x, norm_final)
```

```python
import functools
import math

import jax
import jax.numpy as jnp
from jax import lax
from jax.experimental import pallas as pl
from jax.experimental.pallas import tpu as pltpu

F32 = jnp.float32
BF16 = jnp.bfloat16
EPS = 1e-6
LANES = 128
VMEM_LIMIT = 48 * 1024 * 1024

DSA_PATTERNS = ((128, 1), (512, 4), (2048, 16))
DSA_HEADS = 8
DSA_SPAN = 2048
MLA_HEADS = 16
MLA_Q_LORA, MLA_KV_LORA, MLA_NOPE, MLA_ROPE, MLA_V = 256, 128, 64, 32, 64
MLA_THETA = 10000.0
RET_HEADS, RET_QK, RET_V = 4, 256, 512
RET_THETA = 10000.0
SSD_D_INNER, SSD_HEADS, SSD_GROUPS, SSD_STATE, SSD_CONV, SSD_P = 2048, 32, 4, 128, 4, 64
SSD_CONV_DIM = SSD_D_INNER + 2 * SSD_GROUPS * SSD_STATE
MOE_GROUPS, MOE_EPG, MOE_EXPERTS, MOE_TOPK, MOE_FF = 4, 8, 32, 2, 512
MOE_BLOCK = 256
TOK_TILE = 256


def _cparams(*sem):
    return pltpu.CompilerParams(dimension_semantics=sem, vmem_limit_bytes=VMEM_LIMIT)


def _rms(x, g):
    return x * lax.rsqrt(jnp.mean(x * x, axis=-1, keepdims=True) + EPS) * g


def _dot(a, b):
    return jnp.dot(a, b, preferred_element_type=F32)


def _dot_nt(a, b):
    return lax.dot_general(a, b, (((1,), (1,)), ((), ())), preferred_element_type=F32)


def _norm_linear_kernel(x_ref, g_ref, w_ref, o_ref, h_scr, *, head_major):
    @pl.when(pl.program_id(1) == 0)
    def _():
        h_scr[...] = _rms(x_ref[...], g_ref[...]).astype(BF16)

    r = _dot(h_scr[...], w_ref[...])
    if head_major:
        for c in range(r.shape[1] // LANES):
            o_ref[c] = r[:, c * LANES:(c + 1) * LANES].astype(o_ref.dtype)
    else:
        o_ref[...] = r.astype(o_ref.dtype)


def _norm_linear(x, g, w, out_dtype, *, head_major=False, tm=512, tn=1024):
    T, D = x.shape
    N = w.shape[1]
    tn = min(tn, N)
    assert T % tm == 0 and N % tn == 0 and tn % LANES == 0
    if head_major:
        out_shape = jax.ShapeDtypeStruct((N // LANES, T, LANES), out_dtype)
        out_spec = pl.BlockSpec((tn // LANES, tm, LANES), lambda i, j: (j, i, 0))
    else:
        out_shape = jax.ShapeDtypeStruct((T, N), out_dtype)
        out_spec = pl.BlockSpec((tm, tn), lambda i, j: (i, j))
    return pl.pallas_call(
        functools.partial(_norm_linear_kernel, head_major=head_major),
        grid=(T // tm, N // tn),
        in_specs=[pl.BlockSpec((tm, D), lambda i, j: (i, 0)),
                  pl.BlockSpec((1, D), lambda i, j: (0, 0)),
                  pl.BlockSpec((D, tn), lambda i, j: (0, j))],
        out_specs=out_spec,
        out_shape=out_shape,
        scratch_shapes=[pltpu.VMEM((tm, D), BF16)],
        compiler_params=_cparams("parallel", "arbitrary"),
    )(x, g.reshape(1, D), w)


def _linear_res_kernel(a_ref, w_ref, r_ref, o_ref, *, head_major):
    if head_major:
        a = jnp.concatenate([a_ref[c] for c in range(a_ref.shape[0])], axis=1)
    else:
        a = a_ref[...]
    o_ref[...] = r_ref[...] + _dot(a, w_ref[...])


def _linear_res(a, w, res, *, head_major=False, tm=512):
    T, D = res.shape
    K = w.shape[0]
    if head_major:
        a_spec = pl.BlockSpec((K // LANES, tm, LANES), lambda i: (0, i, 0))
    else:
        a_spec = pl.BlockSpec((tm, K), lambda i: (i, 0))
    return pl.pallas_call(
        functools.partial(_linear_res_kernel, head_major=head_major),
        grid=(T // tm,),
        in_specs=[a_spec,
                  pl.BlockSpec((K, D), lambda i: (0, 0)),
                  pl.BlockSpec((tm, D), lambda i: (i, 0))],
        out_specs=pl.BlockSpec((tm, D), lambda i: (i, 0)),
        out_shape=jax.ShapeDtypeStruct((T, D), F32),
        compiler_params=_cparams("parallel"),
    )(a, w, res)


def _dsa_group_kernel(sl_ref, q_ref, k_ref, kh_ref, v_ref, vh_ref, o_ref, sm_ref, sd_ref, *, dil, rows):
    W = LANES
    i = pl.program_id(1)
    h = pl.program_id(2)
    scale = float(W) ** -0.5
    slope = sl_ref[h] * float(dil)
    qi = lax.broadcasted_iota(jnp.int32, (W, 2 * W), 0)
    kj = lax.broadcasted_iota(jnp.int32, (W, 2 * W), 1)
    steps = W + qi - kj
    band = (steps >= 0) & (steps <= W)
    first_lim = jnp.where(i == 0, W, 0)
    band_first = band & (kj >= first_lim)
    bias = (-slope) * steps.astype(F32)
    lane = lax.broadcasted_iota(jnp.int32, (W, LANES), 1)

    @pl.when(h == 0)
    def _():
        sm_ref[...] = jnp.zeros(sm_ref.shape, F32)
        sd_ref[...] = jnp.ones(sd_ref.shape, F32)

    for r in range(dil):
        cs = slice(r * LANES, (r + 1) * LANES)
        for n in range(rows // W):
            rs = slice(n * W, (n + 1) * W)
            q = q_ref[rs, cs]
            if n == 0:
                kk = jnp.concatenate([kh_ref[:, cs], k_ref[rs, cs]], axis=0)
                vv = jnp.concatenate([vh_ref[:, cs], v_ref[rs, cs]], axis=0)
                valid = band_first
            else:
                kk = k_ref[(n - 1) * W:(n + 1) * W, cs]
                vv = v_ref[(n - 1) * W:(n + 1) * W, cs]
                valid = band
            s = _dot_nt(q, kk) * scale + bias
            s = jnp.where(valid, s, -jnp.inf)
            m = jnp.max(s, axis=-1, keepdims=True)
            e = jnp.exp(s - m)
            den = jnp.sum(e, axis=-1, keepdims=True)
            o = _dot(e.astype(BF16), vv) / den
            o_ref[rs, cs] = o.astype(o_ref.dtype)
            sm_ref[rs, cs] = jnp.where(lane == h, m, sm_ref[rs, cs])
            sd_ref[rs, cs] = jnp.where(lane == h, den, sd_ref[rs, cs])


def _dsa_group(qkv, slopes, g, dil, B, S):
    H = DSA_HEADS
    L = S // dil
    rows = DSA_SPAN // dil
    width = dil * LANES
    qv = qkv.reshape(qkv.shape[0], B, L, width)
    hb = rows // LANES
    base = g * 3 * H

    def qmap(o):
        return lambda b, i, h, sl: (base + o * H + h, b, i, 0)

    def hmap(o):
        return lambda b, i, h, sl: (base + o * H + h, b, jnp.maximum(i * hb - 1, 0), 0)

    blk = (None, None, rows, width)
    hblk = (None, None, LANES, width)
    o, sm, sd = pl.pallas_call(
        functools.partial(_dsa_group_kernel, dil=dil, rows=rows),
        grid_spec=pltpu.PrefetchScalarGridSpec(
            num_scalar_prefetch=1,
            grid=(B, L // rows, H),
            in_specs=[pl.BlockSpec(blk, qmap(0)),
                      pl.BlockSpec(blk, qmap(1)), pl.BlockSpec(hblk, hmap(1)),
                      pl.BlockSpec(blk, qmap(2)), pl.BlockSpec(hblk, hmap(2))],
            out_specs=[pl.BlockSpec(blk, lambda b, i, h, sl: (h, b, i, 0)),
                       pl.BlockSpec((None, rows, width), lambda b, i, h, sl: (b, i, 0)),
                       pl.BlockSpec((None, rows, width), lambda b, i, h, sl: (b, i, 0))]),
        out_shape=[jax.ShapeDtypeStruct((H, B, L, width), BF16),
                   jax.ShapeDtypeStruct((B, L, width), F32),
                   jax.ShapeDtypeStruct((B, L, width), F32)],
        compiler_params=_cparams("parallel", "parallel", "arbitrary"),
    )(slopes, qv, qv, qv, qv, qv)
    T = B * S
    return o.reshape(H, T, LANES), sm.reshape(T, LANES), sd.reshape(T, LANES)


def _dsa_out_kernel(o0, o1, o2, m0, m1, m2, d0, d1, d2, w_ref, x_ref, out_ref):
    ms = [m0[...], m1[...], m2[...]]
    mx = jnp.maximum(jnp.maximum(ms[0], ms[1]), ms[2])
    ws = [d[...] * jnp.exp(m - mx) for d, m in zip((d0, d1, d2), ms)]
    tot = ws[0] + ws[1] + ws[2]
    ws = [w / tot for w in ws]
    parts = []
    for h in range(DSA_HEADS):
        acc = ws[0][:, h:h + 1] * o0[h].astype(F32)
        acc = acc + ws[1][:, h:h + 1] * o1[h].astype(F32)
        acc = acc + ws[2][:, h:h + 1] * o2[h].astype(F32)
        parts.append(acc.astype(BF16))
    a = jnp.concatenate(parts, axis=1)
    out_ref[...] = x_ref[...] + _dot(a, w_ref[...])


def _dsa_out(os_, ms, ds, w, x, tm=512):
    T, D = x.shape
    H = DSA_HEADS
    ospec = pl.BlockSpec((H, tm, LANES), lambda i: (0, i, 0))
    sspec = pl.BlockSpec((tm, LANES), lambda i: (i, 0))
    return pl.pallas_call(
        _dsa_out_kernel,
        grid=(T // tm,),
        in_specs=[ospec] * 3 + [sspec] * 6 + [pl.BlockSpec((H * LANES, D), lambda i: (0, 0)),
                                              pl.BlockSpec((tm, D), lambda i: (i, 0))],
        out_specs=pl.BlockSpec((tm, D), lambda i: (i, 0)),
        out_shape=jax.ShapeDtypeStruct((T, D), F32),
        compiler_params=_cparams("parallel"),
    )(*os_, *ms, *ds, w, x)


def _dilated_attention(x, g, w_in, w_out, B, S):
    qkv = _norm_linear(x, g, w_in.astype(BF16), BF16, head_major=True)
    ng = len(DSA_PATTERNS)
    slopes = 2.0 ** (-8.0 * jnp.arange(1, ng * DSA_HEADS + 1, dtype=F32) / (ng * DSA_HEADS))
    os_, ms, ds = [], [], []
    for gi, (window, dil) in enumerate(DSA_PATTERNS):
        assert window // dil == LANES
        o, m, d = _dsa_group(qkv, slopes[gi * DSA_HEADS:(gi + 1) * DSA_HEADS], gi, dil, B, S)
        os_.append(o)
        ms.append(m)
        ds.append(d)
    return _dsa_out(os_, ms, ds, w_out.astype(BF16), x)


def _mla_qkv_kernel(lat_ref, qn_ref, kvn_ref, wq_ref, wqr_ref, wk_ref, wv_ref, cos_ref, sin_ref,
                    q_ref, k_ref, v_ref):
    lat = lat_ref[...]
    cos = cos_ref[...]
    sin = sin_ref[...]
    hq = _rms(lat[:, :MLA_Q_LORA], qn_ref[...]).astype(BF16)
    hkv = _rms(lat[:, MLA_Q_LORA:MLA_Q_LORA + MLA_KV_LORA], kvn_ref[...]).astype(BF16)
    o1 = MLA_Q_LORA + MLA_KV_LORA
    kr = lat[:, o1:o1 + LANES] * cos + lat[:, o1 + LANES:o1 + 2 * LANES] * sin
    q = _dot(hq, wq_ref[...])
    qr = _dot(hq, wqr_ref[...])
    k = _dot(hkv, wk_ref[...])
    v = _dot(hkv, wv_ref[...])
    lane = lax.broadcasted_iota(jnp.int32, cos.shape, 1)
    cos_q = jnp.where(lane < MLA_NOPE, 1.0, cos)
    for h in range(MLA_HEADS):
        cs = slice(h * LANES, (h + 1) * LANES)
        q_ref[h] = (q[:, cs] * cos_q + qr[:, cs] * sin).astype(BF16)
        k_ref[h] = (k[:, cs] + kr).astype(BF16)
        v_ref[h] = v[:, cs].astype(BF16)


def _mla_attn_kernel(q_ref, k_ref, v_ref, o_ref, m_scr, l_scr, acc_scr, *, tq, tk, scale):
    qi = pl.program_id(2)
    ki = pl.program_id(3)
    last = ((qi + 1) * tq - 1) // tk

    @pl.when(ki == 0)
    def _():
        m_scr[...] = jnp.full(m_scr.shape, -jnp.inf, F32)
        l_scr[...] = jnp.zeros(l_scr.shape, F32)
        acc_scr[...] = jnp.zeros(acc_scr.shape, F32)

    @pl.when(ki <= last)
    def _():
        s = _dot_nt(q_ref[...], k_ref[...]) * scale
        qpos = qi * tq + lax.broadcasted_iota(jnp.int32, (tq, tk), 0)
        kpos = ki * tk + lax.broadcasted_iota(jnp.int32, (tq, tk), 1)
        s = jnp.where(kpos <= qpos, s, -jnp.inf)
        m_prev = m_scr[...]
        m_new = jnp.maximum(m_prev, jnp.max(s, axis=-1, keepdims=True))
        alpha = jnp.exp(m_prev - m_new)
        p = jnp.exp(s - m_new)
        l_scr[...] = alpha * l_scr[...] + jnp.sum(p, axis=-1, keepdims=True)
        acc_scr[...] = alpha * acc_scr[...] + _dot(p.astype(BF16), v_ref[...])
        m_scr[...] = m_new

    @pl.when(ki == last)
    def _():
        o_ref[...] = (acc_scr[...] / l_scr[...]).astype(o_ref.dtype)


def _rope_tables(S, half, theta, lane_off):
    inv = theta ** (-jnp.arange(half, dtype=F32) / half)
    ang = jnp.arange(S, dtype=F32)[:, None] * inv
    cos = jnp.zeros((S, LANES), F32).at[:, lane_off:lane_off + 2 * half].set(jnp.tile(jnp.cos(ang), (1, 2)))
    sin = jnp.zeros((S, LANES), F32).at[:, lane_off:lane_off + 2 * half].set(jnp.tile(jnp.sin(ang), (1, 2)))
    return cos, sin


def _rot_half_cols(w, half):
    return jnp.concatenate([-w[..., half:], w[..., :half]], axis=-1)


def _pad_heads(w, heads, width):
    K = w.shape[0]
    w = w.reshape(K, heads, width)
    return jnp.pad(w, ((0, 0), (0, 0), (0, LANES - width))).reshape(K, heads * LANES)


def _mla(x, g, w_in, q_norm, w_qb, kv_norm, w_kvb, w_out, B, S, tm=512, tq=512, tk=512):
    T, D = x.shape
    H = MLA_HEADS
    half = MLA_ROPE // 2
    o1 = MLA_Q_LORA + MLA_KV_LORA
    w_r = w_in[:, o1:]
    tile_r = jnp.zeros((D, LANES), F32).at[:, MLA_NOPE:MLA_NOPE + MLA_ROPE].set(w_r)
    tile_rr = jnp.zeros((D, LANES), F32).at[:, MLA_NOPE:MLA_NOPE + MLA_ROPE].set(_rot_half_cols(w_r, half))
    w_lat = jnp.concatenate([w_in[:, :o1], tile_r, tile_rr], axis=1).astype(BF16)
    lat = _norm_linear(x, g, w_lat, F32, tn=w_lat.shape[1])

    wq3 = w_qb.reshape(MLA_Q_LORA, H, MLA_NOPE + MLA_ROPE)
    wq_rot = jnp.concatenate([jnp.zeros_like(wq3[..., :MLA_NOPE]), _rot_half_cols(wq3[..., MLA_NOPE:], half)], axis=-1)
    wq = _pad_heads(w_qb, H, MLA_NOPE + MLA_ROPE).astype(BF16)
    wqr = _pad_heads(wq_rot.reshape(MLA_Q_LORA, -1), H, MLA_NOPE + MLA_ROPE).astype(BF16)
    wkv3 = w_kvb.reshape(MLA_KV_LORA, H, MLA_NOPE + MLA_V)
    wk = _pad_heads(wkv3[..., :MLA_NOPE].reshape(MLA_KV_LORA, -1), H, MLA_NOPE).astype(BF16)
    wv = _pad_heads(wkv3[..., MLA_NOPE:].reshape(MLA_KV_LORA, -1), H, MLA_V).astype(BF16)
    cos, sin = _rope_tables(S, half, MLA_THETA, MLA_NOPE)

    nS = S // tm
    full = lambda shp: pl.BlockSpec(shp, lambda i: (0,) * len(shp))
    hm_spec = pl.BlockSpec((H, tm, LANES), lambda i: (0, i, 0))
    hm_shape = jax.ShapeDtypeStruct((H, T, LANES), BF16)
    q, k, v = pl.pallas_call(
        _mla_qkv_kernel,
        grid=(T // tm,),
        in_specs=[pl.BlockSpec((tm, lat.shape[1]), lambda i: (i, 0)),
                  full((1, MLA_Q_LORA)), full((1, MLA_KV_LORA)),
                  full(wq.shape), full(wqr.shape), full(wk.shape), full(wv.shape),
                  pl.BlockSpec((tm, LANES), lambda i: (i % nS, 0)),
                  pl.BlockSpec((tm, LANES), lambda i: (i % nS, 0))],
        out_specs=[hm_spec] * 3,
        out_shape=[hm_shape] * 3,
        compiler_params=_cparams("parallel"),
    )(lat, q_norm.reshape(1, -1), kv_norm.reshape(1, -1), wq, wqr, wk, wv, cos, sin)

    q4, k4, v4 = (t.reshape(H, B, S, LANES) for t in (q, k, v))
    scale = float(MLA_NOPE + MLA_ROPE) ** -0.5

    def kmap(b, h, qi, ki):
        return (h, b, jnp.minimum(ki, ((qi + 1) * tq - 1) // tk), 0)

    o = pl.pallas_call(
        functools.partial(_mla_attn_kernel, tq=tq, tk=tk, scale=scale),
        grid=(B, H, S // tq, S // tk),
        in_specs=[pl.BlockSpec((None, None, tq, LANES), lambda b, h, qi, ki: (h, b, qi, 0)),
                  pl.BlockSpec((None, None, tk, LANES), kmap),
                  pl.BlockSpec((None, None, tk, LANES), kmap)],
        out_specs=pl.BlockSpec((None, None, tq, LANES), lambda b, h, qi, ki: (h, b, qi, 0)),
        out_shape=jax.ShapeDtypeStruct((H, B, S, LANES), BF16),
        scratch_shapes=[pltpu.VMEM((tq, 1), F32), pltpu.VMEM((tq, 1), F32), pltpu.VMEM((tq, LANES), F32)],
        compiler_params=_cparams("parallel", "parallel", "parallel", "arbitrary"),
    )(q4, k4, v4)

    w_o = jnp.pad(w_out.reshape(H, MLA_V, D), ((0, 0), (0, LANES - MLA_V), (0, 0))).reshape(H * LANES, D)
    return _linear_res(o.reshape(H, T, LANES), w_o.astype(BF16), x, head_major=True)


def _ret_kernel(q_ref, k_ref, v_ref, g_ref, cos_ref, sin_ref, gnw_ref, y_ref, state_scr, *, C):
    @pl.when(pl.program_id(1) == 0)
    def _():
        state_scr[...] = jnp.zeros(state_scr.shape, F32)

    half = RET_QK // 2
    cos = cos_ref[...]
    sin = sin_ref[...]
    ii = lax.broadcasted_iota(jnp.int32, (C, C), 0)
    jj = lax.broadcasted_iota(jnp.int32, (C, C), 1)
    diff = (ii - jj).astype(F32)
    causal = ii >= jj
    idx = lax.broadcasted_iota(jnp.int32, (C, 1), 0).astype(F32)

    def rope(t):
        t1, t2 = t[:, :half], t[:, half:]
        return jnp.concatenate([t1 * cos - t2 * sin, t1 * sin + t2 * cos], axis=1)

    for h in range(RET_HEADS):
        lg = math.log1p(-(2.0 ** (-5.0 - h)))
        intra = jnp.exp(jnp.where(causal, lg * diff, -jnp.inf))
        q_dec = jnp.exp(lg * (idx + 1.0))
        k_dec = jnp.exp(lg * (C - 1.0 - idx))
        c_dec = math.exp(lg * C)
        q = rope(q_ref[:, h * RET_QK:(h + 1) * RET_QK].astype(F32))
        k = rope(k_ref[:, h * RET_QK:(h + 1) * RET_QK].astype(F32)) * (RET_QK ** -0.5)
        v = v_ref[:, h * RET_V:(h + 1) * RET_V]
        st = state_scr[h]
        a = _dot_nt(q.astype(BF16), k.astype(BF16)) * intra
        o = _dot(a.astype(BF16), v) + _dot((q * q_dec).astype(BF16), st.astype(BF16))
        state_scr[h] = st * c_dec + _dot((k * k_dec).T.astype(BF16), v)
        mu = jnp.mean(o, axis=-1, keepdims=True)
        oc = o - mu
        var = jnp.mean(oc * oc, axis=-1, keepdims=True)
        on = oc * lax.rsqrt(var + EPS) * gnw_ref[:, h * RET_V:(h + 1) * RET_V]
        gt = g_ref[:, h * RET_V:(h + 1) * RET_V].astype(F32)
        y_ref[:, h * RET_V:(h + 1) * RET_V] = (gt * jax.nn.sigmoid(gt) * on).astype(y_ref.dtype)


def _retention(x, g, w_in, gn_w, w_out, B, S, C=128):
    T, D = x.shape
    nqk, nv = RET_HEADS * RET_QK, RET_HEADS * RET_V
    proj = _norm_linear(x, g, w_in.astype(BF16), BF16)
    half = RET_QK // 2
    ang = jnp.arange(S, dtype=F32)[:, None] * (RET_THETA ** (-jnp.arange(half, dtype=F32) / half))
    cos, sin = jnp.cos(ang), jnp.sin(ang)
    nc = S // C
    y = pl.pallas_call(
        functools.partial(_ret_kernel, C=C),
        grid=(B, nc),
        in_specs=[pl.BlockSpec((C, nqk), lambda b, c: (b * nc + c, 0)),
                  pl.BlockSpec((C, nqk), lambda b, c: (b * nc + c, 1)),
                  pl.BlockSpec((C, nv), lambda b, c: (b * nc + c, (2 * nqk) // nv)),
                  pl.BlockSpec((C, nv), lambda b, c: (b * nc + c, (2 * nqk) // nv + 1)),
                  pl.BlockSpec((C, RET_QK // 2), lambda b, c: (c, 0)),
                  pl.BlockSpec((C, RET_QK // 2), lambda b, c: (c, 0)),
                  pl.BlockSpec((1, nv), lambda b, c: (0, 0))],
        out_specs=pl.BlockSpec((C, nv), lambda b, c: (b * nc + c, 0)),
        out_shape=jax.ShapeDtypeStruct((T, nv), BF16),
        scratch_shapes=[pltpu.VMEM((RET_HEADS, RET_QK, RET_V), F32)],
        compiler_params=_cparams("parallel", "arbitrary"),
    )(proj, proj, proj, proj, cos, sin, gn_w.reshape(1, nv))
    return _linear_res(y, w_out.astype(BF16), x)


def _split3(a):
    a1 = a.astype(BF16)
    r = a - a1.astype(F32)
    a2 = r.astype(BF16)
    a3 = (r - a2.astype(F32)).astype(BF16)
    return a1, a2, a3


SSD_COLB = 1024


def _ssd_kernel(*refs, C):
    nz, nx = SSD_D_INNER // SSD_COLB, SSD_CONV_DIM // SSD_COLB
    z_refs, x_refs = refs[:nz], refs[nz:nz + nx]
    (dt_ref, cw_ref, cb_ref, dtb_ref, alog_ref, dsk_ref, nw_ref, y_ref,
     xe_scr, act_scr, st_scr, y_scr) = refs[nz + nx:]
    G, N, P = SSD_GROUPS, SSD_STATE, SSD_P
    DI = SSD_D_INNER
    pairs_per_group = (SSD_HEADS // G) // 2
    HALO = 8
    tiles_per_colb = SSD_COLB // LANES

    @pl.when(pl.program_id(1) == 0)
    def _():
        xe_scr[0:HALO, :] = jnp.zeros((HALO, xe_scr.shape[1]), F32)
        st_scr[...] = jnp.zeros(st_scr.shape, F32)

    for kx in range(nx):
        xe_scr[HALO:HALO + C, kx * SSD_COLB:(kx + 1) * SSD_COLB] = x_refs[kx][...].astype(F32)
    CB = 512
    for cb in range(SSD_CONV_DIM // CB):
        cs = slice(cb * CB, (cb + 1) * CB)
        acc = jnp.broadcast_to(cb_ref[:, cs], (C, CB))
        for kk in range(SSD_CONV):
            off = HALO - (SSD_CONV - 1) + kk
            acc = acc + cw_ref[kk:kk + 1, cs] * xe_scr[off:off + C, cs]
        act_scr[:, cs] = (acc * jax.nn.sigmoid(acc)).astype(BF16)
    xe_scr[0:HALO, :] = xe_scr[C:C + HALO, :]

    dtr = dt_ref[...] + dtb_ref[...]
    dt = jnp.maximum(dtr, 0.0) + jnp.log1p(jnp.exp(-jnp.abs(dtr)))
    a = dt * (-jnp.exp(alog_ref[...]))
    ii = lax.broadcasted_iota(jnp.int32, (C, C), 0)
    jj = lax.broadcasted_iota(jnp.int32, (C, C), 1)
    causal = ii >= jj
    tril = jnp.where(causal, 1.0, 0.0).astype(BF16)
    a1, a2, a3 = _split3(a)
    a_cs = _dot(tril, a1) + _dot(tril, a2) + _dot(tril, a3)
    a_end = a_cs[C - 1:C, :]
    w_end = jnp.exp(a_end - a_cs) * dt
    ea = jnp.exp(a_cs)
    eend = jnp.exp(a_end)
    a_cs_t = a_cs.T
    dt_t = dt.T
    lane = lax.broadcasted_iota(jnp.int32, (C, LANES), 1)
    lo = lane < P

    def pair(t, ha):
        return jnp.where(lo[:t.shape[0]], t[:, ha:ha + 1], t[:, ha + 1:ha + 2])

    for g in range(G):
        bm = act_scr[:, DI + g * N:DI + (g + 1) * N]
        cm = act_scr[:, DI + G * N + g * N:DI + G * N + (g + 1) * N]
        cbm = _dot_nt(cm, bm)
        bm_t = bm.astype(F32).T.astype(BF16)
        ssq = jnp.zeros((C, 1), F32)
        for pj in range(pairs_per_group):
            j = g * pairs_per_group + pj
            ha = 2 * j
            cs = slice(j * LANES, (j + 1) * LANES)
            xp = act_scr[:, cs]
            ys = []
            for hh in (ha, ha + 1):
                seg = a_cs[:, hh:hh + 1] - a_cs_t[hh:hh + 1, :]
                dec = jnp.exp(jnp.where(causal, seg, -jnp.inf))
                mm = cbm * dec * dt_t[hh:hh + 1, :]
                ys.append(_dot(mm.astype(BF16), xp))
            y = jnp.where(lo, ys[0], ys[1])
            st = st_scr[:, cs]
            y = y + pair(ea, ha) * _dot(cm, st.astype(BF16))
            xpf = xp.astype(F32)
            xw = (xpf * pair(w_end, ha)).astype(BF16)
            st_scr[:, cs] = st * pair(eend, ha) + _dot(bm_t, xw)
            y = y + pair(dsk_ref[...], ha) * xpf
            jz = j % tiles_per_colb
            zt = z_refs[j // tiles_per_colb][:, jz * LANES:(jz + 1) * LANES].astype(F32)
            y = y * (zt * jax.nn.sigmoid(zt))
            ssq = ssq + jnp.sum(y * y, axis=-1, keepdims=True)
            y_scr[:, cs] = y
        gw = pairs_per_group * LANES
        gs = slice(g * gw, (g + 1) * gw)
        rs = lax.rsqrt(ssq / float(gw) + EPS)
        y_ref[:, gs] = (y_scr[:, gs] * rs * nw_ref[:, gs]).astype(y_ref.dtype)


def _pad_lanes(v):
    return jnp.pad(v.astype(F32), (0, LANES - v.shape[0])).reshape(1, LANES)


def _ssd(x, g, w_in, conv_w, conv_b, dt_bias, A_log, D_skip, norm_w, w_out, B, S, C=128):
    T, D = x.shape
    DI, CD = SSD_D_INNER, SSD_CONV_DIM
    w_main = w_in[:, :DI + CD].astype(BF16)
    w_dt = jnp.pad(w_in[:, DI + CD:], ((0, 0), (0, LANES - SSD_HEADS))).astype(BF16)
    proj = _norm_linear(x, g, w_main, BF16)
    dt = _norm_linear(x, g, w_dt, F32)
    nc = S // C
    full = lambda shp: pl.BlockSpec(shp, lambda b, c: (0,) * len(shp))
    ncolb = (DI + CD) // SSD_COLB

    def colb(kc):
        return pl.BlockSpec((C, SSD_COLB), lambda b, c: (b * nc + c, kc))

    y = pl.pallas_call(
        functools.partial(_ssd_kernel, C=C),
        grid=(B, nc),
        in_specs=[colb(kc) for kc in range(ncolb)] + [
                  pl.BlockSpec((C, LANES), lambda b, c: (b * nc + c, 0)),
                  full((SSD_CONV, CD)), full((1, CD)), full((1, LANES)), full((1, LANES)), full((1, LANES)),
                  full((1, DI))],
        out_specs=pl.BlockSpec((C, DI), lambda b, c: (b * nc + c, 0)),
        out_shape=jax.ShapeDtypeStruct((T, DI), BF16),
        scratch_shapes=[pltpu.VMEM((C + 8, CD), F32), pltpu.VMEM((C, CD), BF16),
                        pltpu.VMEM((SSD_STATE, DI), F32), pltpu.VMEM((C, DI), F32)],
        compiler_params=_cparams("parallel", "arbitrary"),
    )(*([proj] * ncolb), dt, conv_w, conv_b.reshape(1, CD), _pad_lanes(dt_bias), _pad_lanes(A_log),
      _pad_lanes(D_skip), norm_w.reshape(1, DI))
    return _linear_res(y, w_out.astype(BF16), x)


def _router_kernel(x_ref, g_ref, w1_ref, w2_ref, b_ref, h_ref, route_ref, cnt_ref, carry_scr):
    E = MOE_EXPERTS

    @pl.when(pl.program_id(0) == 0)
    def _():
        carry_scr[...] = jnp.zeros(carry_scr.shape, F32)

    h = _rms(x_ref[...], g_ref[...])
    h_ref[...] = h
    tm = h.shape[0]
    h1 = h.astype(BF16)
    h2 = (h - h1.astype(F32)).astype(BF16)
    logits = _dot(h1, w1_ref[...]) + _dot(h1, w2_ref[...]) + _dot(h2, w1_ref[...]) + b_ref[...]
    lane = lax.broadcasted_iota(jnp.int32, (tm, LANES), 1)
    lanef = lane.astype(F32)
    big = float(LANES)
    glog = jnp.where((lane >= E) & (lane < E + MOE_GROUPS), logits, -jnp.inf)
    gmax = jnp.max(glog, axis=-1, keepdims=True)
    gval = 1.0 / jnp.sum(jnp.exp(glog - gmax), axis=-1, keepdims=True)
    gidx = jnp.min(jnp.where(glog == gmax, lanef, big), axis=-1, keepdims=True) - float(E)
    in_group = (lane < E) & ((lane // MOE_EPG).astype(F32) == gidx)
    el = jnp.where(in_group, logits, -jnp.inf)
    v1 = jnp.max(el, axis=-1, keepdims=True)
    i1 = jnp.min(jnp.where(el == v1, lanef, big), axis=-1, keepdims=True)
    el2 = jnp.where(lanef == i1, -jnp.inf, el)
    v2 = jnp.max(el2, axis=-1, keepdims=True)
    i2 = jnp.min(jnp.where(el2 == v2, lanef, big), axis=-1, keepdims=True)
    e2 = jnp.exp(v2 - v1)
    wa = gval / (1.0 + e2)
    wb = gval * e2 / (1.0 + e2)
    oh1 = jnp.where(lanef == i1, 1.0, 0.0)
    oh2 = jnp.where(lanef == i2, 1.0, 0.0)
    ohs = oh1 + oh2
    ii = lax.broadcasted_iota(jnp.int32, (tm, tm), 0)
    jj = lax.broadcasted_iota(jnp.int32, (tm, tm), 1)
    stril = jnp.where(ii > jj, 1.0, 0.0).astype(BF16)
    before = _dot(stril, ohs.astype(BF16)) + carry_scr[0:1, :]
    r1 = jnp.sum(before * oh1, axis=-1, keepdims=True)
    r2 = jnp.sum(before * oh2, axis=-1, keepdims=True)
    carry = carry_scr[0:1, :] + jnp.sum(ohs, axis=0, keepdims=True)
    carry_scr[...] = jnp.broadcast_to(carry, carry_scr.shape)
    cnt_ref[...] = jnp.broadcast_to(carry, cnt_ref.shape)
    route = jnp.zeros((tm, LANES), F32)
    for ln, val in enumerate((i1, i2, wa, wb, r1, r2)):
        route = jnp.where(lane == ln, val, route)
    route_ref[...] = route


def _dispatch_kernel(pos_ref, h_hbm, xg_in, xg_hbm, sem):
    del xg_in
    base = pl.program_id(0) * TOK_TILE

    def issue(t, c):
        tok = base + t
        for k in range(MOE_TOPK):
            pltpu.make_async_copy(h_hbm.at[pl.ds(tok, 1)], xg_hbm.at[pl.ds(pos_ref[MOE_TOPK * tok + k], 1)],
                                  sem).start()
        return c

    lax.fori_loop(0, TOK_TILE, issue, 0)

    def drain(t, c):
        for k in range(MOE_TOPK):
            pltpu.make_async_copy(h_hbm.at[pl.ds(0, 1)], xg_hbm.at[pl.ds(0, 1)], sem).wait()
        return c

    lax.fori_loop(0, TOK_TILE, drain, 0)


def _expert_kernel(be_ref, nu_ref, x_ref, wg_ref, wu_ref, wd_ref, o_ref):
    del be_ref
    used = pl.program_id(0) < nu_ref[0]

    @pl.when(used)
    def _():
        xb = x_ref[...].astype(BF16)
        hg = _dot(xb, wg_ref[...])
        hu = _dot(xb, wu_ref[...])
        o_ref[...] = _dot((hg * jax.nn.sigmoid(hg) * hu).astype(BF16), wd_ref[...])

    @pl.when(jnp.logical_not(used))
    def _():
        o_ref[...] = jnp.zeros(o_ref.shape, F32)


def _ple_kernel(pos_ref, x_ref, rt_ref, p_ref, g_ref, gf_ref, wg_ref, wp_ref, yb_hbm, o_ref,
                buf_a, buf_b, sem, *, final):
    tm = x_ref.shape[0]
    base = pl.program_id(0) * tm
    bufs = (buf_a, buf_b)

    def issue(t, c):
        for k in range(MOE_TOPK):
            pltpu.make_async_copy(yb_hbm.at[pl.ds(pos_ref[MOE_TOPK * (base + t) + k], 1)],
                                  bufs[k].at[pl.ds(t, 1)], sem).start()
        return c

    lax.fori_loop(0, tm, issue, 0)
    ple = _dot(p_ref[...].astype(BF16), wp_ref[...])

    def drain(t, c):
        for k in range(MOE_TOPK):
            pltpu.make_async_copy(yb_hbm.at[pl.ds(0, 1)], bufs[k].at[pl.ds(0, 1)], sem).wait()
        return c

    lax.fori_loop(0, tm, drain, 0)
    rt = rt_ref[...]
    x1 = x_ref[...] + rt[:, 2:3] * buf_a[...] + rt[:, 3:4] * buf_b[...]
    gate = jax.nn.sigmoid(_dot(_rms(x1, g_ref[...]).astype(BF16), wg_ref[...]))
    out = x1 + ple * gate
    if final:
        out = _rms(out, gf_ref[...])
    o_ref[...] = out


def _moe_ple(x, p_i, norm_moe, w_group, b_group, w_expert, b_expert, w_gate, w_up, w_down,
             norm_ple, ple_w, ple_w_gate, norm_final, final, tm=512):
    T, D = x.shape
    E, BLK = MOE_EXPERTS, MOE_BLOCK
    A = T * MOE_TOPK
    nb = A // BLK + E
    P = nb * BLK

    w_r = jnp.zeros((D, LANES), F32).at[:, :E].set(w_expert).at[:, E:E + MOE_GROUPS].set(w_group)
    b_r = jnp.zeros((1, LANES), F32).at[0, :E].set(b_expert).at[0, E:E + MOE_GROUPS].set(b_group)
    w_r1 = w_r.astype(BF16)
    w_r2 = (w_r - w_r1.astype(F32)).astype(BF16)
    full = lambda shp: pl.BlockSpec(shp, lambda i: (0,) * len(shp))
    h, route, cnt = pl.pallas_call(
        _router_kernel,
        grid=(T // tm,),
        in_specs=[pl.BlockSpec((tm, D), lambda i: (i, 0)), full((1, D)), full((D, LANES)), full((D, LANES)),
                  full((1, LANES))],
        out_specs=[pl.BlockSpec((tm, D), lambda i: (i, 0)), pl.BlockSpec((tm, LANES), lambda i: (i, 0)),
                   full((8, LANES))],
        out_shape=[jax.ShapeDtypeStruct((T, D), F32), jax.ShapeDtypeStruct((T, LANES), F32),
                   jax.ShapeDtypeStruct((8, LANES), F32)],
        scratch_shapes=[pltpu.VMEM((8, LANES), F32)],
        compiler_params=_cparams("arbitrary"),
    )(x, norm_moe.reshape(1, D), w_r1, w_r2, b_r)

    counts = cnt[0, :E].astype(jnp.int32)
    padded = (counts + BLK - 1) // BLK * BLK
    pad_end = jnp.cumsum(padded)
    pad_start = pad_end - padded
    expert = route[:, 0:MOE_TOPK].astype(jnp.int32)
    rank = route[:, 4:4 + MOE_TOPK].astype(jnp.int32)
    pos = (pad_start[expert] + rank).reshape(A)
    blk_expert = jnp.minimum(jnp.searchsorted(pad_end, jnp.arange(nb, dtype=jnp.int32) * BLK, side='right'),
                             E - 1).astype(jnp.int32)
    n_used = (pad_end[-1:] // BLK).astype(jnp.int32)

    xg = pl.pallas_call(
        _dispatch_kernel,
        grid_spec=pltpu.PrefetchScalarGridSpec(
            num_scalar_prefetch=1, grid=(T // TOK_TILE,),
            in_specs=[pl.BlockSpec(memory_space=pl.ANY), pl.BlockSpec(memory_space=pl.ANY)],
            out_specs=pl.BlockSpec(memory_space=pl.ANY),
            scratch_shapes=[pltpu.SemaphoreType.DMA]),
        out_shape=jax.ShapeDtypeStruct((P, D), F32),
        input_output_aliases={2: 0},
        compiler_params=_cparams("arbitrary"),
    )(pos, h, jnp.zeros((P, D), F32))

    def wmap(i, be, nu):
        return (be[jnp.minimum(i, nu[0] - 1)], 0, 0)

    def xmap(i, be, nu):
        return (jnp.minimum(i, nu[0] - 1), 0)

    yb = pl.pallas_call(
        _expert_kernel,
        grid_spec=pltpu.PrefetchScalarGridSpec(
            num_scalar_prefetch=2, grid=(nb,),
            in_specs=[pl.BlockSpec((BLK, D), xmap),
                      pl.BlockSpec((None, D, MOE_FF), wmap), pl.BlockSpec((None, D, MOE_FF), wmap),
                      pl.BlockSpec((None, MOE_FF, D), wmap)],
            out_specs=pl.BlockSpec((BLK, D), lambda i, be, nu: (i, 0))),
        out_shape=jax.ShapeDtypeStruct((P, D), F32),
        compiler_params=_cparams("arbitrary"),
    )(blk_expert, n_used, xg, w_gate.astype(BF16), w_up.astype(BF16), w_down.astype(BF16))

    tp = TOK_TILE
    PD = p_i.shape[1]
    return pl.pallas_call(
        functools.partial(_ple_kernel, final=final),
        grid_spec=pltpu.PrefetchScalarGridSpec(
            num_scalar_prefetch=1, grid=(T // tp,),
            in_specs=[pl.BlockSpec((tp, D), lambda i, ps: (i, 0)),
                      pl.BlockSpec((tp, LANES), lambda i, ps: (i, 0)),
                      pl.BlockSpec((tp, PD), lambda i, ps: (i, 0)),
                      pl.BlockSpec((1, D), lambda i, ps: (0, 0)), pl.BlockSpec((1, D), lambda i, ps: (0, 0)),
                      pl.BlockSpec((D, D), lambda i, ps: (0, 0)), pl.BlockSpec((PD, D), lambda i, ps: (0, 0)),
                      pl.BlockSpec(memory_space=pl.ANY)],
            out_specs=pl.BlockSpec((tp, D), lambda i, ps: (i, 0)),
            scratch_shapes=[pltpu.VMEM((tp, D), F32), pltpu.VMEM((tp, D), F32), pltpu.SemaphoreType.DMA]),
        out_shape=jax.ShapeDtypeStruct((T, D), F32),
        compiler_params=_cparams("arbitrary"),
    )(pos, x, route, p_i, norm_ple.reshape(1, D), norm_final.reshape(1, D),
      ple_w_gate.astype(BF16), ple_w.astype(BF16), yb)


def kernel(x, p, norm_mix, norm_moe, norm_ple, norm_final, dsa_w_in, dsa_w_out, mla_w_in, mla_q_norm, mla_w_qb,
           mla_kv_norm, mla_w_kvb, mla_w_out, ret_w_in, ret_gn_w, ret_w_out, ssd_w_in, ssd_conv_w, ssd_conv_b,
           ssd_dt_bias, ssd_A_log, ssd_D, ssd_norm_w, ssd_w_out, moe_w_group, moe_b_group, moe_w_expert,
           moe_b_expert, moe_w_gate, moe_w_up, moe_w_down, ple_w, ple_w_gate):
    B, S, D = x.shape
    depth = p.shape[0]
    xt = x.reshape(B * S, D)
    for i in range(depth):
        kind = i % 4
        if kind == 0:
            xt = _dilated_attention(xt, norm_mix[i], dsa_w_in, dsa_w_out, B, S)
        elif kind == 1:
            xt = _mla(xt, norm_mix[i], mla_w_in, mla_q_norm, mla_w_qb, mla_kv_norm, mla_w_kvb, mla_w_out, B, S)
        elif kind == 2:
            xt = _retention(xt, norm_mix[i], ret_w_in, ret_gn_w, ret_w_out, B, S)
        else:
            xt = _ssd(xt, norm_mix[i], ssd_w_in, ssd_conv_w, ssd_conv_b, ssd_dt_bias, ssd_A_log, ssd_D,
                      ssd_norm_w, ssd_w_out, B, S)
        xt = _moe_ple(xt, p[i].reshape(B * S, -1), norm_moe[i], moe_w_group[i], moe_b_group[i], moe_w_expert[i],
                      moe_b_expert[i], moe_w_gate[i], moe_w_up[i], moe_w_down[i], norm_ple[i], ple_w[i],
                      ple_w_gate[i], norm_final, final=(i == depth - 1))
    return xt.reshape(B, S, D)
```

```python
import functools
import math

import jax
import jax.numpy as jnp
from jax import lax
from jax.experimental import pallas as pl
from jax.experimental.pallas import tpu as pltpu

F32 = jnp.float32
BF16 = jnp.bfloat16
EPS = 1e-6
LANES = 128
VMEM_LIMIT = 48 * 1024 * 1024

DSA_PATTERNS = ((128, 1), (512, 4), (2048, 16))
DSA_HEADS = 8
DSA_SPAN = 2048
MLA_HEADS = 16
MLA_Q_LORA, MLA_KV_LORA, MLA_NOPE, MLA_ROPE, MLA_V = 256, 128, 64, 32, 64
MLA_THETA = 10000.0
RET_HEADS, RET_QK, RET_V = 4, 256, 512
RET_THETA = 10000.0
SSD_D_INNER, SSD_HEADS, SSD_GROUPS, SSD_STATE, SSD_CONV, SSD_P = 2048, 32, 4, 128, 4, 64
SSD_CONV_DIM = SSD_D_INNER + 2 * SSD_GROUPS * SSD_STATE
MOE_GROUPS, MOE_EPG, MOE_EXPERTS, MOE_TOPK, MOE_FF = 4, 8, 32, 2, 512
MOE_BLOCK = 256
TOK_TILE = 256


def _cparams(*sem):
    return pltpu.CompilerParams(dimension_semantics=sem, vmem_limit_bytes=VMEM_LIMIT)


def _rms(x, g):
    return x * lax.rsqrt(jnp.mean(x * x, axis=-1, keepdims=True) + EPS) * g


def _dot(a, b):
    return jnp.dot(a, b, preferred_element_type=F32)


def _dot_nt(a, b):
    return lax.dot_general(a, b, (((1,), (1,)), ((), ())), preferred_element_type=F32)


def _norm_linear_kernel(x_ref, g_ref, w_ref, o_ref, h_scr, *, head_major):
    @pl.when(pl.program_id(1) == 0)
    def _():
        h_scr[...] = _rms(x_ref[...], g_ref[...]).astype(BF16)

    r = _dot(h_scr[...], w_ref[...])
    if head_major:
        for c in range(r.shape[1] // LANES):
            o_ref[c] = r[:, c * LANES:(c + 1) * LANES].astype(o_ref.dtype)
    else:
        o_ref[...] = r.astype(o_ref.dtype)


def _norm_linear(x, g, w, out_dtype, *, head_major=False, tm=512, tn=1024):
    T, D = x.shape
    N = w.shape[1]
    tn = min(tn, N)
    assert T % tm == 0 and N % tn == 0 and tn % LANES == 0
    if head_major:
        out_shape = jax.ShapeDtypeStruct((N // LANES, T, LANES), out_dtype)
        out_spec = pl.BlockSpec((tn // LANES, tm, LANES), lambda i, j: (j, i, 0))
    else:
        out_shape = jax.ShapeDtypeStruct((T, N), out_dtype)
        out_spec = pl.BlockSpec((tm, tn), lambda i, j: (i, j))
    return pl.pallas_call(
        functools.partial(_norm_linear_kernel, head_major=head_major),
        grid=(T // tm, N // tn),
        in_specs=[pl.BlockSpec((tm, D), lambda i, j: (i, 0)),
                  pl.BlockSpec((1, D), lambda i, j: (0, 0)),
                  pl.BlockSpec((D, tn), lambda i, j: (0, j))],
        out_specs=out_spec,
        out_shape=out_shape,
        scratch_shapes=[pltpu.VMEM((tm, D), BF16)],
        compiler_params=_cparams("parallel", "arbitrary"),
    )(x, g.reshape(1, D), w)


def _linear_res_kernel(a_ref, w_ref, r_ref, o_ref, *, head_major):
    if head_major:
        a = jnp.concatenate([a_ref[c] for c in range(a_ref.shape[0])], axis=1)
    else:
        a = a_ref[...]
    o_ref[...] = r_ref[...] + _dot(a, w_ref[...])


def _linear_res(a, w, res, *, head_major=False, tm=512):
    T, D = res.shape
    K = w.shape[0]
    if head_major:
        a_spec = pl.BlockSpec((K // LANES, tm, LANES), lambda i: (0, i, 0))
    else:
        a_spec = pl.BlockSpec((tm, K), lambda i: (i, 0))
    return pl.pallas_call(
        functools.partial(_linear_res_kernel, head_major=head_major),
        grid=(T // tm,),
        in_specs=[a_spec,
                  pl.BlockSpec((K, D), lambda i: (0, 0)),
                  pl.BlockSpec((tm, D), lambda i: (i, 0))],
        out_specs=pl.BlockSpec((tm, D), lambda i: (i, 0)),
        out_shape=jax.ShapeDtypeStruct((T, D), F32),
        compiler_params=_cparams("parallel"),
    )(a, w, res)


def _dsa_group_kernel(sl_ref, q_ref, k_ref, kh_ref, v_ref, vh_ref, o_ref, sm_ref, sd_ref, *, dil, rows):
    W = LANES
    i = pl.program_id(1)
    h = pl.program_id(2)
    scale = float(W) ** -0.5
    slope = sl_ref[h] * float(dil)
    qi = lax.broadcasted_iota(jnp.int32, (W, 2 * W), 0)
    kj = lax.broadcasted_iota(jnp.int32, (W, 2 * W), 1)
    steps = W + qi - kj
    band = (steps >= 0) & (steps <= W)
    first_lim = jnp.where(i == 0, W, 0)
    band_first = band & (kj >= first_lim)
    bias = (-slope) * steps.astype(F32)
    lane = lax.broadcasted_iota(jnp.int32, (W, LANES), 1)

    @pl.when(h == 0)
    def _():
        sm_ref[...] = jnp.zeros(sm_ref.shape, F32)
        sd_ref[...] = jnp.ones(sd_ref.shape, F32)

    for r in range(dil):
        cs = slice(r * LANES, (r + 1) * LANES)
        for n in range(rows // W):
            rs = slice(n * W, (n + 1) * W)
            q = q_ref[rs, cs]
            if n == 0:
                kk = jnp.concatenate([kh_ref[:, cs], k_ref[rs, cs]], axis=0)
                vv = jnp.concatenate([vh_ref[:, cs], v_ref[rs, cs]], axis=0)
                valid = band_first
            else:
                kk = k_ref[(n - 1) * W:(n + 1) * W, cs]
                vv = v_ref[(n - 1) * W:(n + 1) * W, cs]
                valid = band
            s = _dot_nt(q, kk) * scale + bias
            s = jnp.where(valid, s, -jnp.inf)
            m = jnp.max(s, axis=-1, keepdims=True)
            e = jnp.exp(s - m)
            den = jnp.sum(e, axis=-1, keepdims=True)
            o = _dot(e.astype(BF16), vv) / den
            o_ref[rs, cs] = o.astype(o_ref.dtype)
            sm_ref[rs, cs] = jnp.where(lane == h, m, sm_ref[rs, cs])
            sd_ref[rs, cs] = jnp.where(lane == h, den, sd_ref[rs, cs])


def _dsa_proj_kernel(x_ref, g_ref, w_ref, o_ref, h_scr, r_scr, *, dil):
    @pl.when(pl.program_id(1) == 0)
    def _():
        h_scr[...] = _rms(x_ref[...], g_ref[...]).astype(BF16)

    res = _dot(h_scr[...], w_ref[...])
    rows = r_scr.shape[1] // dil
    for c in range(o_ref.shape[0]):
        if dil == 1:
            o_ref[c] = res[:, c * LANES:(c + 1) * LANES].astype(o_ref.dtype)
            continue
        r_scr[c] = res[:, c * LANES:(c + 1) * LANES]
        for r in range(dil):
            o_ref[c, :, r * LANES:(r + 1) * LANES] = r_scr[c, pl.ds(r, rows, stride=dil), :].astype(o_ref.dtype)


def _dsa_proj(x, g, w, dil, tm=512):
    T, D = x.shape
    N = w.shape[1]
    tn = DSA_HEADS * LANES
    return pl.pallas_call(
        functools.partial(_dsa_proj_kernel, dil=dil),
        grid=(T // tm, N // tn),
        in_specs=[pl.BlockSpec((tm, D), lambda i, j: (i, 0)),
                  pl.BlockSpec((1, D), lambda i, j: (0, 0)),
                  pl.BlockSpec((D, tn), lambda i, j: (0, j))],
        out_specs=pl.BlockSpec((DSA_HEADS, tm // dil, dil * LANES), lambda i, j: (j, i, 0)),
        out_shape=jax.ShapeDtypeStruct((N // LANES, T // dil, dil * LANES), BF16),
        scratch_shapes=[pltpu.VMEM((tm, D), BF16), pltpu.VMEM((DSA_HEADS, tm, LANES), F32)],
        compiler_params=_cparams("parallel", "arbitrary"),
    )(x, g.reshape(1, D), w)


def _dsa_group(qkv, slopes, dil, B, S):
    H = DSA_HEADS
    L = S // dil
    rows = DSA_SPAN // dil
    width = dil * LANES
    qv = qkv.reshape(qkv.shape[0], B, L, width)
    hb = rows // LANES
    base = 0

    def qmap(o):
        return lambda b, i, h, sl: (base + o * H + h, b, i, 0)

    def hmap(o):
        return lambda b, i, h, sl: (base + o * H + h, b, jnp.maximum(i * hb - 1, 0), 0)

    blk = (None, None, rows, width)
    hblk = (None, None, LANES, width)
    o, sm, sd = pl.pallas_call(
        functools.partial(_dsa_group_kernel, dil=dil, rows=rows),
        grid_spec=pltpu.PrefetchScalarGridSpec(
            num_scalar_prefetch=1,
            grid=(B, L // rows, H),
            in_specs=[pl.BlockSpec(blk, qmap(0)),
                      pl.BlockSpec(blk, qmap(1)), pl.BlockSpec(hblk, hmap(1)),
                      pl.BlockSpec(blk, qmap(2)), pl.BlockSpec(hblk, hmap(2))],
            out_specs=[pl.BlockSpec(blk, lambda b, i, h, sl: (h, b, i, 0)),
                       pl.BlockSpec((None, rows, width), lambda b, i, h, sl: (b, i, 0)),
                       pl.BlockSpec((None, rows, width), lambda b, i, h, sl: (b, i, 0))]),
        out_shape=[jax.ShapeDtypeStruct((H, B, L, width), BF16),
                   jax.ShapeDtypeStruct((B, L, width), F32),
                   jax.ShapeDtypeStruct((B, L, width), F32)],
        compiler_params=_cparams("parallel", "parallel", "arbitrary"),
    )(slopes, qv, qv, qv, qv, qv)
    T = B * S
    return o.reshape(H, T, LANES), sm.reshape(T, LANES), sd.reshape(T, LANES)


def _dsa_out_kernel(o0, o1, o2, m0, m1, m2, d0, d1, d2, w_ref, x_ref, out_ref):
    ms = [m0[...], m1[...], m2[...]]
    mx = jnp.maximum(jnp.maximum(ms[0], ms[1]), ms[2])
    ws = [d[...] * jnp.exp(m - mx) for d, m in zip((d0, d1, d2), ms)]
    tot = ws[0] + ws[1] + ws[2]
    ws = [w / tot for w in ws]
    parts = []
    for h in range(DSA_HEADS):
        acc = ws[0][:, h:h + 1] * o0[h].astype(F32)
        acc = acc + ws[1][:, h:h + 1] * o1[h].astype(F32)
        acc = acc + ws[2][:, h:h + 1] * o2[h].astype(F32)
        parts.append(acc.astype(BF16))
    a = jnp.concatenate(parts, axis=1)
    out_ref[...] = x_ref[...] + _dot(a, w_ref[...])


def _dsa_out(os_, ms, ds, w, x, tm=512):
    T, D = x.shape
    H = DSA_HEADS
    ospec = pl.BlockSpec((H, tm, LANES), lambda i: (0, i, 0))
    sspec = pl.BlockSpec((tm, LANES), lambda i: (i, 0))
    return pl.pallas_call(
        _dsa_out_kernel,
        grid=(T // tm,),
        in_specs=[ospec] * 3 + [sspec] * 6 + [pl.BlockSpec((H * LANES, D), lambda i: (0, 0)),
                                              pl.BlockSpec((tm, D), lambda i: (i, 0))],
        out_specs=pl.BlockSpec((tm, D), lambda i: (i, 0)),
        out_shape=jax.ShapeDtypeStruct((T, D), F32),
        compiler_params=_cparams("parallel"),
    )(*os_, *ms, *ds, w, x)


def _dilated_attention(x, g, w_in, w_out, B, S):
    ng = len(DSA_PATTERNS)
    gw = 3 * DSA_HEADS * LANES
    slopes = 2.0 ** (-8.0 * jnp.arange(1, ng * DSA_HEADS + 1, dtype=F32) / (ng * DSA_HEADS))
    os_, ms, ds = [], [], []
    for gi, (window, dil) in enumerate(DSA_PATTERNS):
        assert window // dil == LANES
        qkv = _dsa_proj(x, g, w_in[:, gi * gw:(gi + 1) * gw].astype(BF16), dil)
        o, m, d = _dsa_group(qkv, slopes[gi * DSA_HEADS:(gi + 1) * DSA_HEADS], dil, B, S)
        os_.append(o)
        ms.append(m)
        ds.append(d)
    return _dsa_out(os_, ms, ds, w_out.astype(BF16), x)


def _mla_qkv_kernel(lat_ref, qn_ref, kvn_ref, wq_ref, wqr_ref, wk_ref, wv_ref, cos_ref, sin_ref,
                    q_ref, k_ref, vt_ref):
    lat = lat_ref[...]
    cos = cos_ref[...]
    sin = sin_ref[...]
    hq = _rms(lat[:, :MLA_Q_LORA], qn_ref[...]).astype(BF16)
    hkv = _rms(lat[:, MLA_Q_LORA:MLA_Q_LORA + MLA_KV_LORA], kvn_ref[...]).astype(BF16)
    o1 = MLA_Q_LORA + MLA_KV_LORA
    kr = lat[:, o1:o1 + LANES] * cos + lat[:, o1 + LANES:o1 + 2 * LANES] * sin
    q = _dot(hq, wq_ref[...])
    qr = _dot(hq, wqr_ref[...])
    k = _dot(hkv, wk_ref[...])
    v = _dot(hkv, wv_ref[...])
    lane = lax.broadcasted_iota(jnp.int32, cos.shape, 1)
    scale = float(MLA_NOPE + MLA_ROPE) ** -0.5 * math.log2(math.e)
    cos_q = jnp.where(lane < MLA_NOPE, 1.0, cos) * scale
    sin_q = sin * scale
    for h in range(MLA_HEADS):
        cs = slice(h * LANES, (h + 1) * LANES)
        q_ref[h] = (q[:, cs] * cos_q + qr[:, cs] * sin_q).astype(BF16)
        k_ref[h] = (k[:, cs] + kr).astype(BF16)
        vt_ref[h] = jnp.where(lane == MLA_V, 1.0, v[:, cs]).T.astype(BF16)


def _mla_attn_kernel(q_ref, k_ref, vt_ref, o_ref, m_scr, acc_scr, sa_scr, sb_scr, *, tq, tk):
    qi = pl.program_id(2)
    q = q_ref[...]
    m_scr[...] = jnp.full(m_scr.shape, -jnp.inf, F32)
    acc_scr[...] = jnp.zeros(acc_scr.shape, F32)

    def scores(j, buf):
        buf[...] = _dot_nt(k_ref[pl.ds(pl.multiple_of(j * tk, tk), tk), :], q)

    def update(j, buf, masked):
        off = pl.multiple_of(j * tk, tk)
        st = buf[...]
        if masked:
            kpos = off + lax.broadcasted_iota(jnp.int32, (tk, tq), 0)
            qpos = qi * tq + lax.broadcasted_iota(jnp.int32, (tk, tq), 1)
            st = jnp.where(kpos <= qpos, st, -jnp.inf)
        m_prev = m_scr[...]
        m_new = jnp.maximum(m_prev, jnp.max(st, axis=0, keepdims=True))
        p = jnp.exp2(st - m_new).astype(BF16)
        acc_scr[...] = jnp.exp2(m_prev - m_new) * acc_scr[...] + _dot(vt_ref[:, pl.ds(off, tk)], p)
        m_scr[...] = m_new

    scores(0, sa_scr)

    def body(jj, c):
        j = 2 * jj
        scores(j + 1, sb_scr)
        update(j, sa_scr, False)
        scores(j + 2, sa_scr)
        update(j + 1, sb_scr, False)
        return c

    lax.fori_loop(0, qi // 2, body, 0)

    @pl.when(qi % 2 == 0)
    def _():
        update(qi, sa_scr, True)

    @pl.when(qi % 2 == 1)
    def _():
        scores(qi, sb_scr)
        update(qi - 1, sa_scr, False)
        update(qi, sb_scr, True)

    acc = acc_scr[...].T
    o_ref[...] = (acc / acc[:, MLA_V:MLA_V + 1]).astype(o_ref.dtype)


def _rope_tables(S, half, theta, lane_off):
    inv = theta ** (-jnp.arange(half, dtype=F32) / half)
    ang = jnp.arange(S, dtype=F32)[:, None] * inv
    cos = jnp.zeros((S, LANES), F32).at[:, lane_off:lane_off + 2 * half].set(jnp.tile(jnp.cos(ang), (1, 2)))
    sin = jnp.zeros((S, LANES), F32).at[:, lane_off:lane_off + 2 * half].set(jnp.tile(jnp.sin(ang), (1, 2)))
    return cos, sin


def _rot_half_cols(w, half):
    return jnp.concatenate([-w[..., half:], w[..., :half]], axis=-1)


def _pad_heads(w, heads, width):
    K = w.shape[0]
    w = w.reshape(K, heads, width)
    return jnp.pad(w, ((0, 0), (0, 0), (0, LANES - width))).reshape(K, heads * LANES)


def _mla(x, g, w_in, q_norm, w_qb, kv_norm, w_kvb, w_out, B, S, tm=512, tq=512, tk=512):
    T, D = x.shape
    H = MLA_HEADS
    half = MLA_ROPE // 2
    o1 = MLA_Q_LORA + MLA_KV_LORA
    w_r = w_in[:, o1:]
    tile_r = jnp.zeros((D, LANES), F32).at[:, MLA_NOPE:MLA_NOPE + MLA_ROPE].set(w_r)
    tile_rr = jnp.zeros((D, LANES), F32).at[:, MLA_NOPE:MLA_NOPE + MLA_ROPE].set(_rot_half_cols(w_r, half))
    w_lat = jnp.concatenate([w_in[:, :o1], tile_r, tile_rr], axis=1).astype(BF16)
    lat = _norm_linear(x, g, w_lat, F32, tn=w_lat.shape[1])

    wq3 = w_qb.reshape(MLA_Q_LORA, H, MLA_NOPE + MLA_ROPE)
    wq_rot = jnp.concatenate([jnp.zeros_like(wq3[..., :MLA_NOPE]), _rot_half_cols(wq3[..., MLA_NOPE:], half)], axis=-1)
    wq = _pad_heads(w_qb, H, MLA_NOPE + MLA_ROPE).astype(BF16)
    wqr = _pad_heads(wq_rot.reshape(MLA_Q_LORA, -1), H, MLA_NOPE + MLA_ROPE).astype(BF16)
    wkv3 = w_kvb.reshape(MLA_KV_LORA, H, MLA_NOPE + MLA_V)
    wk = _pad_heads(wkv3[..., :MLA_NOPE].reshape(MLA_KV_LORA, -1), H, MLA_NOPE).astype(BF16)
    wv = _pad_heads(wkv3[..., MLA_NOPE:].reshape(MLA_KV_LORA, -1), H, MLA_V).astype(BF16)
    cos, sin = _rope_tables(S, half, MLA_THETA, MLA_NOPE)

    nS = S // tm
    full = lambda shp: pl.BlockSpec(shp, lambda i: (0,) * len(shp))
    hm_spec = pl.BlockSpec((H, tm, LANES), lambda i: (0, i, 0))
    hm_shape = jax.ShapeDtypeStruct((H, T, LANES), BF16)
    q, k, v = pl.pallas_call(
        _mla_qkv_kernel,
        grid=(T // tm,),
        in_specs=[pl.BlockSpec((tm, lat.shape[1]), lambda i: (i, 0)),
                  full((1, MLA_Q_LORA)), full((1, MLA_KV_LORA)),
                  full(wq.shape), full(wqr.shape), full(wk.shape), full(wv.shape),
                  pl.BlockSpec((tm, LANES), lambda i: (i % nS, 0)),
                  pl.BlockSpec((tm, LANES), lambda i: (i % nS, 0))],
        out_specs=[hm_spec, hm_spec, pl.BlockSpec((H, LANES, tm), lambda i: (0, 0, i))],
        out_shape=[hm_shape, hm_shape, jax.ShapeDtypeStruct((H, LANES, T), BF16)],
        compiler_params=_cparams("parallel"),
    )(lat, q_norm.reshape(1, -1), kv_norm.reshape(1, -1), wq, wqr, wk, wv, cos, sin)

    q4, k4 = (t.reshape(H, B, S, LANES) for t in (q, k))
    assert tq == tk
    o = pl.pallas_call(
        functools.partial(_mla_attn_kernel, tq=tq, tk=tk),
        grid=(B, H, S // tq),
        in_specs=[pl.BlockSpec((None, None, tq, LANES), lambda b, h, qi: (h, b, qi, 0)),
                  pl.BlockSpec((None, None, S, LANES), lambda b, h, qi: (h, b, 0, 0)),
                  pl.BlockSpec((None, LANES, S), lambda b, h, qi: (h, 0, b))],
        out_specs=pl.BlockSpec((None, None, tq, LANES), lambda b, h, qi: (h, b, qi, 0)),
        out_shape=jax.ShapeDtypeStruct((H, B, S, LANES), BF16),
        scratch_shapes=[pltpu.VMEM((1, tq), F32), pltpu.VMEM((LANES, tq), F32),
                        pltpu.VMEM((tk, tq), F32), pltpu.VMEM((tk, tq), F32)],
        compiler_params=_cparams("parallel", "parallel", "arbitrary"),
    )(q4, k4, v)

    w_o = jnp.pad(w_out.reshape(H, MLA_V, D), ((0, 0), (0, LANES - MLA_V), (0, 0))).reshape(H * LANES, D)
    return _linear_res(o.reshape(H, T, LANES), w_o.astype(BF16), x, head_major=True)


def _ret_kernel(q_ref, k_ref, v_ref, g_ref, cos_ref, sin_ref, gnw_ref, y_ref, state_scr, *, C):
    @pl.when(pl.program_id(1) == 0)
    def _():
        state_scr[...] = jnp.zeros(state_scr.shape, F32)

    half = RET_QK // 2
    cos = cos_ref[...]
    sin = sin_ref[...]
    ii = lax.broadcasted_iota(jnp.int32, (C, C), 0)
    jj = lax.broadcasted_iota(jnp.int32, (C, C), 1)
    diff = (ii - jj).astype(F32)
    causal = ii >= jj
    idx = lax.broadcasted_iota(jnp.int32, (C, 1), 0).astype(F32)

    def rope(t):
        t1, t2 = t[:, :half], t[:, half:]
        return jnp.concatenate([t1 * cos - t2 * sin, t1 * sin + t2 * cos], axis=1)

    for h in range(RET_HEADS):
        lg = math.log1p(-(2.0 ** (-5.0 - h)))
        intra = jnp.exp(jnp.where(causal, lg * diff, -jnp.inf))
        q_dec = jnp.exp(lg * (idx + 1.0))
        k_dec = jnp.exp(lg * (C - 1.0 - idx))
        c_dec = math.exp(lg * C)
        q = rope(q_ref[:, h * RET_QK:(h + 1) * RET_QK].astype(F32))
        k = rope(k_ref[:, h * RET_QK:(h + 1) * RET_QK].astype(F32)) * (RET_QK ** -0.5)
        v = v_ref[:, h * RET_V:(h + 1) * RET_V]
        st = state_scr[h]
        a = _dot_nt(q.astype(BF16), k.astype(BF16)) * intra
        o = _dot(a.astype(BF16), v) + _dot((q * q_dec).astype(BF16), st.astype(BF16))
        state_scr[h] = st * c_dec + _dot((k * k_dec).T.astype(BF16), v)
        mu = jnp.mean(o, axis=-1, keepdims=True)
        oc = o - mu
        var = jnp.mean(oc * oc, axis=-1, keepdims=True)
        on = oc * lax.rsqrt(var + EPS) * gnw_ref[:, h * RET_V:(h + 1) * RET_V]
        gt = g_ref[:, h * RET_V:(h + 1) * RET_V].astype(F32)
        y_ref[:, h * RET_V:(h + 1) * RET_V] = (gt * jax.nn.sigmoid(gt) * on).astype(y_ref.dtype)


def _retention(x, g, w_in, gn_w, w_out, B, S, C=128):
    T, D = x.shape
    nqk, nv = RET_HEADS * RET_QK, RET_HEADS * RET_V
    proj = _norm_linear(x, g, w_in.astype(BF16), BF16)
    half = RET_QK // 2
    ang = jnp.arange(S, dtype=F32)[:, None] * (RET_THETA ** (-jnp.arange(half, dtype=F32) / half))
    cos, sin = jnp.cos(ang), jnp.sin(ang)
    nc = S // C
    y = pl.pallas_call(
        functools.partial(_ret_kernel, C=C),
        grid=(B, nc),
        in_specs=[pl.BlockSpec((C, nqk), lambda b, c: (b * nc + c, 0)),
                  pl.BlockSpec((C, nqk), lambda b, c: (b * nc + c, 1)),
                  pl.BlockSpec((C, nv), lambda b, c: (b * nc + c, (2 * nqk) // nv)),
                  pl.BlockSpec((C, nv), lambda b, c: (b * nc + c, (2 * nqk) // nv + 1)),
                  pl.BlockSpec((C, RET_QK // 2), lambda b, c: (c, 0)),
                  pl.BlockSpec((C, RET_QK // 2), lambda b, c: (c, 0)),
                  pl.BlockSpec((1, nv), lambda b, c: (0, 0))],
        out_specs=pl.BlockSpec((C, nv), lambda b, c: (b * nc + c, 0)),
        out_shape=jax.ShapeDtypeStruct((T, nv), BF16),
        scratch_shapes=[pltpu.VMEM((RET_HEADS, RET_QK, RET_V), F32)],
        compiler_params=_cparams("parallel", "arbitrary"),
    )(proj, proj, proj, proj, cos, sin, gn_w.reshape(1, nv))
    return _linear_res(y, w_out.astype(BF16), x)


def _split3(a):
    a1 = a.astype(BF16)
    r = a - a1.astype(F32)
    a2 = r.astype(BF16)
    a3 = (r - a2.astype(F32)).astype(BF16)
    return a1, a2, a3


SSD_COLB = 1024


def _ssd_kernel(*refs, C):
    nz, nx = SSD_D_INNER // SSD_COLB, SSD_CONV_DIM // SSD_COLB
    z_refs, x_refs = refs[:nz], refs[nz:nz + nx]
    (dt_ref, cw_ref, cb_ref, dtb_ref, alog_ref, dsk_ref, nw_ref, y_ref,
     xe_scr, act_scr, st_scr, y_scr) = refs[nz + nx:]
    G, N, P = SSD_GROUPS, SSD_STATE, SSD_P
    DI = SSD_D_INNER
    pairs_per_group = (SSD_HEADS // G) // 2
    HALO = 8
    tiles_per_colb = SSD_COLB // LANES

    @pl.when(pl.program_id(1) == 0)
    def _():
        xe_scr[0:HALO, :] = jnp.zeros((HALO, xe_scr.shape[1]), F32)
        st_scr[...] = jnp.zeros(st_scr.shape, F32)

    for kx in range(nx):
        xe_scr[HALO:HALO + C, kx * SSD_COLB:(kx + 1) * SSD_COLB] = x_refs[kx][...].astype(F32)
    CB = 512
    for cb in range(SSD_CONV_DIM // CB):
        cs = slice(cb * CB, (cb + 1) * CB)
        acc = jnp.broadcast_to(cb_ref[:, cs], (C, CB))
        for kk in range(SSD_CONV):
            off = HALO - (SSD_CONV - 1) + kk
            acc = acc + cw_ref[kk:kk + 1, cs] * xe_scr[off:off + C, cs]
        act_scr[:, cs] = (acc * jax.nn.sigmoid(acc)).astype(BF16)
    xe_scr[0:HALO, :] = xe_scr[C:C + HALO, :]

    dtr = dt_ref[...] + dtb_ref[...]
    dt = jnp.maximum(dtr, 0.0) + jnp.log1p(jnp.exp(-jnp.abs(dtr)))
    a = dt * (-jnp.exp(alog_ref[...]))
    ii = lax.broadcasted_iota(jnp.int32, (C, C), 0)
    jj = lax.broadcasted_iota(jnp.int32, (C, C), 1)
    causal = ii >= jj
    tril = jnp.where(causal, 1.0, 0.0).astype(BF16)
    a1, a2, a3 = _split3(a)
    a_cs = _dot(tril, a1) + _dot(tril, a2) + _dot(tril, a3)
    a_end = a_cs[C - 1:C, :]
    w_end = jnp.exp(a_end - a_cs) * dt
    ea = jnp.exp(a_cs)
    eend = jnp.exp(a_end)
    a_cs_t = a_cs.T
    dt_t = dt.T
    lane = lax.broadcasted_iota(jnp.int32, (C, LANES), 1)
    lo = lane < P

    def pair(t, ha):
        return jnp.where(lo[:t.shape[0]], t[:, ha:ha + 1], t[:, ha + 1:ha + 2])

    for g in range(G):
        bm = act_scr[:, DI + g * N:DI + (g + 1) * N]
        cm = act_scr[:, DI + G * N + g * N:DI + G * N + (g + 1) * N]
        cbm = _dot_nt(cm, bm)
        bm_t = bm.astype(F32).T.astype(BF16)
        ssq = jnp.zeros((C, 1), F32)
        for pj in range(pairs_per_group):
            j = g * pairs_per_group + pj
            ha = 2 * j
            cs = slice(j * LANES, (j + 1) * LANES)
            xp = act_scr[:, cs]
            ys = []
            for hh in (ha, ha + 1):
                seg = a_cs[:, hh:hh + 1] - a_cs_t[hh:hh + 1, :]
                dec = jnp.exp(jnp.where(causal, seg, -jnp.inf))
                mm = cbm * dec * dt_t[hh:hh + 1, :]
                ys.append(_dot(mm.astype(BF16), xp))
            y = jnp.where(lo, ys[0], ys[1])
            st = st_scr[:, cs]
            y = y + pair(ea, ha) * _dot(cm, st.astype(BF16))
            xpf = xp.astype(F32)
            xw = (xpf * pair(w_end, ha)).astype(BF16)
            st_scr[:, cs] = st * pair(eend, ha) + _dot(bm_t, xw)
            y = y + pair(dsk_ref[...], ha) * xpf
            jz = j % tiles_per_colb
            zt = z_refs[j // tiles_per_colb][:, jz * LANES:(jz + 1) * LANES].astype(F32)
            y = y * (zt * jax.nn.sigmoid(zt))
            ssq = ssq + jnp.sum(y * y, axis=-1, keepdims=True)
            y_scr[:, cs] = y
        gw = pairs_per_group * LANES
        gs = slice(g * gw, (g + 1) * gw)
        rs = lax.rsqrt(ssq / float(gw) + EPS)
        y_ref[:, gs] = (y_scr[:, gs] * rs * nw_ref[:, gs]).astype(y_ref.dtype)


def _pad_lanes(v):
    return jnp.pad(v.astype(F32), (0, LANES - v.shape[0])).reshape(1, LANES)


def _ssd(x, g, w_in, conv_w, conv_b, dt_bias, A_log, D_skip, norm_w, w_out, B, S, C=128):
    T, D = x.shape
    DI, CD = SSD_D_INNER, SSD_CONV_DIM
    w_main = w_in[:, :DI + CD].astype(BF16)
    w_dt = jnp.pad(w_in[:, DI + CD:], ((0, 0), (0, LANES - SSD_HEADS))).astype(BF16)
    proj = _norm_linear(x, g, w_main, BF16)
    dt = _norm_linear(x, g, w_dt, F32)
    nc = S // C
    full = lambda shp: pl.BlockSpec(shp, lambda b, c: (0,) * len(shp))
    ncolb = (DI + CD) // SSD_COLB

    def colb(kc):
        return pl.BlockSpec((C, SSD_COLB), lambda b, c: (b * nc + c, kc))

    y = pl.pallas_call(
        functools.partial(_ssd_kernel, C=C),
        grid=(B, nc),
        in_specs=[colb(kc) for kc in range(ncolb)] + [
                  pl.BlockSpec((C, LANES), lambda b, c: (b * nc + c, 0)),
                  full((SSD_CONV, CD)), full((1, CD)), full((1, LANES)), full((1, LANES)), full((1, LANES)),
                  full((1, DI))],
        out_specs=pl.BlockSpec((C, DI), lambda b, c: (b * nc + c, 0)),
        out_shape=jax.ShapeDtypeStruct((T, DI), BF16),
        scratch_shapes=[pltpu.VMEM((C + 8, CD), F32), pltpu.VMEM((C, CD), BF16),
                        pltpu.VMEM((SSD_STATE, DI), F32), pltpu.VMEM((C, DI), F32)],
        compiler_params=_cparams("parallel", "arbitrary"),
    )(*([proj] * ncolb), dt, conv_w, conv_b.reshape(1, CD), _pad_lanes(dt_bias), _pad_lanes(A_log),
      _pad_lanes(D_skip), norm_w.reshape(1, DI))
    return _linear_res(y, w_out.astype(BF16), x)


def _router_kernel(x_ref, g_ref, w1_ref, w2_ref, b_ref, h_ref, route_ref, cnt_ref, carry_scr):
    E = MOE_EXPERTS

    @pl.when(pl.program_id(0) == 0)
    def _():
        carry_scr[...] = jnp.zeros(carry_scr.shape, F32)

    h = _rms(x_ref[...], g_ref[...])
    h_ref[...] = h
    tm = h.shape[0]
    h1 = h.astype(BF16)
    h2 = (h - h1.astype(F32)).astype(BF16)
    logits = _dot(h1, w1_ref[...]) + _dot(h1, w2_ref[...]) + _dot(h2, w1_ref[...]) + b_ref[...]
    lane = lax.broadcasted_iota(jnp.int32, (tm, LANES), 1)
    lanef = lane.astype(F32)
    big = float(LANES)
    glog = jnp.where((lane >= E) & (lane < E + MOE_GROUPS), logits, -jnp.inf)
    gmax = jnp.max(glog, axis=-1, keepdims=True)
    gval = 1.0 / jnp.sum(jnp.exp(glog - gmax), axis=-1, keepdims=True)
    gidx = jnp.min(jnp.where(glog == gmax, lanef, big), axis=-1, keepdims=True) - float(E)
    in_group = (lane < E) & ((lane // MOE_EPG).astype(F32) == gidx)
    el = jnp.where(in_group, logits, -jnp.inf)
    v1 = jnp.max(el, axis=-1, keepdims=True)
    i1 = jnp.min(jnp.where(el == v1, lanef, big), axis=-1, keepdims=True)
    el2 = jnp.where(lanef == i1, -jnp.inf, el)
    v2 = jnp.max(el2, axis=-1, keepdims=True)
    i2 = jnp.min(jnp.where(el2 == v2, lanef, big), axis=-1, keepdims=True)
    e2 = jnp.exp(v2 - v1)
    wa = gval / (1.0 + e2)
    wb = gval * e2 / (1.0 + e2)
    oh1 = jnp.where(lanef == i1, 1.0, 0.0)
    oh2 = jnp.where(lanef == i2, 1.0, 0.0)
    ohs = oh1 + oh2
    ii = lax.broadcasted_iota(jnp.int32, (tm, tm), 0)
    jj = lax.broadcasted_iota(jnp.int32, (tm, tm), 1)
    stril = jnp.where(ii > jj, 1.0, 0.0).astype(BF16)
    before = _dot(stril, ohs.astype(BF16)) + carry_scr[0:1, :]
    r1 = jnp.sum(before * oh1, axis=-1, keepdims=True)
    r2 = jnp.sum(before * oh2, axis=-1, keepdims=True)
    carry = carry_scr[0:1, :] + jnp.sum(ohs, axis=0, keepdims=True)
    carry_scr[...] = jnp.broadcast_to(carry, carry_scr.shape)
    cnt_ref[...] = jnp.broadcast_to(carry, cnt_ref.shape)
    route = jnp.zeros((tm, LANES), F32)
    for ln, val in enumerate((i1, i2, wa, wb, r1, r2)):
        route = jnp.where(lane == ln, val, route)
    route_ref[...] = route


def _dispatch_kernel(pos_ref, h_ref, xg_in, xg_hbm, sem):
    del xg_in
    base = pl.program_id(0) * TOK_TILE

    def issue(t, c):
        for k in range(MOE_TOPK):
            pltpu.make_async_copy(h_ref.at[pl.ds(t, 1)],
                                  xg_hbm.at[pl.ds(pos_ref[MOE_TOPK * (base + t) + k], 1)], sem).start()
        return c

    lax.fori_loop(0, TOK_TILE, issue, 0, unroll=8)

    def drain(t, c):
        for k in range(MOE_TOPK):
            pltpu.make_async_copy(h_ref.at[pl.ds(0, 1)], xg_hbm.at[pl.ds(0, 1)], sem).wait()
        return c

    lax.fori_loop(0, TOK_TILE, drain, 0, unroll=8)


def _expert_kernel(be_ref, nu_ref, x_ref, wg_ref, wu_ref, wd_ref, o_ref):
    del be_ref
    used = pl.program_id(0) < nu_ref[0]

    @pl.when(used)
    def _():
        xb = x_ref[...].astype(BF16)
        hg = _dot(xb, wg_ref[...])
        hu = _dot(xb, wu_ref[...])
        o_ref[...] = _dot((hg * jax.nn.sigmoid(hg) * hu).astype(BF16), wd_ref[...])

    @pl.when(jnp.logical_not(used))
    def _():
        o_ref[...] = jnp.zeros(o_ref.shape, F32)


def _ple_kernel(pos_ref, x_ref, rt_ref, p_ref, g_ref, gf_ref, wg_ref, wp_ref, yb_hbm, o_ref,
                buf_a, buf_b, sem, *, final):
    tm = x_ref.shape[0]
    base = pl.program_id(0) * tm
    bufs = (buf_a, buf_b)

    def issue(t, c):
        for k in range(MOE_TOPK):
            pltpu.make_async_copy(yb_hbm.at[pl.ds(pos_ref[MOE_TOPK * (base + t) + k], 1)],
                                  bufs[k].at[pl.ds(t, 1)], sem).start()
        return c

    lax.fori_loop(0, tm, issue, 0, unroll=8)
    ple = _dot(p_ref[...].astype(BF16), wp_ref[...])

    def drain(t, c):
        for k in range(MOE_TOPK):
            pltpu.make_async_copy(yb_hbm.at[pl.ds(0, 1)], bufs[k].at[pl.ds(0, 1)], sem).wait()
        return c

    lax.fori_loop(0, tm, drain, 0, unroll=8)
    rt = rt_ref[...]
    x1 = x_ref[...] + rt[:, 2:3] * buf_a[...] + rt[:, 3:4] * buf_b[...]
    gate = jax.nn.sigmoid(_dot(_rms(x1, g_ref[...]).astype(BF16), wg_ref[...]))
    out = x1 + ple * gate
    if final:
        out = _rms(out, gf_ref[...])
    o_ref[...] = out


def _moe_ple(x, p_i, norm_moe, w_group, b_group, w_expert, b_expert, w_gate, w_up, w_down,
             norm_ple, ple_w, ple_w_gate, norm_final, final, tm=512):
    T, D = x.shape
    E, BLK = MOE_EXPERTS, MOE_BLOCK
    A = T * MOE_TOPK
    nb = A // BLK + E
    P = nb * BLK

    w_r = jnp.zeros((D, LANES), F32).at[:, :E].set(w_expert).at[:, E:E + MOE_GROUPS].set(w_group)
    b_r = jnp.zeros((1, LANES), F32).at[0, :E].set(b_expert).at[0, E:E + MOE_GROUPS].set(b_group)
    w_r1 = w_r.astype(BF16)
    w_r2 = (w_r - w_r1.astype(F32)).astype(BF16)
    full = lambda shp: pl.BlockSpec(shp, lambda i: (0,) * len(shp))
    h, route, cnt = pl.pallas_call(
        _router_kernel,
        grid=(T // tm,),
        in_specs=[pl.BlockSpec((tm, D), lambda i: (i, 0)), full((1, D)), full((D, LANES)), full((D, LANES)),
                  full((1, LANES))],
        out_specs=[pl.BlockSpec((tm, D), lambda i: (i, 0)), pl.BlockSpec((tm, LANES), lambda i: (i, 0)),
                   full((8, LANES))],
        out_shape=[jax.ShapeDtypeStruct((T, D), F32), jax.ShapeDtypeStruct((T, LANES), F32),
                   jax.ShapeDtypeStruct((8, LANES), F32)],
        scratch_shapes=[pltpu.VMEM((8, LANES), F32)],
        compiler_params=_cparams("arbitrary"),
    )(x, norm_moe.reshape(1, D), w_r1, w_r2, b_r)

    counts = cnt[0, :E].astype(jnp.int32)
    padded = (counts + BLK - 1) // BLK * BLK
    pad_end = jnp.cumsum(padded)
    pad_start = pad_end - padded
    expert = route[:, 0:MOE_TOPK].astype(jnp.int32)
    rank = route[:, 4:4 + MOE_TOPK].astype(jnp.int32)
    pos = (pad_start[expert] + rank).reshape(A)
    blk_expert = jnp.minimum(jnp.searchsorted(pad_end, jnp.arange(nb, dtype=jnp.int32) * BLK, side='right'),
                             E - 1).astype(jnp.int32)
    n_used = (pad_end[-1:] // BLK).astype(jnp.int32)

    xg = pl.pallas_call(
        _dispatch_kernel,
        grid_spec=pltpu.PrefetchScalarGridSpec(
            num_scalar_prefetch=1, grid=(T // TOK_TILE,),
            in_specs=[pl.BlockSpec((TOK_TILE, D), lambda i, ps: (i, 0)), pl.BlockSpec(memory_space=pl.ANY)],
            out_specs=pl.BlockSpec(memory_space=pl.ANY),
            scratch_shapes=[pltpu.SemaphoreType.DMA]),
        out_shape=jax.ShapeDtypeStruct((P, D), F32),
        input_output_aliases={2: 0},
        compiler_params=_cparams("arbitrary"),
    )(pos, h, jnp.zeros((P, D), F32))

    def wmap(i, be, nu):
        return (be[jnp.minimum(i, nu[0] - 1)], 0, 0)

    def xmap(i, be, nu):
        return (jnp.minimum(i, nu[0] - 1), 0)

    yb = pl.pallas_call(
        _expert_kernel,
        grid_spec=pltpu.PrefetchScalarGridSpec(
            num_scalar_prefetch=2, grid=(nb,),
            in_specs=[pl.BlockSpec((BLK, D), xmap),
                      pl.BlockSpec((None, D, MOE_FF), wmap), pl.BlockSpec((None, D, MOE_FF), wmap),
                      pl.BlockSpec((None, MOE_FF, D), wmap)],
            out_specs=pl.BlockSpec((BLK, D), lambda i, be, nu: (i, 0))),
        out_shape=jax.ShapeDtypeStruct((P, D), F32),
        compiler_params=_cparams("arbitrary"),
    )(blk_expert, n_used, xg, w_gate.astype(BF16), w_up.astype(BF16), w_down.astype(BF16))

    tp = TOK_TILE
    PD = p_i.shape[1]
    return pl.pallas_call(
        functools.partial(_ple_kernel, final=final),
        grid_spec=pltpu.PrefetchScalarGridSpec(
            num_scalar_prefetch=1, grid=(T // tp,),
            in_specs=[pl.BlockSpec((tp, D), lambda i, ps: (i, 0)),
                      pl.BlockSpec((tp, LANES), lambda i, ps: (i, 0)),
                      pl.BlockSpec((tp, PD), lambda i, ps: (i, 0)),
                      pl.BlockSpec((1, D), lambda i, ps: (0, 0)), pl.BlockSpec((1, D), lambda i, ps: (0, 0)),
                      pl.BlockSpec((D, D), lambda i, ps: (0, 0)), pl.BlockSpec((PD, D), lambda i, ps: (0, 0)),
                      pl.BlockSpec(memory_space=pl.ANY)],
            out_specs=pl.BlockSpec((tp, D), lambda i, ps: (i, 0)),
            scratch_shapes=[pltpu.VMEM((tp, D), F32), pltpu.VMEM((tp, D), F32), pltpu.SemaphoreType.DMA]),
        out_shape=jax.ShapeDtypeStruct((T, D), F32),
        compiler_params=_cparams("arbitrary"),
    )(pos, x, route, p_i, norm_ple.reshape(1, D), norm_final.reshape(1, D),
      ple_w_gate.astype(BF16), ple_w.astype(BF16), yb)


def kernel(x, p, norm_mix, norm_moe, norm_ple, norm_final, dsa_w_in, dsa_w_out, mla_w_in, mla_q_norm, mla_w_qb,
           mla_kv_norm, mla_w_kvb, mla_w_out, ret_w_in, ret_gn_w, ret_w_out, ssd_w_in, ssd_conv_w, ssd_conv_b,
           ssd_dt_bias, ssd_A_log, ssd_D, ssd_norm_w, ssd_w_out, moe_w_group, moe_b_group, moe_w_expert,
           moe_b_expert, moe_w_gate, moe_w_up, moe_w_down, ple_w, ple_w_gate):
    B, S, D = x.shape
    depth = p.shape[0]
    xt = x.reshape(B * S, D)
    for i in range(depth):
        kind = i % 4
        if kind == 0:
            xt = _dilated_attention(xt, norm_mix[i], dsa_w_in, dsa_w_out, B, S)
        elif kind == 1:
            xt = _mla(xt, norm_mix[i], mla_w_in, mla_q_norm, mla_w_qb, mla_kv_norm, mla_w_kvb, mla_w_out, B, S)
        elif kind == 2:
            xt = _retention(xt, norm_mix[i], ret_w_in, ret_gn_w, ret_w_out, B, S)
        else:
            xt = _ssd(xt, norm_mix[i], ssd_w_in, ssd_conv_w, ssd_conv_b, ssd_dt_bias, ssd_A_log, ssd_D,
                      ssd_norm_w, ssd_w_out, B, S)
        xt = _moe_ple(xt, p[i].reshape(B * S, -1), norm_moe[i], moe_w_group[i], moe_b_group[i], moe_w_expert[i],
                      moe_b_expert[i], moe_w_gate[i], moe_w_up[i], moe_w_down[i], norm_ple[i], ple_w[i],
                      ple_w_gate[i], norm_final, final=(i == depth - 1))
    return xt.reshape(B, S, D)
```

```python
import functools
import math

import jax
import jax.numpy as jnp
from jax import lax
from jax.experimental import pallas as pl
from jax.experimental.pallas import tpu as pltpu

F32 = jnp.float32
BF16 = jnp.bfloat16
EPS = 1e-6
LANES = 128
VMEM_LIMIT = 48 * 1024 * 1024

DSA_PATTERNS = ((128, 1), (512, 4), (2048, 16))
DSA_HEADS = 8
DSA_SPAN = 2048
MLA_HEADS = 16
MLA_Q_LORA, MLA_KV_LORA, MLA_NOPE, MLA_ROPE, MLA_V = 256, 128, 64, 32, 64
MLA_THETA = 10000.0
MLA_VT_ROWS = 80
RET_HEADS, RET_QK, RET_V = 4, 256, 512
RET_THETA = 10000.0
SSD_D_INNER, SSD_HEADS, SSD_GROUPS, SSD_STATE, SSD_CONV, SSD_P = 2048, 32, 4, 128, 4, 64
SSD_CONV_DIM = SSD_D_INNER + 2 * SSD_GROUPS * SSD_STATE
MOE_GROUPS, MOE_EPG, MOE_EXPERTS, MOE_TOPK, MOE_FF = 4, 8, 32, 2, 512
MOE_BLOCK = 256
TOK_TILE = 256


def _cparams(*sem):
    return pltpu.CompilerParams(dimension_semantics=sem, vmem_limit_bytes=VMEM_LIMIT)


def _rms(x, g):
    return x * lax.rsqrt(jnp.mean(x * x, axis=-1, keepdims=True) + EPS) * g


def _dot(a, b):
    return jnp.dot(a, b, preferred_element_type=F32)


def _dot_nt(a, b):
    return lax.dot_general(a, b, (((1,), (1,)), ((), ())), preferred_element_type=F32)


def _norm_linear_kernel(x_ref, g_ref, w_ref, o_ref, h_scr, *, head_major):
    @pl.when(pl.program_id(1) == 0)
    def _():
        h_scr[...] = _rms(x_ref[...], g_ref[...]).astype(BF16)

    r = _dot(h_scr[...], w_ref[...])
    if head_major:
        for c in range(r.shape[1] // LANES):
            o_ref[c] = r[:, c * LANES:(c + 1) * LANES].astype(o_ref.dtype)
    else:
        o_ref[...] = r.astype(o_ref.dtype)


def _norm_linear(x, g, w, out_dtype, *, head_major=False, tm=1024, tn=1024):
    T, D = x.shape
    N = w.shape[1]
    tn = min(tn, N)
    assert T % tm == 0 and N % tn == 0 and tn % LANES == 0
    if head_major:
        out_shape = jax.ShapeDtypeStruct((N // LANES, T, LANES), out_dtype)
        out_spec = pl.BlockSpec((tn // LANES, tm, LANES), lambda i, j: (j, i, 0))
    else:
        out_shape = jax.ShapeDtypeStruct((T, N), out_dtype)
        out_spec = pl.BlockSpec((tm, tn), lambda i, j: (i, j))
    return pl.pallas_call(
        functools.partial(_norm_linear_kernel, head_major=head_major),
        grid=(T // tm, N // tn),
        in_specs=[pl.BlockSpec((tm, D), lambda i, j: (i, 0)),
                  pl.BlockSpec((1, D), lambda i, j: (0, 0)),
                  pl.BlockSpec((D, tn), lambda i, j: (0, j))],
        out_specs=out_spec,
        out_shape=out_shape,
        scratch_shapes=[pltpu.VMEM((tm, D), BF16)],
        compiler_params=_cparams("parallel", "arbitrary"),
    )(x, g.reshape(1, D), w)


def _linear_res_kernel(a_ref, w_ref, r_ref, o_ref, *, head_major):
    if head_major:
        a = jnp.concatenate([a_ref[c] for c in range(a_ref.shape[0])], axis=1)
    else:
        a = a_ref[...]
    o_ref[...] = r_ref[...] + _dot(a, w_ref[...])


def _linear_res(a, w, res, *, head_major=False, tm=512):
    T, D = res.shape
    K = w.shape[0]
    if head_major:
        a_spec = pl.BlockSpec((K // LANES, tm, LANES), lambda i: (0, i, 0))
    else:
        a_spec = pl.BlockSpec((tm, K), lambda i: (i, 0))
    return pl.pallas_call(
        functools.partial(_linear_res_kernel, head_major=head_major),
        grid=(T // tm,),
        in_specs=[a_spec,
                  pl.BlockSpec((K, D), lambda i: (0, 0)),
                  pl.BlockSpec((tm, D), lambda i: (i, 0))],
        out_specs=pl.BlockSpec((tm, D), lambda i: (i, 0)),
        out_shape=jax.ShapeDtypeStruct((T, D), F32),
        compiler_params=_cparams("parallel"),
    )(a, w, res)


def _dsa_group_kernel(sl_ref, q_ref, k_ref, kh_ref, v_ref, vh_ref, o_ref, sm_ref, sd_ref, *, dil, rows):
    W = LANES
    i = pl.program_id(1)
    h = pl.program_id(2)
    scale = float(W) ** -0.5
    slope = sl_ref[h] * float(dil)
    qi = lax.broadcasted_iota(jnp.int32, (W, 2 * W), 0)
    kj = lax.broadcasted_iota(jnp.int32, (W, 2 * W), 1)
    steps = W + qi - kj
    band = (steps >= 0) & (steps <= W)
    first_lim = jnp.where(i == 0, W, 0)
    band_first = band & (kj >= first_lim)
    bias = (-slope) * steps.astype(F32)
    lane = lax.broadcasted_iota(jnp.int32, (W, LANES), 1)

    @pl.when(h == 0)
    def _():
        sm_ref[...] = jnp.zeros(sm_ref.shape, F32)
        sd_ref[...] = jnp.ones(sd_ref.shape, F32)

    for r in range(dil):
        cs = slice(r * LANES, (r + 1) * LANES)
        for n in range(rows // W):
            rs = slice(n * W, (n + 1) * W)
            q = q_ref[rs, cs]
            if n == 0:
                kk = jnp.concatenate([kh_ref[:, cs], k_ref[rs, cs]], axis=0)
                vv = jnp.concatenate([vh_ref[:, cs], v_ref[rs, cs]], axis=0)
                valid = band_first
            else:
                kk = k_ref[(n - 1) * W:(n + 1) * W, cs]
                vv = v_ref[(n - 1) * W:(n + 1) * W, cs]
                valid = band
            s = _dot_nt(q, kk) * scale + bias
            s = jnp.where(valid, s, -jnp.inf)
            m = jnp.max(s, axis=-1, keepdims=True)
            e = jnp.exp(s - m)
            den = jnp.sum(e, axis=-1, keepdims=True)
            o = _dot(e.astype(BF16), vv) / den
            o_ref[rs, cs] = o.astype(o_ref.dtype)
            sm_ref[rs, cs] = jnp.where(lane == h, m, sm_ref[rs, cs])
            sd_ref[rs, cs] = jnp.where(lane == h, den, sd_ref[rs, cs])


def _dsa_proj_kernel(x_ref, g_ref, w_ref, o_ref, h_scr, r_scr, *, dil):
    @pl.when(pl.program_id(1) == 0)
    def _():
        h_scr[...] = _rms(x_ref[...], g_ref[...]).astype(BF16)

    res = _dot(h_scr[...], w_ref[...])
    rows = r_scr.shape[1] // dil
    for c in range(o_ref.shape[0]):
        if dil == 1:
            o_ref[c] = res[:, c * LANES:(c + 1) * LANES].astype(o_ref.dtype)
            continue
        r_scr[c] = res[:, c * LANES:(c + 1) * LANES]
        for r in range(dil):
            o_ref[c, :, r * LANES:(r + 1) * LANES] = r_scr[c, pl.ds(r, rows, stride=dil), :].astype(o_ref.dtype)


def _dsa_proj(x, g, w, dil, tm=512):
    T, D = x.shape
    N = w.shape[1]
    tn = DSA_HEADS * LANES
    return pl.pallas_call(
        functools.partial(_dsa_proj_kernel, dil=dil),
        grid=(T // tm, N // tn),
        in_specs=[pl.BlockSpec((tm, D), lambda i, j: (i, 0)),
                  pl.BlockSpec((1, D), lambda i, j: (0, 0)),
                  pl.BlockSpec((D, tn), lambda i, j: (0, j))],
        out_specs=pl.BlockSpec((DSA_HEADS, tm // dil, dil * LANES), lambda i, j: (j, i, 0)),
        out_shape=jax.ShapeDtypeStruct((N // LANES, T // dil, dil * LANES), BF16),
        scratch_shapes=[pltpu.VMEM((tm, D), BF16), pltpu.VMEM((DSA_HEADS, tm, LANES), F32)],
        compiler_params=_cparams("parallel", "arbitrary"),
    )(x, g.reshape(1, D), w)


def _dsa_group(qkv, slopes, dil, B, S):
    H = DSA_HEADS
    L = S // dil
    rows = DSA_SPAN // dil
    width = dil * LANES
    qv = qkv.reshape(qkv.shape[0], B, L, width)
    hb = rows // LANES
    base = 0

    def qmap(o):
        return lambda b, i, h, sl: (base + o * H + h, b, i, 0)

    def hmap(o):
        return lambda b, i, h, sl: (base + o * H + h, b, jnp.maximum(i * hb - 1, 0), 0)

    blk = (None, None, rows, width)
    hblk = (None, None, LANES, width)
    o, sm, sd = pl.pallas_call(
        functools.partial(_dsa_group_kernel, dil=dil, rows=rows),
        grid_spec=pltpu.PrefetchScalarGridSpec(
            num_scalar_prefetch=1,
            grid=(B, L // rows, H),
            in_specs=[pl.BlockSpec(blk, qmap(0)),
                      pl.BlockSpec(blk, qmap(1)), pl.BlockSpec(hblk, hmap(1)),
                      pl.BlockSpec(blk, qmap(2)), pl.BlockSpec(hblk, hmap(2))],
            out_specs=[pl.BlockSpec(blk, lambda b, i, h, sl: (h, b, i, 0)),
                       pl.BlockSpec((None, rows, width), lambda b, i, h, sl: (b, i, 0)),
                       pl.BlockSpec((None, rows, width), lambda b, i, h, sl: (b, i, 0))]),
        out_shape=[jax.ShapeDtypeStruct((H, B, L, width), BF16),
                   jax.ShapeDtypeStruct((B, L, width), F32),
                   jax.ShapeDtypeStruct((B, L, width), F32)],
        compiler_params=_cparams("parallel", "parallel", "arbitrary"),
    )(slopes, qv, qv, qv, qv, qv)
    T = B * S
    return o.reshape(H, T, LANES), sm.reshape(T, LANES), sd.reshape(T, LANES)


def _dsa_out_kernel(o0, o1, o2, m0, m1, m2, d0, d1, d2, w_ref, x_ref, out_ref):
    ms = [m0[...], m1[...], m2[...]]
    mx = jnp.maximum(jnp.maximum(ms[0], ms[1]), ms[2])
    ws = [d[...] * jnp.exp(m - mx) for d, m in zip((d0, d1, d2), ms)]
    tot = ws[0] + ws[1] + ws[2]
    ws = [w / tot for w in ws]
    parts = []
    for h in range(DSA_HEADS):
        acc = ws[0][:, h:h + 1] * o0[h].astype(F32)
        acc = acc + ws[1][:, h:h + 1] * o1[h].astype(F32)
        acc = acc + ws[2][:, h:h + 1] * o2[h].astype(F32)
        parts.append(acc.astype(BF16))
    a = jnp.concatenate(parts, axis=1)
    out_ref[...] = x_ref[...] + _dot(a, w_ref[...])


def _dsa_out(os_, ms, ds, w, x, tm=512):
    T, D = x.shape
    H = DSA_HEADS
    ospec = pl.BlockSpec((H, tm, LANES), lambda i: (0, i, 0))
    sspec = pl.BlockSpec((tm, LANES), lambda i: (i, 0))
    return pl.pallas_call(
        _dsa_out_kernel,
        grid=(T // tm,),
        in_specs=[ospec] * 3 + [sspec] * 6 + [pl.BlockSpec((H * LANES, D), lambda i: (0, 0)),
                                              pl.BlockSpec((tm, D), lambda i: (i, 0))],
        out_specs=pl.BlockSpec((tm, D), lambda i: (i, 0)),
        out_shape=jax.ShapeDtypeStruct((T, D), F32),
        compiler_params=_cparams("parallel"),
    )(*os_, *ms, *ds, w, x)


def _dilated_attention(x, g, w_in, w_out, B, S):
    ng = len(DSA_PATTERNS)
    gw = 3 * DSA_HEADS * LANES
    slopes = 2.0 ** (-8.0 * jnp.arange(1, ng * DSA_HEADS + 1, dtype=F32) / (ng * DSA_HEADS))
    os_, ms, ds = [], [], []
    for gi, (window, dil) in enumerate(DSA_PATTERNS):
        assert window // dil == LANES
        qkv = _dsa_proj(x, g, w_in[:, gi * gw:(gi + 1) * gw].astype(BF16), dil)
        o, m, d = _dsa_group(qkv, slopes[gi * DSA_HEADS:(gi + 1) * DSA_HEADS], dil, B, S)
        os_.append(o)
        ms.append(m)
        ds.append(d)
    return _dsa_out(os_, ms, ds, w_out.astype(BF16), x)


def _mla_qkv_kernel(lat_ref, qn_ref, kvn_ref, wq_ref, wqr_ref, wk_ref, wv_ref, cos_ref, sin_ref,
                    q_ref, k_ref, vt_ref):
    lat = lat_ref[...]
    cos = cos_ref[...]
    sin = sin_ref[...]
    hq = _rms(lat[:, :MLA_Q_LORA], qn_ref[...]).astype(BF16)
    hkv = _rms(lat[:, MLA_Q_LORA:MLA_Q_LORA + MLA_KV_LORA], kvn_ref[...]).astype(BF16)
    o1 = MLA_Q_LORA + MLA_KV_LORA
    kr = lat[:, o1:o1 + LANES] * cos + lat[:, o1 + LANES:o1 + 2 * LANES] * sin
    q = _dot(hq, wq_ref[...])
    qr = _dot(hq, wqr_ref[...])
    k = _dot(hkv, wk_ref[...])
    v = _dot(hkv, wv_ref[...])
    lane = lax.broadcasted_iota(jnp.int32, cos.shape, 1)
    scale = float(MLA_NOPE + MLA_ROPE) ** -0.5 * math.log2(math.e)
    cos_q = jnp.where(lane < MLA_NOPE, 1.0, cos) * scale
    sin_q = sin * scale
    for h in range(MLA_HEADS):
        cs = slice(h * LANES, (h + 1) * LANES)
        q_ref[h] = (q[:, cs] * cos_q + qr[:, cs] * sin_q).astype(BF16)
        k_ref[h] = (k[:, cs] + kr).astype(BF16)
        vt_ref[h] = jnp.where(lane == MLA_V, 1.0, v[:, cs]).T[:MLA_VT_ROWS].astype(BF16)


def _mla_attn_kernel(q_ref, k_ref, vt_ref, o_ref, m_scr, acc_scr, sa_scr, sb_scr, *, tq, tk):
    qi = pl.program_id(2)
    heads = range(q_ref.shape[0])
    m_scr[...] = jnp.full(m_scr.shape, -jnp.inf, F32)
    acc_scr[...] = jnp.zeros(acc_scr.shape, F32)

    def scores(j, buf):
        for h in heads:
            buf[h] = _dot_nt(k_ref[h, pl.ds(pl.multiple_of(j * tk, tk), tk), :], q_ref[h])

    def update(j, buf, masked):
        off = pl.multiple_of(j * tk, tk)
        for h in heads:
            st = buf[h]
            if masked:
                kpos = off + lax.broadcasted_iota(jnp.int32, (tk, tq), 0)
                qpos = qi * tq + lax.broadcasted_iota(jnp.int32, (tk, tq), 1)
                st = jnp.where(kpos <= qpos, st, -jnp.inf)
            m_prev = m_scr[h]
            m_new = jnp.maximum(m_prev, jnp.max(st, axis=0, keepdims=True))
            p = jnp.exp2(st - m_new).astype(BF16)
            acc_scr[h] = jnp.exp2(m_prev - m_new) * acc_scr[h] + _dot(vt_ref[h, :, pl.ds(off, tk)], p)
            m_scr[h] = m_new

    scores(0, sa_scr)

    def body(jj, c):
        j = 2 * jj
        scores(j + 1, sb_scr)
        update(j, sa_scr, False)
        scores(j + 2, sa_scr)
        update(j + 1, sb_scr, False)
        return c

    lax.fori_loop(0, qi // 2, body, 0)

    @pl.when(qi % 2 == 0)
    def _():
        update(qi, sa_scr, True)

    @pl.when(qi % 2 == 1)
    def _():
        scores(qi, sb_scr)
        update(qi - 1, sa_scr, False)
        update(qi, sb_scr, True)

    for h in heads:
        acc = acc_scr[h]
        acc = jnp.concatenate([acc, jnp.zeros((LANES - MLA_VT_ROWS, tq), F32)], axis=0).T
        o_ref[h] = (acc / acc[:, MLA_V:MLA_V + 1]).astype(o_ref.dtype)


def _rope_tables(S, half, theta, lane_off):
    inv = theta ** (-jnp.arange(half, dtype=F32) / half)
    ang = jnp.arange(S, dtype=F32)[:, None] * inv
    cos = jnp.zeros((S, LANES), F32).at[:, lane_off:lane_off + 2 * half].set(jnp.tile(jnp.cos(ang), (1, 2)))
    sin = jnp.zeros((S, LANES), F32).at[:, lane_off:lane_off + 2 * half].set(jnp.tile(jnp.sin(ang), (1, 2)))
    return cos, sin


def _rot_half_cols(w, half):
    return jnp.concatenate([-w[..., half:], w[..., :half]], axis=-1)


def _pad_heads(w, heads, width):
    K = w.shape[0]
    w = w.reshape(K, heads, width)
    return jnp.pad(w, ((0, 0), (0, 0), (0, LANES - width))).reshape(K, heads * LANES)


def _mla(x, g, w_in, q_norm, w_qb, kv_norm, w_kvb, w_out, B, S, tm=512, tq=512, tk=512, hp=2):
    T, D = x.shape
    H = MLA_HEADS
    half = MLA_ROPE // 2
    o1 = MLA_Q_LORA + MLA_KV_LORA
    w_r = w_in[:, o1:]
    tile_r = jnp.zeros((D, LANES), F32).at[:, MLA_NOPE:MLA_NOPE + MLA_ROPE].set(w_r)
    tile_rr = jnp.zeros((D, LANES), F32).at[:, MLA_NOPE:MLA_NOPE + MLA_ROPE].set(_rot_half_cols(w_r, half))
    w_lat = jnp.concatenate([w_in[:, :o1], tile_r, tile_rr], axis=1).astype(BF16)
    lat = _norm_linear(x, g, w_lat, F32, tn=w_lat.shape[1])

    wq3 = w_qb.reshape(MLA_Q_LORA, H, MLA_NOPE + MLA_ROPE)
    wq_rot = jnp.concatenate([jnp.zeros_like(wq3[..., :MLA_NOPE]), _rot_half_cols(wq3[..., MLA_NOPE:], half)], axis=-1)
    wq = _pad_heads(w_qb, H, MLA_NOPE + MLA_ROPE).astype(BF16)
    wqr = _pad_heads(wq_rot.reshape(MLA_Q_LORA, -1), H, MLA_NOPE + MLA_ROPE).astype(BF16)
    wkv3 = w_kvb.reshape(MLA_KV_LORA, H, MLA_NOPE + MLA_V)
    wk = _pad_heads(wkv3[..., :MLA_NOPE].reshape(MLA_KV_LORA, -1), H, MLA_NOPE).astype(BF16)
    wv = _pad_heads(wkv3[..., MLA_NOPE:].reshape(MLA_KV_LORA, -1), H, MLA_V).astype(BF16)
    cos, sin = _rope_tables(S, half, MLA_THETA, MLA_NOPE)

    nS = S // tm
    full = lambda shp: pl.BlockSpec(shp, lambda i: (0,) * len(shp))
    hm_spec = pl.BlockSpec((H, tm, LANES), lambda i: (0, i, 0))
    hm_shape = jax.ShapeDtypeStruct((H, T, LANES), BF16)
    q, k, v = pl.pallas_call(
        _mla_qkv_kernel,
        grid=(T // tm,),
        in_specs=[pl.BlockSpec((tm, lat.shape[1]), lambda i: (i, 0)),
                  full((1, MLA_Q_LORA)), full((1, MLA_KV_LORA)),
                  full(wq.shape), full(wqr.shape), full(wk.shape), full(wv.shape),
                  pl.BlockSpec((tm, LANES), lambda i: (i % nS, 0)),
                  pl.BlockSpec((tm, LANES), lambda i: (i % nS, 0))],
        out_specs=[hm_spec, hm_spec, pl.BlockSpec((H, MLA_VT_ROWS, tm), lambda i: (0, 0, i))],
        out_shape=[hm_shape, hm_shape, jax.ShapeDtypeStruct((H, MLA_VT_ROWS, T), BF16)],
        compiler_params=_cparams("parallel"),
    )(lat, q_norm.reshape(1, -1), kv_norm.reshape(1, -1), wq, wqr, wk, wv, cos, sin)

    q4, k4 = (t.reshape(H, B, S, LANES) for t in (q, k))
    assert tq == tk and H % hp == 0
    o = pl.pallas_call(
        functools.partial(_mla_attn_kernel, tq=tq, tk=tk),
        grid=(B, H // hp, S // tq),
        in_specs=[pl.BlockSpec((hp, None, tq, LANES), lambda b, h, qi: (h, b, qi, 0)),
                  pl.BlockSpec((hp, None, S, LANES), lambda b, h, qi: (h, b, 0, 0)),
                  pl.BlockSpec((hp, MLA_VT_ROWS, S), lambda b, h, qi: (h, 0, b))],
        out_specs=pl.BlockSpec((hp, None, tq, LANES), lambda b, h, qi: (h, b, qi, 0)),
        out_shape=jax.ShapeDtypeStruct((H, B, S, LANES), BF16),
        scratch_shapes=[pltpu.VMEM((hp, 1, tq), F32), pltpu.VMEM((hp, MLA_VT_ROWS, tq), F32),
                        pltpu.VMEM((hp, tk, tq), F32), pltpu.VMEM((hp, tk, tq), F32)],
        compiler_params=_cparams("parallel", "parallel", "arbitrary"),
    )(q4, k4, v)

    w_o = jnp.pad(w_out.reshape(H, MLA_V, D), ((0, 0), (0, LANES - MLA_V), (0, 0))).reshape(H * LANES, D)
    return _linear_res(o.reshape(H, T, LANES), w_o.astype(BF16), x, head_major=True)


def _ret_kernel(q_ref, k_ref, v_ref, g_ref, cos_ref, sin_ref, gnw_ref, y_ref, state_scr, *, C):
    @pl.when(pl.program_id(1) == 0)
    def _():
        state_scr[...] = jnp.zeros(state_scr.shape, F32)

    half = RET_QK // 2
    cos = cos_ref[...]
    sin = sin_ref[...]
    ii = lax.broadcasted_iota(jnp.int32, (C, C), 0)
    jj = lax.broadcasted_iota(jnp.int32, (C, C), 1)
    diff = (ii - jj).astype(F32)
    causal = ii >= jj
    idx = lax.broadcasted_iota(jnp.int32, (C, 1), 0).astype(F32)

    def rope(t):
        t1, t2 = t[:, :half], t[:, half:]
        return jnp.concatenate([t1 * cos - t2 * sin, t1 * sin + t2 * cos], axis=1)

    for h in range(RET_HEADS):
        lg = math.log1p(-(2.0 ** (-5.0 - h)))
        intra = jnp.exp(jnp.where(causal, lg * diff, -jnp.inf))
        q_dec = jnp.exp(lg * (idx + 1.0))
        k_dec = jnp.exp(lg * (C - 1.0 - idx))
        c_dec = math.exp(lg * C)
        q = rope(q_ref[:, h * RET_QK:(h + 1) * RET_QK].astype(F32))
        k = rope(k_ref[:, h * RET_QK:(h + 1) * RET_QK].astype(F32)) * (RET_QK ** -0.5)
        v = v_ref[:, h * RET_V:(h + 1) * RET_V]
        st = state_scr[h]
        a = _dot_nt(q.astype(BF16), k.astype(BF16)) * intra
        o = _dot(a.astype(BF16), v) + _dot((q * q_dec).astype(BF16), st.astype(BF16))
        state_scr[h] = st * c_dec + _dot((k * k_dec).T.astype(BF16), v)
        mu = jnp.mean(o, axis=-1, keepdims=True)
        oc = o - mu
        var = jnp.mean(oc * oc, axis=-1, keepdims=True)
        on = oc * lax.rsqrt(var + EPS) * gnw_ref[:, h * RET_V:(h + 1) * RET_V]
        gt = g_ref[:, h * RET_V:(h + 1) * RET_V].astype(F32)
        y_ref[:, h * RET_V:(h + 1) * RET_V] = (gt * jax.nn.sigmoid(gt) * on).astype(y_ref.dtype)


def _retention(x, g, w_in, gn_w, w_out, B, S, C=128):
    T, D = x.shape
    nqk, nv = RET_HEADS * RET_QK, RET_HEADS * RET_V
    proj = _norm_linear(x, g, w_in.astype(BF16), BF16)
    half = RET_QK // 2
    ang = jnp.arange(S, dtype=F32)[:, None] * (RET_THETA ** (-jnp.arange(half, dtype=F32) / half))
    cos, sin = jnp.cos(ang), jnp.sin(ang)
    nc = S // C
    y = pl.pallas_call(
        functools.partial(_ret_kernel, C=C),
        grid=(B, nc),
        in_specs=[pl.BlockSpec((C, nqk), lambda b, c: (b * nc + c, 0)),
                  pl.BlockSpec((C, nqk), lambda b, c: (b * nc + c, 1)),
                  pl.BlockSpec((C, nv), lambda b, c: (b * nc + c, (2 * nqk) // nv)),
                  pl.BlockSpec((C, nv), lambda b, c: (b * nc + c, (2 * nqk) // nv + 1)),
                  pl.BlockSpec((C, RET_QK // 2), lambda b, c: (c, 0)),
                  pl.BlockSpec((C, RET_QK // 2), lambda b, c: (c, 0)),
                  pl.BlockSpec((1, nv), lambda b, c: (0, 0))],
        out_specs=pl.BlockSpec((C, nv), lambda b, c: (b * nc + c, 0)),
        out_shape=jax.ShapeDtypeStruct((T, nv), BF16),
        scratch_shapes=[pltpu.VMEM((RET_HEADS, RET_QK, RET_V), F32)],
        compiler_params=_cparams("parallel", "arbitrary"),
    )(proj, proj, proj, proj, cos, sin, gn_w.reshape(1, nv))
    return _linear_res(y, w_out.astype(BF16), x)


def _split3(a):
    a1 = a.astype(BF16)
    r = a - a1.astype(F32)
    a2 = r.astype(BF16)
    a3 = (r - a2.astype(F32)).astype(BF16)
    return a1, a2, a3


SSD_COLB = 1024


def _ssd_kernel(*refs, C):
    nz, nx = SSD_D_INNER // SSD_COLB, SSD_CONV_DIM // SSD_COLB
    z_refs, x_refs = refs[:nz], refs[nz:nz + nx]
    (dt_ref, cw_ref, cb_ref, dtb_ref, alog_ref, dsk_ref, nw_ref, y_ref,
     xe_scr, act_scr, st_scr, y_scr) = refs[nz + nx:]
    G, N, P = SSD_GROUPS, SSD_STATE, SSD_P
    DI = SSD_D_INNER
    pairs_per_group = (SSD_HEADS // G) // 2
    HALO = 8
    tiles_per_colb = SSD_COLB // LANES

    @pl.when(pl.program_id(1) == 0)
    def _():
        xe_scr[0:HALO, :] = jnp.zeros((HALO, xe_scr.shape[1]), F32)
        st_scr[...] = jnp.zeros(st_scr.shape, F32)

    for kx in range(nx):
        xe_scr[HALO:HALO + C, kx * SSD_COLB:(kx + 1) * SSD_COLB] = x_refs[kx][...].astype(F32)
    CB = 512
    for cb in range(SSD_CONV_DIM // CB):
        cs = slice(cb * CB, (cb + 1) * CB)
        acc = jnp.broadcast_to(cb_ref[:, cs], (C, CB))
        for kk in range(SSD_CONV):
            off = HALO - (SSD_CONV - 1) + kk
            acc = acc + cw_ref[kk:kk + 1, cs] * xe_scr[off:off + C, cs]
        act_scr[:, cs] = (acc * jax.nn.sigmoid(acc)).astype(BF16)
    xe_scr[0:HALO, :] = xe_scr[C:C + HALO, :]

    dtr = dt_ref[...] + dtb_ref[...]
    dt = jnp.maximum(dtr, 0.0) + jnp.log1p(jnp.exp(-jnp.abs(dtr)))
    a = dt * (-jnp.exp(alog_ref[...]))
    ii = lax.broadcasted_iota(jnp.int32, (C, C), 0)
    jj = lax.broadcasted_iota(jnp.int32, (C, C), 1)
    causal = ii >= jj
    tril = jnp.where(causal, 1.0, 0.0).astype(BF16)
    a1, a2, a3 = _split3(a)
    a_cs = _dot(tril, a1) + _dot(tril, a2) + _dot(tril, a3)
    a_end = a_cs[C - 1:C, :]
    w_end = jnp.exp(a_end - a_cs) * dt
    ea = jnp.exp(a_cs)
    eend = jnp.exp(a_end)
    a_cs_t = a_cs.T
    dt_t = dt.T
    lane = lax.broadcasted_iota(jnp.int32, (C, LANES), 1)
    lo = lane < P

    def pair(t, ha):
        return jnp.where(lo[:t.shape[0]], t[:, ha:ha + 1], t[:, ha + 1:ha + 2])

    for g in range(G):
        bm = act_scr[:, DI + g * N:DI + (g + 1) * N]
        cm = act_scr[:, DI + G * N + g * N:DI + G * N + (g + 1) * N]
        cbm = _dot_nt(cm, bm)
        bm_t = bm.astype(F32).T.astype(BF16)
        ssq = jnp.zeros((C, 1), F32)
        for pj in range(pairs_per_group):
            j = g * pairs_per_group + pj
            ha = 2 * j
            cs = slice(j * LANES, (j + 1) * LANES)
            xp = act_scr[:, cs]
            ys = []
            for hh in (ha, ha + 1):
                seg = a_cs[:, hh:hh + 1] - a_cs_t[hh:hh + 1, :]
                dec = jnp.exp(jnp.where(causal, seg, -jnp.inf))
                mm = cbm * dec * dt_t[hh:hh + 1, :]
                ys.append(_dot(mm.astype(BF16), xp))
            y = jnp.where(lo, ys[0], ys[1])
            st = st_scr[:, cs]
            y = y + pair(ea, ha) * _dot(cm, st.astype(BF16))
            xpf = xp.astype(F32)
            xw = (xpf * pair(w_end, ha)).astype(BF16)
            st_scr[:, cs] = st * pair(eend, ha) + _dot(bm_t, xw)
            y = y + pair(dsk_ref[...], ha) * xpf
            jz = j % tiles_per_colb
            zt = z_refs[j // tiles_per_colb][:, jz * LANES:(jz + 1) * LANES].astype(F32)
            y = y * (zt * jax.nn.sigmoid(zt))
            ssq = ssq + jnp.sum(y * y, axis=-1, keepdims=True)
            y_scr[:, cs] = y
        gw = pairs_per_group * LANES
        gs = slice(g * gw, (g + 1) * gw)
        rs = lax.rsqrt(ssq / float(gw) + EPS)
        y_ref[:, gs] = (y_scr[:, gs] * rs * nw_ref[:, gs]).astype(y_ref.dtype)


def _pad_lanes(v):
    return jnp.pad(v.astype(F32), (0, LANES - v.shape[0])).reshape(1, LANES)


def _ssd(x, g, w_in, conv_w, conv_b, dt_bias, A_log, D_skip, norm_w, w_out, B, S, C=128):
    T, D = x.shape
    DI, CD = SSD_D_INNER, SSD_CONV_DIM
    w_main = w_in[:, :DI + CD].astype(BF16)
    w_dt = jnp.pad(w_in[:, DI + CD:], ((0, 0), (0, LANES - SSD_HEADS))).astype(BF16)
    proj = _norm_linear(x, g, w_main, BF16)
    dt = _norm_linear(x, g, w_dt, F32)
    nc = S // C
    full = lambda shp: pl.BlockSpec(shp, lambda b, c: (0,) * len(shp))
    ncolb = (DI + CD) // SSD_COLB

    def colb(kc):
        return pl.BlockSpec((C, SSD_COLB), lambda b, c: (b * nc + c, kc))

    y = pl.pallas_call(
        functools.partial(_ssd_kernel, C=C),
        grid=(B, nc),
        in_specs=[colb(kc) for kc in range(ncolb)] + [
                  pl.BlockSpec((C, LANES), lambda b, c: (b * nc + c, 0)),
                  full((SSD_CONV, CD)), full((1, CD)), full((1, LANES)), full((1, LANES)), full((1, LANES)),
                  full((1, DI))],
        out_specs=pl.BlockSpec((C, DI), lambda b, c: (b * nc + c, 0)),
        out_shape=jax.ShapeDtypeStruct((T, DI), BF16),
        scratch_shapes=[pltpu.VMEM((C + 8, CD), F32), pltpu.VMEM((C, CD), BF16),
                        pltpu.VMEM((SSD_STATE, DI), F32), pltpu.VMEM((C, DI), F32)],
        compiler_params=_cparams("parallel", "arbitrary"),
    )(*([proj] * ncolb), dt, conv_w, conv_b.reshape(1, CD), _pad_lanes(dt_bias), _pad_lanes(A_log),
      _pad_lanes(D_skip), norm_w.reshape(1, DI))
    return _linear_res(y, w_out.astype(BF16), x)


ROW_TILE = 8


def _rows_to_tiles(ref, val):
    n = val.shape[0]
    for s in range(ROW_TILE):
        ref[pl.ds(s, n, stride=ROW_TILE), :] = val[:, s * LANES:(s + 1) * LANES]


def _tiles_to_rows(ref, n):
    return jnp.concatenate([ref[pl.ds(s, n, stride=ROW_TILE), :] for s in range(ROW_TILE)], axis=1)


def _router_kernel(x_ref, g_ref, w1_ref, w2_ref, b_ref, h_ref, route_ref, cnt_ref, carry_scr):
    E = MOE_EXPERTS

    @pl.when(pl.program_id(0) == 0)
    def _():
        carry_scr[...] = jnp.zeros(carry_scr.shape, F32)

    h = _rms(x_ref[...], g_ref[...])
    _rows_to_tiles(h_ref, h)
    tm = h.shape[0]
    h1 = h.astype(BF16)
    h2 = (h - h1.astype(F32)).astype(BF16)
    logits = _dot(h1, w1_ref[...]) + _dot(h1, w2_ref[...]) + _dot(h2, w1_ref[...]) + b_ref[...]
    lane = lax.broadcasted_iota(jnp.int32, (tm, LANES), 1)
    lanef = lane.astype(F32)
    big = float(LANES)
    glog = jnp.where((lane >= E) & (lane < E + MOE_GROUPS), logits, -jnp.inf)
    gmax = jnp.max(glog, axis=-1, keepdims=True)
    gval = 1.0 / jnp.sum(jnp.exp(glog - gmax), axis=-1, keepdims=True)
    gidx = jnp.min(jnp.where(glog == gmax, lanef, big), axis=-1, keepdims=True) - float(E)
    in_group = (lane < E) & ((lane // MOE_EPG).astype(F32) == gidx)
    el = jnp.where(in_group, logits, -jnp.inf)
    v1 = jnp.max(el, axis=-1, keepdims=True)
    i1 = jnp.min(jnp.where(el == v1, lanef, big), axis=-1, keepdims=True)
    el2 = jnp.where(lanef == i1, -jnp.inf, el)
    v2 = jnp.max(el2, axis=-1, keepdims=True)
    i2 = jnp.min(jnp.where(el2 == v2, lanef, big), axis=-1, keepdims=True)
    e2 = jnp.exp(v2 - v1)
    wa = gval / (1.0 + e2)
    wb = gval * e2 / (1.0 + e2)
    oh1 = jnp.where(lanef == i1, 1.0, 0.0)
    oh2 = jnp.where(lanef == i2, 1.0, 0.0)
    ohs = oh1 + oh2
    ii = lax.broadcasted_iota(jnp.int32, (tm, tm), 0)
    jj = lax.broadcasted_iota(jnp.int32, (tm, tm), 1)
    stril = jnp.where(ii > jj, 1.0, 0.0).astype(BF16)
    before = _dot(stril, ohs.astype(BF16)) + carry_scr[0:1, :]
    r1 = jnp.sum(before * oh1, axis=-1, keepdims=True)
    r2 = jnp.sum(before * oh2, axis=-1, keepdims=True)
    carry = carry_scr[0:1, :] + jnp.sum(ohs, axis=0, keepdims=True)
    carry_scr[...] = jnp.broadcast_to(carry, carry_scr.shape)
    cnt_ref[...] = jnp.broadcast_to(carry, cnt_ref.shape)
    route = jnp.zeros((tm, LANES), F32)
    for ln, val in enumerate((i1, i2, wa, wb, r1, r2)):
        route = jnp.where(lane == ln, val, route)
    route_ref[...] = route


def _dispatch_kernel(pos_ref, lo_ref, hi_ref, h_ref, xg_hbm, zero_scr, sem, zsem):
    base = pl.program_id(0) * (TOK_TILE * MOE_TOPK)

    @pl.when(pl.program_id(0) == 0)
    def _():
        zero_scr[...] = jnp.zeros(zero_scr.shape, F32)
        blk = zero_scr.shape[0]

        def zero_copy(r, n):
            return pltpu.make_async_copy(zero_scr.at[pl.ds(0, n)], xg_hbm.at[pl.ds(pl.multiple_of(r * n, n), n)], zsem)

        def loop(lo, hi, n, wait):
            def body(r, c):
                cp = zero_copy(r, n)
                cp.wait() if wait else cp.start()
                return c
            lax.fori_loop(lo, hi, body, 0)

        tail_lo = hi_ref[MOE_EXPERTS - 1] * ROW_TILE // blk
        tail_hi = xg_hbm.shape[0] // blk
        for wait in (False, True):
            for e in range(MOE_EXPERTS):
                loop(lo_ref[e], hi_ref[e], ROW_TILE, wait)
            loop(tail_lo, tail_hi, blk, wait)

    def copy(t, k):
        row = pl.multiple_of(pos_ref[base + MOE_TOPK * t + k] * ROW_TILE, ROW_TILE)
        return pltpu.make_async_copy(h_ref.at[pl.ds(t * ROW_TILE, ROW_TILE)], xg_hbm.at[pl.ds(row, ROW_TILE)], sem)

    for t in range(TOK_TILE):
        for k in range(MOE_TOPK):
            copy(t, k).start()
    same_size = pltpu.make_async_copy(h_ref.at[pl.ds(0, ROW_TILE)], xg_hbm.at[pl.ds(0, ROW_TILE)], sem)
    for _ in range(TOK_TILE * MOE_TOPK):
        same_size.wait()


def _expert_kernel(be_ref, nu_ref, x_ref, wg_ref, wu_ref, wd_ref, o_ref):
    del be_ref
    used = pl.program_id(0) < nu_ref[0]
    n = x_ref.shape[0] // ROW_TILE

    @pl.when(used)
    def _():
        xb = _tiles_to_rows(x_ref, n).astype(BF16)
        hg = _dot(xb, wg_ref[...])
        hu = _dot(xb, wu_ref[...])
        _rows_to_tiles(o_ref, _dot((hg * jax.nn.sigmoid(hg) * hu).astype(BF16), wd_ref[...]))

    @pl.when(jnp.logical_not(used))
    def _():
        o_ref[...] = jnp.zeros(o_ref.shape, F32)


def _ple_kernel(pos_ref, x_ref, rt_ref, p_ref, g_ref, gf_ref, wg_ref, wp_ref, yb_hbm, o_ref,
                buf_a, buf_b, sems, *, final):
    tp = x_ref.shape[0]
    i = pl.program_id(0)
    last = pl.num_programs(0) - 1
    slot = i % 2
    bufs = (buf_a, buf_b)

    def copy(tile, sl, t, k):
        row = pl.multiple_of(pos_ref[(tile * tp + t) * MOE_TOPK + k] * ROW_TILE, ROW_TILE)
        dst = bufs[k].at[sl, pl.ds(pl.multiple_of(t * ROW_TILE, ROW_TILE), ROW_TILE)]
        return pltpu.make_async_copy(yb_hbm.at[pl.ds(row, ROW_TILE)], dst, sems.at[sl])

    def wait_slot(sl):
        for k in range(MOE_TOPK):
            same_size = pltpu.make_async_copy(yb_hbm.at[pl.ds(0, ROW_TILE)], bufs[k].at[sl, pl.ds(0, ROW_TILE)],
                                              sems.at[sl])
            for _ in range(tp):
                same_size.wait()

    @pl.when(i == 0)
    def _():
        def first(t, c):
            for k in range(MOE_TOPK):
                copy(0, 0, t, k).start()
            return c
        lax.fori_loop(0, tp, first, 0, unroll=8)

    wait_slot(slot)
    nxt = jnp.minimum(i + 1, last)
    for t in range(tp):
        for k in range(MOE_TOPK):
            copy(nxt, 1 - slot, t, k).start()

    ple = _dot(p_ref[...].astype(BF16), wp_ref[...])
    rt = rt_ref[...]
    ya = _tiles_to_rows(buf_a.at[slot], tp)
    yb = _tiles_to_rows(buf_b.at[slot], tp)
    x1 = x_ref[...] + rt[:, 2:3] * ya + rt[:, 3:4] * yb
    gate = jax.nn.sigmoid(_dot(_rms(x1, g_ref[...]).astype(BF16), wg_ref[...]))
    out = x1 + ple * gate
    if final:
        out = _rms(out, gf_ref[...])
    o_ref[...] = out

    @pl.when(i == last)
    def _():
        wait_slot(1 - slot)


def _moe_ple(x, p_i, norm_moe, w_group, b_group, w_expert, b_expert, w_gate, w_up, w_down,
             norm_ple, ple_w, ple_w_gate, norm_final, final, tm=512):
    T, D = x.shape
    E, BLK = MOE_EXPERTS, MOE_BLOCK
    A = T * MOE_TOPK
    nb = A // BLK + E
    P = nb * BLK

    w_r = jnp.zeros((D, LANES), F32).at[:, :E].set(w_expert).at[:, E:E + MOE_GROUPS].set(w_group)
    b_r = jnp.zeros((1, LANES), F32).at[0, :E].set(b_expert).at[0, E:E + MOE_GROUPS].set(b_group)
    w_r1 = w_r.astype(BF16)
    w_r2 = (w_r - w_r1.astype(F32)).astype(BF16)
    full = lambda shp: pl.BlockSpec(shp, lambda i: (0,) * len(shp))
    h, route, cnt = pl.pallas_call(
        _router_kernel,
        grid=(T // tm,),
        in_specs=[pl.BlockSpec((tm, D), lambda i: (i, 0)), full((1, D)), full((D, LANES)), full((D, LANES)),
                  full((1, LANES))],
        out_specs=[pl.BlockSpec((tm * ROW_TILE, LANES), lambda i: (i, 0)),
                   pl.BlockSpec((tm, LANES), lambda i: (i, 0)), full((8, LANES))],
        out_shape=[jax.ShapeDtypeStruct((T * ROW_TILE, LANES), F32), jax.ShapeDtypeStruct((T, LANES), F32),
                   jax.ShapeDtypeStruct((8, LANES), F32)],
        scratch_shapes=[pltpu.VMEM((8, LANES), F32)],
        compiler_params=_cparams("arbitrary"),
    )(x, norm_moe.reshape(1, D), w_r1, w_r2, b_r)

    counts = cnt[0, :E].astype(jnp.int32)
    padded = (counts + BLK - 1) // BLK * BLK
    pad_end = jnp.cumsum(padded)
    pad_start = pad_end - padded
    expert = route[:, 0:MOE_TOPK].astype(jnp.int32)
    rank = route[:, 4:4 + MOE_TOPK].astype(jnp.int32)
    pos = (pad_start[expert] + rank).reshape(A)
    blk_start = jnp.arange(nb, dtype=jnp.int32) * BLK
    blk_expert = jnp.minimum(jnp.sum((pad_end[None, :] <= blk_start[:, None]).astype(jnp.int32), axis=1), E - 1)
    n_used = (pad_end[-1:] // BLK).astype(jnp.int32)

    assert D == ROW_TILE * LANES
    xg = pl.pallas_call(
        _dispatch_kernel,
        grid_spec=pltpu.PrefetchScalarGridSpec(
            num_scalar_prefetch=3, grid=(T // TOK_TILE,),
            in_specs=[pl.BlockSpec((TOK_TILE * ROW_TILE, LANES), lambda i, ps, lo, hi: (i, 0))],
            out_specs=pl.BlockSpec(memory_space=pl.ANY),
            scratch_shapes=[pltpu.VMEM((BLK * ROW_TILE, LANES), F32), pltpu.SemaphoreType.DMA,
                            pltpu.SemaphoreType.DMA]),
        out_shape=jax.ShapeDtypeStruct((P * ROW_TILE, LANES), F32),
        compiler_params=_cparams("arbitrary"),
    )(pos, pad_start + counts, pad_end, h)

    def wmap(i, be, nu):
        return (be[jnp.minimum(i, nu[0] - 1)], 0, 0)

    def xmap(i, be, nu):
        return (jnp.minimum(i, nu[0] - 1), 0)

    yb = pl.pallas_call(
        _expert_kernel,
        grid_spec=pltpu.PrefetchScalarGridSpec(
            num_scalar_prefetch=2, grid=(nb,),
            in_specs=[pl.BlockSpec((BLK * ROW_TILE, LANES), xmap),
                      pl.BlockSpec((None, D, MOE_FF), wmap), pl.BlockSpec((None, D, MOE_FF), wmap),
                      pl.BlockSpec((None, MOE_FF, D), wmap)],
            out_specs=pl.BlockSpec((BLK * ROW_TILE, LANES), lambda i, be, nu: (i, 0))),
        out_shape=jax.ShapeDtypeStruct((P * ROW_TILE, LANES), F32),
        compiler_params=_cparams("arbitrary"),
    )(blk_expert, n_used, xg, w_gate.astype(BF16), w_up.astype(BF16), w_down.astype(BF16))

    tp = TOK_TILE
    PD = p_i.shape[1]
    return pl.pallas_call(
        functools.partial(_ple_kernel, final=final),
        grid_spec=pltpu.PrefetchScalarGridSpec(
            num_scalar_prefetch=1, grid=(T // tp,),
            in_specs=[pl.BlockSpec((tp, D), lambda i, ps: (i, 0)),
                      pl.BlockSpec((tp, LANES), lambda i, ps: (i, 0)),
                      pl.BlockSpec((tp, PD), lambda i, ps: (i, 0)),
                      pl.BlockSpec((1, D), lambda i, ps: (0, 0)), pl.BlockSpec((1, D), lambda i, ps: (0, 0)),
                      pl.BlockSpec((D, D), lambda i, ps: (0, 0)), pl.BlockSpec((PD, D), lambda i, ps: (0, 0)),
                      pl.BlockSpec(memory_space=pl.ANY)],
            out_specs=pl.BlockSpec((tp, D), lambda i, ps: (i, 0)),
            scratch_shapes=[pltpu.VMEM((2, tp * ROW_TILE, LANES), F32), pltpu.VMEM((2, tp * ROW_TILE, LANES), F32),
                            pltpu.SemaphoreType.DMA((2,))]),
        out_shape=jax.ShapeDtypeStruct((T, D), F32),
        compiler_params=_cparams("arbitrary"),
    )(pos, x, route, p_i, norm_ple.reshape(1, D), norm_final.reshape(1, D),
      ple_w_gate.astype(BF16), ple_w.astype(BF16), yb)


def kernel(x, p, norm_mix, norm_moe, norm_ple, norm_final, dsa_w_in, dsa_w_out, mla_w_in, mla_q_norm, mla_w_qb,
           mla_kv_norm, mla_w_kvb, mla_w_out, ret_w_in, ret_gn_w, ret_w_out, ssd_w_in, ssd_conv_w, ssd_conv_b,
           ssd_dt_bias, ssd_A_log, ssd_D, ssd_norm_w, ssd_w_out, moe_w_group, moe_b_group, moe_w_expert,
           moe_b_expert, moe_w_gate, moe_w_up, moe_w_down, ple_w, ple_w_gate):
    B, S, D = x.shape
    depth = p.shape[0]
    xt = x.reshape(B * S, D)
    for i in range(depth):
        kind = i % 4
        if kind == 0:
            xt = _dilated_attention(xt, norm_mix[i], dsa_w_in, dsa_w_out, B, S)
        elif kind == 1:
            xt = _mla(xt, norm_mix[i], mla_w_in, mla_q_norm, mla_w_qb, mla_kv_norm, mla_w_kvb, mla_w_out, B, S)
        elif kind == 2:
            xt = _retention(xt, norm_mix[i], ret_w_in, ret_gn_w, ret_w_out, B, S)
        else:
            xt = _ssd(xt, norm_mix[i], ssd_w_in, ssd_conv_w, ssd_conv_b, ssd_dt_bias, ssd_A_log, ssd_D,
                      ssd_norm_w, ssd_w_out, B, S)
        xt = _moe_ple(xt, p[i].reshape(B * S, -1), norm_moe[i], moe_w_group[i], moe_b_group[i], moe_w_expert[i],
                      moe_b_expert[i], moe_w_gate[i], moe_w_up[i], moe_w_down[i], norm_ple[i], ple_w[i],
                      ple_w_gate[i], norm_final, final=(i == depth - 1))
    return xt.reshape(B, S, D)
```

```python
import functools
import math

import jax
import jax.numpy as jnp
from jax import lax
from jax.experimental import pallas as pl
from jax.experimental.pallas import tpu as pltpu

F32 = jnp.float32
BF16 = jnp.bfloat16
EPS = 1e-6
LANES = 128
VMEM_LIMIT = 48 * 1024 * 1024

DSA_PATTERNS = ((128, 1), (512, 4), (2048, 16))
DSA_HEADS = 8
DSA_SPAN = 2048
MLA_HEADS = 16
MLA_Q_LORA, MLA_KV_LORA, MLA_NOPE, MLA_ROPE, MLA_V = 256, 128, 64, 32, 64
MLA_THETA = 10000.0
MLA_VT_ROWS = 80
RET_HEADS, RET_QK, RET_V = 4, 256, 512
RET_THETA = 10000.0
SSD_D_INNER, SSD_HEADS, SSD_GROUPS, SSD_STATE, SSD_CONV, SSD_P = 2048, 32, 4, 128, 4, 64
SSD_CONV_DIM = SSD_D_INNER + 2 * SSD_GROUPS * SSD_STATE
MOE_GROUPS, MOE_EPG, MOE_EXPERTS, MOE_TOPK, MOE_FF = 4, 8, 32, 2, 512
MOE_BLOCK = 256
TOK_TILE = 256


def _cparams(*sem):
    return pltpu.CompilerParams(dimension_semantics=sem, vmem_limit_bytes=VMEM_LIMIT)


def _rms(x, g):
    return x * lax.rsqrt(jnp.mean(x * x, axis=-1, keepdims=True) + EPS) * g


def _dot(a, b):
    return jnp.dot(a, b, preferred_element_type=F32)


def _dot_nt(a, b):
    return lax.dot_general(a, b, (((1,), (1,)), ((), ())), preferred_element_type=F32)


def _norm_linear_kernel(x_ref, g_ref, w_ref, o_ref, h_scr, *, head_major):
    @pl.when(pl.program_id(1) == 0)
    def _():
        h_scr[...] = _rms(x_ref[...], g_ref[...]).astype(BF16)

    r = _dot(h_scr[...], w_ref[...])
    if head_major:
        for c in range(r.shape[1] // LANES):
            o_ref[c] = r[:, c * LANES:(c + 1) * LANES].astype(o_ref.dtype)
    else:
        o_ref[...] = r.astype(o_ref.dtype)


def _norm_linear(x, g, w, out_dtype, *, head_major=False, tm=1024, tn=1024):
    T, D = x.shape
    N = w.shape[1]
    tn = min(tn, N)
    assert T % tm == 0 and N % tn == 0 and tn % LANES == 0
    if head_major:
        out_shape = jax.ShapeDtypeStruct((N // LANES, T, LANES), out_dtype)
        out_spec = pl.BlockSpec((tn // LANES, tm, LANES), lambda i, j: (j, i, 0))
    else:
        out_shape = jax.ShapeDtypeStruct((T, N), out_dtype)
        out_spec = pl.BlockSpec((tm, tn), lambda i, j: (i, j))
    return pl.pallas_call(
        functools.partial(_norm_linear_kernel, head_major=head_major),
        grid=(T // tm, N // tn),
        in_specs=[pl.BlockSpec((tm, D), lambda i, j: (i, 0)),
                  pl.BlockSpec((1, D), lambda i, j: (0, 0)),
                  pl.BlockSpec((D, tn), lambda i, j: (0, j))],
        out_specs=out_spec,
        out_shape=out_shape,
        scratch_shapes=[pltpu.VMEM((tm, D), BF16)],
        compiler_params=_cparams("parallel", "arbitrary"),
    )(x, g.reshape(1, D), w)


def _linear_res_kernel(a_ref, w_ref, r_ref, o_ref, *, head_major):
    if head_major:
        a = jnp.concatenate([a_ref[c] for c in range(a_ref.shape[0])], axis=1)
    else:
        a = a_ref[...]
    o_ref[...] = r_ref[...] + _dot(a, w_ref[...])


def _linear_res(a, w, res, *, head_major=False, tm=512):
    T, D = res.shape
    K = w.shape[0]
    if head_major:
        a_spec = pl.BlockSpec((K // LANES, tm, LANES), lambda i: (0, i, 0))
    else:
        a_spec = pl.BlockSpec((tm, K), lambda i: (i, 0))
    return pl.pallas_call(
        functools.partial(_linear_res_kernel, head_major=head_major),
        grid=(T // tm,),
        in_specs=[a_spec,
                  pl.BlockSpec((K, D), lambda i: (0, 0)),
                  pl.BlockSpec((tm, D), lambda i: (i, 0))],
        out_specs=pl.BlockSpec((tm, D), lambda i: (i, 0)),
        out_shape=jax.ShapeDtypeStruct((T, D), F32),
        compiler_params=_cparams("parallel"),
    )(a, w, res)


def _dsa_group_kernel(sl_ref, q_ref, k_ref, kh_ref, v_ref, vh_ref, o_ref, sm_ref, sd_ref, *, dil, rows):
    W = LANES
    i = pl.program_id(1)
    h = pl.program_id(2)
    scale = float(W) ** -0.5
    slope = sl_ref[h] * float(dil)
    qi = lax.broadcasted_iota(jnp.int32, (W, 2 * W), 0)
    kj = lax.broadcasted_iota(jnp.int32, (W, 2 * W), 1)
    steps = W + qi - kj
    band = (steps >= 0) & (steps <= W)
    first_lim = jnp.where(i == 0, W, 0)
    band_first = band & (kj >= first_lim)
    bias = (-slope) * steps.astype(F32)
    lane = lax.broadcasted_iota(jnp.int32, (W, LANES), 1)

    @pl.when(h == 0)
    def _():
        sm_ref[...] = jnp.zeros(sm_ref.shape, F32)
        sd_ref[...] = jnp.ones(sd_ref.shape, F32)

    for r in range(dil):
        cs = slice(r * LANES, (r + 1) * LANES)
        for n in range(rows // W):
            rs = slice(n * W, (n + 1) * W)
            q = q_ref[rs, cs]
            if n == 0:
                kk = jnp.concatenate([kh_ref[:, cs], k_ref[rs, cs]], axis=0)
                vv = jnp.concatenate([vh_ref[:, cs], v_ref[rs, cs]], axis=0)
                valid = band_first
            else:
                kk = k_ref[(n - 1) * W:(n + 1) * W, cs]
                vv = v_ref[(n - 1) * W:(n + 1) * W, cs]
                valid = band
            s = _dot_nt(q, kk) * scale + bias
            s = jnp.where(valid, s, -jnp.inf)
            m = jnp.max(s, axis=-1, keepdims=True)
            e = jnp.exp(s - m)
            den = jnp.sum(e, axis=-1, keepdims=True)
            o = _dot(e.astype(BF16), vv) / den
            o_ref[rs, cs] = o.astype(o_ref.dtype)
            sm_ref[rs, cs] = jnp.where(lane == h, m, sm_ref[rs, cs])
            sd_ref[rs, cs] = jnp.where(lane == h, den, sd_ref[rs, cs])


def _dsa_proj_kernel(x_ref, g_ref, w_ref, o_ref, h_scr, r_scr, *, dil):
    @pl.when(pl.program_id(1) == 0)
    def _():
        h_scr[...] = _rms(x_ref[...], g_ref[...]).astype(BF16)

    res = _dot(h_scr[...], w_ref[...])
    rows = r_scr.shape[1] // dil
    for c in range(o_ref.shape[0]):
        if dil == 1:
            o_ref[c] = res[:, c * LANES:(c + 1) * LANES].astype(o_ref.dtype)
            continue
        r_scr[c] = res[:, c * LANES:(c + 1) * LANES]
        for r in range(dil):
            o_ref[c, :, r * LANES:(r + 1) * LANES] = r_scr[c, pl.ds(r, rows, stride=dil), :].astype(o_ref.dtype)


def _dsa_proj(x, g, w, dil, tm=1024):
    T, D = x.shape
    N = w.shape[1]
    tn = DSA_HEADS * LANES
    return pl.pallas_call(
        functools.partial(_dsa_proj_kernel, dil=dil),
        grid=(T // tm, N // tn),
        in_specs=[pl.BlockSpec((tm, D), lambda i, j: (i, 0)),
                  pl.BlockSpec((1, D), lambda i, j: (0, 0)),
                  pl.BlockSpec((D, tn), lambda i, j: (0, j))],
        out_specs=pl.BlockSpec((DSA_HEADS, tm // dil, dil * LANES), lambda i, j: (j, i, 0)),
        out_shape=jax.ShapeDtypeStruct((N // LANES, T // dil, dil * LANES), BF16),
        scratch_shapes=[pltpu.VMEM((tm, D), BF16), pltpu.VMEM((DSA_HEADS, tm, LANES), F32)],
        compiler_params=_cparams("parallel", "arbitrary"),
    )(x, g.reshape(1, D), w)


def _dsa_group(qkv, slopes, dil, B, S):
    H = DSA_HEADS
    L = S // dil
    rows = DSA_SPAN // dil
    width = dil * LANES
    qv = qkv.reshape(qkv.shape[0], B, L, width)
    hb = rows // LANES
    base = 0

    def qmap(o):
        return lambda b, i, h, sl: (base + o * H + h, b, i, 0)

    def hmap(o):
        return lambda b, i, h, sl: (base + o * H + h, b, jnp.maximum(i * hb - 1, 0), 0)

    blk = (None, None, rows, width)
    hblk = (None, None, LANES, width)
    o, sm, sd = pl.pallas_call(
        functools.partial(_dsa_group_kernel, dil=dil, rows=rows),
        grid_spec=pltpu.PrefetchScalarGridSpec(
            num_scalar_prefetch=1,
            grid=(B, L // rows, H),
            in_specs=[pl.BlockSpec(blk, qmap(0)),
                      pl.BlockSpec(blk, qmap(1)), pl.BlockSpec(hblk, hmap(1)),
                      pl.BlockSpec(blk, qmap(2)), pl.BlockSpec(hblk, hmap(2))],
            out_specs=[pl.BlockSpec(blk, lambda b, i, h, sl: (h, b, i, 0)),
                       pl.BlockSpec((None, rows, width), lambda b, i, h, sl: (b, i, 0)),
                       pl.BlockSpec((None, rows, width), lambda b, i, h, sl: (b, i, 0))]),
        out_shape=[jax.ShapeDtypeStruct((H, B, L, width), BF16),
                   jax.ShapeDtypeStruct((B, L, width), F32),
                   jax.ShapeDtypeStruct((B, L, width), F32)],
        compiler_params=_cparams("parallel", "parallel", "arbitrary"),
    )(slopes, qv, qv, qv, qv, qv)
    T = B * S
    return o.reshape(H, T, LANES), sm.reshape(T, LANES), sd.reshape(T, LANES)


def _dsa_out_kernel(o0, o1, o2, m0, m1, m2, d0, d1, d2, w_ref, x_ref, out_ref):
    ms = [m0[...], m1[...], m2[...]]
    mx = jnp.maximum(jnp.maximum(ms[0], ms[1]), ms[2])
    ws = [d[...] * jnp.exp(m - mx) for d, m in zip((d0, d1, d2), ms)]
    tot = ws[0] + ws[1] + ws[2]
    ws = [w / tot for w in ws]
    parts = []
    for h in range(DSA_HEADS):
        acc = ws[0][:, h:h + 1] * o0[h].astype(F32)
        acc = acc + ws[1][:, h:h + 1] * o1[h].astype(F32)
        acc = acc + ws[2][:, h:h + 1] * o2[h].astype(F32)
        parts.append(acc.astype(BF16))
    a = jnp.concatenate(parts, axis=1)
    out_ref[...] = x_ref[...] + _dot(a, w_ref[...])


def _dsa_out(os_, ms, ds, w, x, tm=512):
    T, D = x.shape
    H = DSA_HEADS
    ospec = pl.BlockSpec((H, tm, LANES), lambda i: (0, i, 0))
    sspec = pl.BlockSpec((tm, LANES), lambda i: (i, 0))
    return pl.pallas_call(
        _dsa_out_kernel,
        grid=(T // tm,),
        in_specs=[ospec] * 3 + [sspec] * 6 + [pl.BlockSpec((H * LANES, D), lambda i: (0, 0)),
                                              pl.BlockSpec((tm, D), lambda i: (i, 0))],
        out_specs=pl.BlockSpec((tm, D), lambda i: (i, 0)),
        out_shape=jax.ShapeDtypeStruct((T, D), F32),
        compiler_params=_cparams("parallel"),
    )(*os_, *ms, *ds, w, x)


def _dilated_attention(x, g, w_in, w_out, B, S):
    ng = len(DSA_PATTERNS)
    gw = 3 * DSA_HEADS * LANES
    slopes = 2.0 ** (-8.0 * jnp.arange(1, ng * DSA_HEADS + 1, dtype=F32) / (ng * DSA_HEADS))
    os_, ms, ds = [], [], []
    for gi, (window, dil) in enumerate(DSA_PATTERNS):
        assert window // dil == LANES
        qkv = _dsa_proj(x, g, w_in[:, gi * gw:(gi + 1) * gw].astype(BF16), dil)
        o, m, d = _dsa_group(qkv, slopes[gi * DSA_HEADS:(gi + 1) * DSA_HEADS], dil, B, S)
        os_.append(o)
        ms.append(m)
        ds.append(d)
    return _dsa_out(os_, ms, ds, w_out.astype(BF16), x)


def _mla_qkv_kernel(lat_ref, qn_ref, kvn_ref, wq_ref, wqr_ref, wk_ref, wv_ref, cos_ref, sin_ref,
                    q_ref, k_ref, vt_ref):
    lat = lat_ref[...]
    cos = cos_ref[...]
    sin = sin_ref[...]
    hq = _rms(lat[:, :MLA_Q_LORA], qn_ref[...]).astype(BF16)
    hkv = _rms(lat[:, MLA_Q_LORA:MLA_Q_LORA + MLA_KV_LORA], kvn_ref[...]).astype(BF16)
    o1 = MLA_Q_LORA + MLA_KV_LORA
    kr = lat[:, o1:o1 + LANES] * cos + lat[:, o1 + LANES:o1 + 2 * LANES] * sin
    q = _dot(hq, wq_ref[...])
    qr = _dot(hq, wqr_ref[...])
    k = _dot(hkv, wk_ref[...])
    v = _dot(hkv, wv_ref[...])
    lane = lax.broadcasted_iota(jnp.int32, cos.shape, 1)
    scale = float(MLA_NOPE + MLA_ROPE) ** -0.5 * math.log2(math.e)
    cos_q = jnp.where(lane < MLA_NOPE, 1.0, cos) * scale
    sin_q = sin * scale
    for h in range(MLA_HEADS):
        cs = slice(h * LANES, (h + 1) * LANES)
        q_ref[h] = (q[:, cs] * cos_q + qr[:, cs] * sin_q).astype(BF16)
        k_ref[h] = (k[:, cs] + kr).astype(BF16)
        vt_ref[h] = jnp.where(lane == MLA_V, 1.0, v[:, cs]).T[:MLA_VT_ROWS].astype(BF16)


def _mla_attn_kernel(q_ref, k_ref, vt_ref, o_ref, m_scr, acc_scr, sa_scr, sb_scr, *, tq, tk):
    qi = pl.program_id(2)
    heads = range(q_ref.shape[0])
    m_scr[...] = jnp.full(m_scr.shape, -jnp.inf, F32)
    acc_scr[...] = jnp.zeros(acc_scr.shape, F32)

    def scores(j, buf):
        for h in heads:
            buf[h] = _dot_nt(k_ref[h, pl.ds(pl.multiple_of(j * tk, tk), tk), :], q_ref[h])

    def update(j, buf, masked):
        off = pl.multiple_of(j * tk, tk)
        for h in heads:
            st = buf[h]
            if masked:
                kpos = off + lax.broadcasted_iota(jnp.int32, (tk, tq), 0)
                qpos = qi * tq + lax.broadcasted_iota(jnp.int32, (tk, tq), 1)
                st = jnp.where(kpos <= qpos, st, -jnp.inf)
            m_prev = m_scr[h]
            m_new = jnp.maximum(m_prev, jnp.max(st, axis=0, keepdims=True))
            p = jnp.exp2(st - m_new).astype(BF16)
            acc_scr[h] = jnp.exp2(m_prev - m_new) * acc_scr[h] + _dot(vt_ref[h, :, pl.ds(off, tk)], p)
            m_scr[h] = m_new

    scores(0, sa_scr)

    def body(jj, c):
        j = 2 * jj
        scores(j + 1, sb_scr)
        update(j, sa_scr, False)
        scores(j + 2, sa_scr)
        update(j + 1, sb_scr, False)
        return c

    lax.fori_loop(0, qi // 2, body, 0)

    @pl.when(qi % 2 == 0)
    def _():
        update(qi, sa_scr, True)

    @pl.when(qi % 2 == 1)
    def _():
        scores(qi, sb_scr)
        update(qi - 1, sa_scr, False)
        update(qi, sb_scr, True)

    for h in heads:
        acc = acc_scr[h]
        acc = jnp.concatenate([acc, jnp.zeros((LANES - MLA_VT_ROWS, tq), F32)], axis=0).T
        o_ref[h] = (acc / acc[:, MLA_V:MLA_V + 1]).astype(o_ref.dtype)


def _rope_tables(S, half, theta, lane_off):
    inv = theta ** (-jnp.arange(half, dtype=F32) / half)
    ang = jnp.arange(S, dtype=F32)[:, None] * inv
    cos = jnp.zeros((S, LANES), F32).at[:, lane_off:lane_off + 2 * half].set(jnp.tile(jnp.cos(ang), (1, 2)))
    sin = jnp.zeros((S, LANES), F32).at[:, lane_off:lane_off + 2 * half].set(jnp.tile(jnp.sin(ang), (1, 2)))
    return cos, sin


def _rot_half_cols(w, half):
    return jnp.concatenate([-w[..., half:], w[..., :half]], axis=-1)


def _pad_heads(w, heads, width):
    K = w.shape[0]
    w = w.reshape(K, heads, width)
    return jnp.pad(w, ((0, 0), (0, 0), (0, LANES - width))).reshape(K, heads * LANES)


def _mla(x, g, w_in, q_norm, w_qb, kv_norm, w_kvb, w_out, B, S, tm=512, tq=512, tk=512, hp=4):
    T, D = x.shape
    H = MLA_HEADS
    half = MLA_ROPE // 2
    o1 = MLA_Q_LORA + MLA_KV_LORA
    w_r = w_in[:, o1:]
    tile_r = jnp.zeros((D, LANES), F32).at[:, MLA_NOPE:MLA_NOPE + MLA_ROPE].set(w_r)
    tile_rr = jnp.zeros((D, LANES), F32).at[:, MLA_NOPE:MLA_NOPE + MLA_ROPE].set(_rot_half_cols(w_r, half))
    w_lat = jnp.concatenate([w_in[:, :o1], tile_r, tile_rr], axis=1).astype(BF16)
    lat = _norm_linear(x, g, w_lat, F32, tn=w_lat.shape[1])

    wq3 = w_qb.reshape(MLA_Q_LORA, H, MLA_NOPE + MLA_ROPE)
    wq_rot = jnp.concatenate([jnp.zeros_like(wq3[..., :MLA_NOPE]), _rot_half_cols(wq3[..., MLA_NOPE:], half)], axis=-1)
    wq = _pad_heads(w_qb, H, MLA_NOPE + MLA_ROPE).astype(BF16)
    wqr = _pad_heads(wq_rot.reshape(MLA_Q_LORA, -1), H, MLA_NOPE + MLA_ROPE).astype(BF16)
    wkv3 = w_kvb.reshape(MLA_KV_LORA, H, MLA_NOPE + MLA_V)
    wk = _pad_heads(wkv3[..., :MLA_NOPE].reshape(MLA_KV_LORA, -1), H, MLA_NOPE).astype(BF16)
    wv = _pad_heads(wkv3[..., MLA_NOPE:].reshape(MLA_KV_LORA, -1), H, MLA_V).astype(BF16)
    cos, sin = _rope_tables(S, half, MLA_THETA, MLA_NOPE)

    nS = S // tm
    full = lambda shp: pl.BlockSpec(shp, lambda i: (0,) * len(shp))
    hm_spec = pl.BlockSpec((H, tm, LANES), lambda i: (0, i, 0))
    hm_shape = jax.ShapeDtypeStruct((H, T, LANES), BF16)
    q, k, v = pl.pallas_call(
        _mla_qkv_kernel,
        grid=(T // tm,),
        in_specs=[pl.BlockSpec((tm, lat.shape[1]), lambda i: (i, 0)),
                  full((1, MLA_Q_LORA)), full((1, MLA_KV_LORA)),
                  full(wq.shape), full(wqr.shape), full(wk.shape), full(wv.shape),
                  pl.BlockSpec((tm, LANES), lambda i: (i % nS, 0)),
                  pl.BlockSpec((tm, LANES), lambda i: (i % nS, 0))],
        out_specs=[hm_spec, hm_spec, pl.BlockSpec((H, MLA_VT_ROWS, tm), lambda i: (0, 0, i))],
        out_shape=[hm_shape, hm_shape, jax.ShapeDtypeStruct((H, MLA_VT_ROWS, T), BF16)],
        compiler_params=_cparams("parallel"),
    )(lat, q_norm.reshape(1, -1), kv_norm.reshape(1, -1), wq, wqr, wk, wv, cos, sin)

    q4, k4 = (t.reshape(H, B, S, LANES) for t in (q, k))
    assert tq == tk and H % hp == 0
    o = pl.pallas_call(
        functools.partial(_mla_attn_kernel, tq=tq, tk=tk),
        grid=(B, H // hp, S // tq),
        in_specs=[pl.BlockSpec((hp, None, tq, LANES), lambda b, h, qi: (h, b, qi, 0)),
                  pl.BlockSpec((hp, None, S, LANES), lambda b, h, qi: (h, b, 0, 0), pipeline_mode=pl.Buffered(1)),
                  pl.BlockSpec((hp, MLA_VT_ROWS, S), lambda b, h, qi: (h, 0, b), pipeline_mode=pl.Buffered(1))],
        out_specs=pl.BlockSpec((hp, None, tq, LANES), lambda b, h, qi: (h, b, qi, 0)),
        out_shape=jax.ShapeDtypeStruct((H, B, S, LANES), BF16),
        scratch_shapes=[pltpu.VMEM((hp, 1, tq), F32), pltpu.VMEM((hp, MLA_VT_ROWS, tq), F32),
                        pltpu.VMEM((hp, tk, tq), F32), pltpu.VMEM((hp, tk, tq), F32)],
        compiler_params=_cparams("parallel", "parallel", "arbitrary"),
    )(q4, k4, v)

    w_o = jnp.pad(w_out.reshape(H, MLA_V, D), ((0, 0), (0, LANES - MLA_V), (0, 0))).reshape(H * LANES, D)
    return _linear_res(o.reshape(H, T, LANES), w_o.astype(BF16), x, head_major=True)


def _ret_kernel(q_ref, k_ref, v_ref, g_ref, cos_ref, sin_ref, gnw_ref, y_ref, state_scr, *, C):
    @pl.when(pl.program_id(1) == 0)
    def _():
        state_scr[...] = jnp.zeros(state_scr.shape, F32)

    half = RET_QK // 2
    cos = cos_ref[...]
    sin = sin_ref[...]
    ii = lax.broadcasted_iota(jnp.int32, (C, C), 0)
    jj = lax.broadcasted_iota(jnp.int32, (C, C), 1)
    diff = (ii - jj).astype(F32)
    causal = ii >= jj
    idx = lax.broadcasted_iota(jnp.int32, (C, 1), 0).astype(F32)

    def rope(t):
        t1, t2 = t[:, :half], t[:, half:]
        return jnp.concatenate([t1 * cos - t2 * sin, t1 * sin + t2 * cos], axis=1)

    for h in range(RET_HEADS):
        lg = math.log1p(-(2.0 ** (-5.0 - h)))
        intra = jnp.exp(jnp.where(causal, lg * diff, -jnp.inf))
        q_dec = jnp.exp(lg * (idx + 1.0))
        k_dec = jnp.exp(lg * (C - 1.0 - idx))
        c_dec = math.exp(lg * C)
        q = rope(q_ref[:, h * RET_QK:(h + 1) * RET_QK].astype(F32))
        k = rope(k_ref[:, h * RET_QK:(h + 1) * RET_QK].astype(F32)) * (RET_QK ** -0.5)
        v = v_ref[:, h * RET_V:(h + 1) * RET_V]
        st = state_scr[h]
        a = _dot_nt(q.astype(BF16), k.astype(BF16)) * intra
        o = _dot(a.astype(BF16), v) + _dot((q * q_dec).astype(BF16), st.astype(BF16))
        state_scr[h] = st * c_dec + _dot((k * k_dec).T.astype(BF16), v)
        mu = jnp.mean(o, axis=-1, keepdims=True)
        oc = o - mu
        var = jnp.mean(oc * oc, axis=-1, keepdims=True)
        on = oc * lax.rsqrt(var + EPS) * gnw_ref[:, h * RET_V:(h + 1) * RET_V]
        gt = g_ref[:, h * RET_V:(h + 1) * RET_V].astype(F32)
        y_ref[:, h * RET_V:(h + 1) * RET_V] = (gt * jax.nn.sigmoid(gt) * on).astype(y_ref.dtype)


def _retention(x, g, w_in, gn_w, w_out, B, S, C=128):
    T, D = x.shape
    nqk, nv = RET_HEADS * RET_QK, RET_HEADS * RET_V
    proj = _norm_linear(x, g, w_in.astype(BF16), BF16)
    half = RET_QK // 2
    ang = jnp.arange(S, dtype=F32)[:, None] * (RET_THETA ** (-jnp.arange(half, dtype=F32) / half))
    cos, sin = jnp.cos(ang), jnp.sin(ang)
    nc = S // C
    y = pl.pallas_call(
        functools.partial(_ret_kernel, C=C),
        grid=(B, nc),
        in_specs=[pl.BlockSpec((C, nqk), lambda b, c: (b * nc + c, 0)),
                  pl.BlockSpec((C, nqk), lambda b, c: (b * nc + c, 1)),
                  pl.BlockSpec((C, nv), lambda b, c: (b * nc + c, (2 * nqk) // nv)),
                  pl.BlockSpec((C, nv), lambda b, c: (b * nc + c, (2 * nqk) // nv + 1)),
                  pl.BlockSpec((C, RET_QK // 2), lambda b, c: (c, 0)),
                  pl.BlockSpec((C, RET_QK // 2), lambda b, c: (c, 0)),
                  pl.BlockSpec((1, nv), lambda b, c: (0, 0))],
        out_specs=pl.BlockSpec((C, nv), lambda b, c: (b * nc + c, 0)),
        out_shape=jax.ShapeDtypeStruct((T, nv), BF16),
        scratch_shapes=[pltpu.VMEM((RET_HEADS, RET_QK, RET_V), F32)],
        compiler_params=_cparams("parallel", "arbitrary"),
    )(proj, proj, proj, proj, cos, sin, gn_w.reshape(1, nv))
    return _linear_res(y, w_out.astype(BF16), x)


def _split3(a):
    a1 = a.astype(BF16)
    r = a - a1.astype(F32)
    a2 = r.astype(BF16)
    a3 = (r - a2.astype(F32)).astype(BF16)
    return a1, a2, a3


SSD_COLB = 1024


def _ssd_kernel(*refs, C):
    nz, nx = SSD_D_INNER // SSD_COLB, SSD_CONV_DIM // SSD_COLB
    z_refs, x_refs = refs[:nz], refs[nz:nz + nx]
    (dt_ref, cw_ref, cb_ref, dtb_ref, alog_ref, dsk_ref, nw_ref, y_ref,
     xe_scr, act_scr, st_scr, y_scr) = refs[nz + nx:]
    G, N, P = SSD_GROUPS, SSD_STATE, SSD_P
    DI = SSD_D_INNER
    pairs_per_group = (SSD_HEADS // G) // 2
    HALO = 8
    tiles_per_colb = SSD_COLB // LANES

    @pl.when(pl.program_id(1) == 0)
    def _():
        xe_scr[0:HALO, :] = jnp.zeros((HALO, xe_scr.shape[1]), F32)
        st_scr[...] = jnp.zeros(st_scr.shape, F32)

    for kx in range(nx):
        xe_scr[HALO:HALO + C, kx * SSD_COLB:(kx + 1) * SSD_COLB] = x_refs[kx][...].astype(F32)
    CB = 512
    for cb in range(SSD_CONV_DIM // CB):
        cs = slice(cb * CB, (cb + 1) * CB)
        acc = jnp.broadcast_to(cb_ref[:, cs], (C, CB))
        for kk in range(SSD_CONV):
            off = HALO - (SSD_CONV - 1) + kk
            acc = acc + cw_ref[kk:kk + 1, cs] * xe_scr[off:off + C, cs]
        act_scr[:, cs] = (acc * jax.nn.sigmoid(acc)).astype(BF16)
    xe_scr[0:HALO, :] = xe_scr[C:C + HALO, :]

    dtr = dt_ref[...] + dtb_ref[...]
    dt = jnp.maximum(dtr, 0.0) + jnp.log1p(jnp.exp(-jnp.abs(dtr)))
    a = dt * (-jnp.exp(alog_ref[...]))
    ii = lax.broadcasted_iota(jnp.int32, (C, C), 0)
    jj = lax.broadcasted_iota(jnp.int32, (C, C), 1)
    causal = ii >= jj
    tril = jnp.where(causal, 1.0, 0.0).astype(BF16)
    a1, a2, a3 = _split3(a)
    a_cs = _dot(tril, a1) + _dot(tril, a2) + _dot(tril, a3)
    a_end = a_cs[C - 1:C, :]
    w_end = jnp.exp(a_end - a_cs) * dt
    ea = jnp.exp(a_cs)
    eend = jnp.exp(a_end)
    a_cs_t = a_cs.T
    dt_t = dt.T
    lane = lax.broadcasted_iota(jnp.int32, (C, LANES), 1)
    lo = lane < P

    def pair(t, ha):
        return jnp.where(lo[:t.shape[0]], t[:, ha:ha + 1], t[:, ha + 1:ha + 2])

    for g in range(G):
        bm = act_scr[:, DI + g * N:DI + (g + 1) * N]
        cm = act_scr[:, DI + G * N + g * N:DI + G * N + (g + 1) * N]
        cbm = _dot_nt(cm, bm)
        bm_t = bm.astype(F32).T.astype(BF16)
        ssq = jnp.zeros((C, 1), F32)
        for pj in range(pairs_per_group):
            j = g * pairs_per_group + pj
            ha = 2 * j
            cs = slice(j * LANES, (j + 1) * LANES)
            xp = act_scr[:, cs]
            ys = []
            for hh in (ha, ha + 1):
                seg = a_cs[:, hh:hh + 1] - a_cs_t[hh:hh + 1, :]
                dec = jnp.exp(jnp.where(causal, seg, -jnp.inf))
                mm = cbm * dec * dt_t[hh:hh + 1, :]
                ys.append(_dot(mm.astype(BF16), xp))
            y = jnp.where(lo, ys[0], ys[1])
            st = st_scr[:, cs]
            y = y + pair(ea, ha) * _dot(cm, st.astype(BF16))
            xpf = xp.astype(F32)
            xw = (xpf * pair(w_end, ha)).astype(BF16)
            st_scr[:, cs] = st * pair(eend, ha) + _dot(bm_t, xw)
            y = y + pair(dsk_ref[...], ha) * xpf
            jz = j % tiles_per_colb
            zt = z_refs[j // tiles_per_colb][:, jz * LANES:(jz + 1) * LANES].astype(F32)
            y = y * (zt * jax.nn.sigmoid(zt))
            ssq = ssq + jnp.sum(y * y, axis=-1, keepdims=True)
            y_scr[:, cs] = y
        gw = pairs_per_group * LANES
        gs = slice(g * gw, (g + 1) * gw)
        rs = lax.rsqrt(ssq / float(gw) + EPS)
        y_ref[:, gs] = (y_scr[:, gs] * rs * nw_ref[:, gs]).astype(y_ref.dtype)


def _pad_lanes(v):
    return jnp.pad(v.astype(F32), (0, LANES - v.shape[0])).reshape(1, LANES)


def _ssd(x, g, w_in, conv_w, conv_b, dt_bias, A_log, D_skip, norm_w, w_out, B, S, C=128):
    T, D = x.shape
    DI, CD = SSD_D_INNER, SSD_CONV_DIM
    w_main = w_in[:, :DI + CD].astype(BF16)
    w_dt = jnp.pad(w_in[:, DI + CD:], ((0, 0), (0, LANES - SSD_HEADS))).astype(BF16)
    proj = _norm_linear(x, g, w_main, BF16)
    dt = _norm_linear(x, g, w_dt, F32)
    nc = S // C
    full = lambda shp: pl.BlockSpec(shp, lambda b, c: (0,) * len(shp))
    ncolb = (DI + CD) // SSD_COLB

    def colb(kc):
        return pl.BlockSpec((C, SSD_COLB), lambda b, c: (b * nc + c, kc))

    y = pl.pallas_call(
        functools.partial(_ssd_kernel, C=C),
        grid=(B, nc),
        in_specs=[colb(kc) for kc in range(ncolb)] + [
                  pl.BlockSpec((C, LANES), lambda b, c: (b * nc + c, 0)),
                  full((SSD_CONV, CD)), full((1, CD)), full((1, LANES)), full((1, LANES)), full((1, LANES)),
                  full((1, DI))],
        out_specs=pl.BlockSpec((C, DI), lambda b, c: (b * nc + c, 0)),
        out_shape=jax.ShapeDtypeStruct((T, DI), BF16),
        scratch_shapes=[pltpu.VMEM((C + 8, CD), F32), pltpu.VMEM((C, CD), BF16),
                        pltpu.VMEM((SSD_STATE, DI), F32), pltpu.VMEM((C, DI), F32)],
        compiler_params=_cparams("parallel", "arbitrary"),
    )(*([proj] * ncolb), dt, conv_w, conv_b.reshape(1, CD), _pad_lanes(dt_bias), _pad_lanes(A_log),
      _pad_lanes(D_skip), norm_w.reshape(1, DI))
    return _linear_res(y, w_out.astype(BF16), x)


ROW_TILE = 8


def _rows_to_tiles(ref, val):
    n = val.shape[0]
    for s in range(ROW_TILE):
        ref[pl.ds(s, n, stride=ROW_TILE), :] = val[:, s * LANES:(s + 1) * LANES]


def _tiles_to_rows(ref, n):
    return jnp.concatenate([ref[pl.ds(s, n, stride=ROW_TILE), :] for s in range(ROW_TILE)], axis=1)


def _router_kernel(x_ref, g_ref, w1_ref, w2_ref, b_ref, h_ref, route_ref, cnt_ref, carry_scr):
    E = MOE_EXPERTS

    @pl.when(pl.program_id(0) == 0)
    def _():
        carry_scr[...] = jnp.zeros(carry_scr.shape, F32)

    h = _rms(x_ref[...], g_ref[...])
    _rows_to_tiles(h_ref, h)
    tm = h.shape[0]
    h1 = h.astype(BF16)
    h2 = (h - h1.astype(F32)).astype(BF16)
    logits = _dot(h1, w1_ref[...]) + _dot(h1, w2_ref[...]) + _dot(h2, w1_ref[...]) + b_ref[...]
    lane = lax.broadcasted_iota(jnp.int32, (tm, LANES), 1)
    lanef = lane.astype(F32)
    big = float(LANES)
    glog = jnp.where((lane >= E) & (lane < E + MOE_GROUPS), logits, -jnp.inf)
    gmax = jnp.max(glog, axis=-1, keepdims=True)
    gval = 1.0 / jnp.sum(jnp.exp(glog - gmax), axis=-1, keepdims=True)
    gidx = jnp.min(jnp.where(glog == gmax, lanef, big), axis=-1, keepdims=True) - float(E)
    in_group = (lane < E) & ((lane // MOE_EPG).astype(F32) == gidx)
    el = jnp.where(in_group, logits, -jnp.inf)
    v1 = jnp.max(el, axis=-1, keepdims=True)
    i1 = jnp.min(jnp.where(el == v1, lanef, big), axis=-1, keepdims=True)
    el2 = jnp.where(lanef == i1, -jnp.inf, el)
    v2 = jnp.max(el2, axis=-1, keepdims=True)
    i2 = jnp.min(jnp.where(el2 == v2, lanef, big), axis=-1, keepdims=True)
    e2 = jnp.exp(v2 - v1)
    wa = gval / (1.0 + e2)
    wb = gval * e2 / (1.0 + e2)
    oh1 = jnp.where(lanef == i1, 1.0, 0.0)
    oh2 = jnp.where(lanef == i2, 1.0, 0.0)
    ohs = oh1 + oh2
    ii = lax.broadcasted_iota(jnp.int32, (tm, tm), 0)
    jj = lax.broadcasted_iota(jnp.int32, (tm, tm), 1)
    stril = jnp.where(ii > jj, 1.0, 0.0).astype(BF16)
    before = _dot(stril, ohs.astype(BF16)) + carry_scr[0:1, :]
    r1 = jnp.sum(before * oh1, axis=-1, keepdims=True)
    r2 = jnp.sum(before * oh2, axis=-1, keepdims=True)
    carry = carry_scr[0:1, :] + jnp.sum(ohs, axis=0, keepdims=True)
    carry_scr[...] = jnp.broadcast_to(carry, carry_scr.shape)
    cnt_ref[...] = jnp.broadcast_to(carry, cnt_ref.shape)
    route = jnp.zeros((tm, LANES), F32)
    for ln, val in enumerate((i1, i2, wa, wb, r1, r2)):
        route = jnp.where(lane == ln, val, route)
    route_ref[...] = route


def _dispatch_kernel(pos_ref, lo_ref, hi_ref, h_ref, xg_hbm, zero_scr, sem, zsem):
    base = pl.program_id(0) * (TOK_TILE * MOE_TOPK)

    @pl.when(pl.program_id(0) == 0)
    def _():
        zero_scr[...] = jnp.zeros(zero_scr.shape, F32)
        blk = zero_scr.shape[0]

        def zero_copy(r, n):
            return pltpu.make_async_copy(zero_scr.at[pl.ds(0, n)], xg_hbm.at[pl.ds(pl.multiple_of(r * n, n), n)], zsem)

        def loop(lo, hi, n, wait):
            def body(r, c):
                cp = zero_copy(r, n)
                cp.wait() if wait else cp.start()
                return c
            lax.fori_loop(lo, hi, body, 0)

        tail_lo = hi_ref[MOE_EXPERTS - 1] * ROW_TILE // blk
        tail_hi = xg_hbm.shape[0] // blk
        for wait in (False, True):
            for e in range(MOE_EXPERTS):
                loop(lo_ref[e], hi_ref[e], ROW_TILE, wait)
            loop(tail_lo, tail_hi, blk, wait)

    def copy(t, k):
        row = pl.multiple_of(pos_ref[base + MOE_TOPK * t + k] * ROW_TILE, ROW_TILE)
        return pltpu.make_async_copy(h_ref.at[pl.ds(t * ROW_TILE, ROW_TILE)], xg_hbm.at[pl.ds(row, ROW_TILE)], sem)

    for t in range(TOK_TILE):
        for k in range(MOE_TOPK):
            copy(t, k).start()
    same_size = pltpu.make_async_copy(h_ref.at[pl.ds(0, ROW_TILE)], xg_hbm.at[pl.ds(0, ROW_TILE)], sem)
    for _ in range(TOK_TILE * MOE_TOPK):
        same_size.wait()


def _expert_kernel(be_ref, nu_ref, x_ref, wg_ref, wu_ref, wd_ref, o_ref, wg_scr, wu_scr, wd_scr):
    i = pl.program_id(0)
    used = i < nu_ref[0]
    n = x_ref.shape[0] // ROW_TILE

    @pl.when(used & ((i == 0) | (be_ref[i] != be_ref[jnp.maximum(i - 1, 0)])))
    def _():
        wg_scr[...] = wg_ref[...].astype(BF16)
        wu_scr[...] = wu_ref[...].astype(BF16)
        wd_scr[...] = wd_ref[...].astype(BF16)

    @pl.when(used)
    def _():
        xb = _tiles_to_rows(x_ref, n).astype(BF16)
        hg = _dot(xb, wg_scr[...])
        hu = _dot(xb, wu_scr[...])
        _rows_to_tiles(o_ref, _dot((hg * jax.nn.sigmoid(hg) * hu).astype(BF16), wd_scr[...]))

    @pl.when(jnp.logical_not(used))
    def _():
        o_ref[...] = jnp.zeros(o_ref.shape, F32)


def _ple_kernel(pos_ref, x_ref, rt_ref, p_ref, g_ref, gf_ref, wg_ref, wp_ref, yb_hbm, o_ref,
                buf_a, buf_b, sems, *, final):
    tp = x_ref.shape[0]
    i = pl.program_id(0)
    last = pl.num_programs(0) - 1
    slot = i % 2
    bufs = (buf_a, buf_b)

    def copy(tile, sl, t, k):
        row = pl.multiple_of(pos_ref[(tile * tp + t) * MOE_TOPK + k] * ROW_TILE, ROW_TILE)
        dst = bufs[k].at[sl, pl.ds(pl.multiple_of(t * ROW_TILE, ROW_TILE), ROW_TILE)]
        return pltpu.make_async_copy(yb_hbm.at[pl.ds(row, ROW_TILE)], dst, sems.at[sl])

    def wait_slot(sl):
        for k in range(MOE_TOPK):
            same_size = pltpu.make_async_copy(yb_hbm.at[pl.ds(0, ROW_TILE)], bufs[k].at[sl, pl.ds(0, ROW_TILE)],
                                              sems.at[sl])
            for _ in range(tp):
                same_size.wait()

    @pl.when(i == 0)
    def _():
        def first(t, c):
            for k in range(MOE_TOPK):
                copy(0, 0, t, k).start()
            return c
        lax.fori_loop(0, tp, first, 0, unroll=8)

    wait_slot(slot)
    nxt = jnp.minimum(i + 1, last)
    for t in range(tp):
        for k in range(MOE_TOPK):
            copy(nxt, 1 - slot, t, k).start()

    ple = _dot(p_ref[...].astype(BF16), wp_ref[...])
    rt = rt_ref[...]
    ya = _tiles_to_rows(buf_a.at[slot], tp)
    yb = _tiles_to_rows(buf_b.at[slot], tp)
    x1 = x_ref[...] + rt[:, 2:3] * ya + rt[:, 3:4] * yb
    gate = jax.nn.sigmoid(_dot(_rms(x1, g_ref[...]).astype(BF16), wg_ref[...]))
    out = x1 + ple * gate
    if final:
        out = _rms(out, gf_ref[...])
    o_ref[...] = out

    @pl.when(i == last)
    def _():
        wait_slot(1 - slot)


def _moe_ple(x, layer, p_all, norm_moe, w_group, b_group, w_expert, b_expert, w_gate, w_up, w_down,
             norm_ple, ple_w, ple_w_gate, norm_final, final, tm=512):
    T, D = x.shape
    E, BLK = MOE_EXPERTS, MOE_BLOCK
    A = T * MOE_TOPK
    nb = A // BLK + E
    P = nb * BLK

    w_r = jnp.zeros((D, LANES), F32).at[:, :E].set(w_expert).at[:, E:E + MOE_GROUPS].set(w_group)
    b_r = jnp.zeros((1, LANES), F32).at[0, :E].set(b_expert).at[0, E:E + MOE_GROUPS].set(b_group)
    w_r1 = w_r.astype(BF16)
    w_r2 = (w_r - w_r1.astype(F32)).astype(BF16)
    full = lambda shp: pl.BlockSpec(shp, lambda i: (0,) * len(shp))
    h, route, cnt = pl.pallas_call(
        _router_kernel,
        grid=(T // tm,),
        in_specs=[pl.BlockSpec((tm, D), lambda i: (i, 0)), full((1, D)), full((D, LANES)), full((D, LANES)),
                  full((1, LANES))],
        out_specs=[pl.BlockSpec((tm * ROW_TILE, LANES), lambda i: (i, 0)),
                   pl.BlockSpec((tm, LANES), lambda i: (i, 0)), full((8, LANES))],
        out_shape=[jax.ShapeDtypeStruct((T * ROW_TILE, LANES), F32), jax.ShapeDtypeStruct((T, LANES), F32),
                   jax.ShapeDtypeStruct((8, LANES), F32)],
        scratch_shapes=[pltpu.VMEM((8, LANES), F32)],
        compiler_params=_cparams("arbitrary"),
    )(x, norm_moe.reshape(1, D), w_r1, w_r2, b_r)

    counts = cnt[0, :E].astype(jnp.int32)
    padded = (counts + BLK - 1) // BLK * BLK
    pad_end = jnp.cumsum(padded)
    pad_start = pad_end - padded
    expert = route[:, 0:MOE_TOPK].astype(jnp.int32)
    rank = route[:, 4:4 + MOE_TOPK].astype(jnp.int32)
    pos = (pad_start[expert] + rank).reshape(A)
    blk_start = jnp.arange(nb, dtype=jnp.int32) * BLK
    blk_expert = jnp.minimum(jnp.sum((pad_end[None, :] <= blk_start[:, None]).astype(jnp.int32), axis=1), E - 1)
    n_used = (pad_end[-1:] // BLK).astype(jnp.int32)

    assert D == ROW_TILE * LANES
    xg = pl.pallas_call(
        _dispatch_kernel,
        grid_spec=pltpu.PrefetchScalarGridSpec(
            num_scalar_prefetch=3, grid=(T // TOK_TILE,),
            in_specs=[pl.BlockSpec((TOK_TILE * ROW_TILE, LANES), lambda i, ps, lo, hi: (i, 0))],
            out_specs=pl.BlockSpec(memory_space=pl.ANY),
            scratch_shapes=[pltpu.VMEM((BLK * ROW_TILE, LANES), F32), pltpu.SemaphoreType.DMA,
                            pltpu.SemaphoreType.DMA]),
        out_shape=jax.ShapeDtypeStruct((P * ROW_TILE, LANES), F32),
        compiler_params=_cparams("arbitrary"),
    )(pos, pad_start + counts, pad_end, h)

    def wmap(i, be, nu):
        return (layer, be[jnp.minimum(i, nu[0] - 1)], 0, 0)

    def xmap(i, be, nu):
        return (jnp.minimum(i, nu[0] - 1), 0)

    yb = pl.pallas_call(
        _expert_kernel,
        grid_spec=pltpu.PrefetchScalarGridSpec(
            num_scalar_prefetch=2, grid=(nb,),
            in_specs=[pl.BlockSpec((BLK * ROW_TILE, LANES), xmap),
                      pl.BlockSpec((None, None, D, MOE_FF), wmap), pl.BlockSpec((None, None, D, MOE_FF), wmap),
                      pl.BlockSpec((None, None, MOE_FF, D), wmap)],
            out_specs=pl.BlockSpec((BLK * ROW_TILE, LANES), lambda i, be, nu: (i, 0)),
            scratch_shapes=[pltpu.VMEM((D, MOE_FF), BF16), pltpu.VMEM((D, MOE_FF), BF16),
                            pltpu.VMEM((MOE_FF, D), BF16)]),
        out_shape=jax.ShapeDtypeStruct((P * ROW_TILE, LANES), F32),
        compiler_params=_cparams("arbitrary"),
    )(blk_expert, n_used, xg, w_gate, w_up, w_down)

    tp = TOK_TILE
    PD = p_all.shape[-1]
    return pl.pallas_call(
        functools.partial(_ple_kernel, final=final),
        grid_spec=pltpu.PrefetchScalarGridSpec(
            num_scalar_prefetch=1, grid=(T // tp,),
            in_specs=[pl.BlockSpec((tp, D), lambda i, ps: (i, 0)),
                      pl.BlockSpec((tp, LANES), lambda i, ps: (i, 0)),
                      pl.BlockSpec((None, tp, PD), lambda i, ps: (layer, i, 0)),
                      pl.BlockSpec((1, D), lambda i, ps: (0, 0)), pl.BlockSpec((1, D), lambda i, ps: (0, 0)),
                      pl.BlockSpec((D, D), lambda i, ps: (0, 0)), pl.BlockSpec((PD, D), lambda i, ps: (0, 0)),
                      pl.BlockSpec(memory_space=pl.ANY)],
            out_specs=pl.BlockSpec((tp, D), lambda i, ps: (i, 0)),
            scratch_shapes=[pltpu.VMEM((2, tp * ROW_TILE, LANES), F32), pltpu.VMEM((2, tp * ROW_TILE, LANES), F32),
                            pltpu.SemaphoreType.DMA((2,))]),
        out_shape=jax.ShapeDtypeStruct((T, D), F32),
        compiler_params=_cparams("arbitrary"),
    )(pos, x, route, p_all, norm_ple.reshape(1, D), norm_final.reshape(1, D),
      ple_w_gate.astype(BF16), ple_w.astype(BF16), yb)


def kernel(x, p, norm_mix, norm_moe, norm_ple, norm_final, dsa_w_in, dsa_w_out, mla_w_in, mla_q_norm, mla_w_qb,
           mla_kv_norm, mla_w_kvb, mla_w_out, ret_w_in, ret_gn_w, ret_w_out, ssd_w_in, ssd_conv_w, ssd_conv_b,
           ssd_dt_bias, ssd_A_log, ssd_D, ssd_norm_w, ssd_w_out, moe_w_group, moe_b_group, moe_w_expert,
           moe_b_expert, moe_w_gate, moe_w_up, moe_w_down, ple_w, ple_w_gate):
    B, S, D = x.shape
    depth = p.shape[0]
    xt = x.reshape(B * S, D)
    for i in range(depth):
        kind = i % 4
        if kind == 0:
            xt = _dilated_attention(xt, norm_mix[i], dsa_w_in, dsa_w_out, B, S)
        elif kind == 1:
            xt = _mla(xt, norm_mix[i], mla_w_in, mla_q_norm, mla_w_qb, mla_kv_norm, mla_w_kvb, mla_w_out, B, S)
        elif kind == 2:
            xt = _retention(xt, norm_mix[i], ret_w_in, ret_gn_w, ret_w_out, B, S)
        else:
            xt = _ssd(xt, norm_mix[i], ssd_w_in, ssd_conv_w, ssd_conv_b, ssd_dt_bias, ssd_A_log, ssd_D,
                      ssd_norm_w, ssd_w_out, B, S)
        xt = _moe_ple(xt, i, p.reshape(depth, B * S, -1), norm_moe[i], moe_w_group[i], moe_b_group[i],
                      moe_w_expert[i], moe_b_expert[i], moe_w_gate, moe_w_up, moe_w_down, norm_ple[i], ple_w[i],
                      ple_w_gate[i], norm_final, final=(i == depth - 1))
    return xt.reshape(B, S, D)
```

```python
import functools
import math

import jax
import jax.numpy as jnp
from jax import lax
from jax.experimental import pallas as pl
from jax.experimental.pallas import tpu as pltpu

F32 = jnp.float32
BF16 = jnp.bfloat16
EPS = 1e-6
LANES = 128
VMEM_LIMIT = 48 * 1024 * 1024

DSA_PATTERNS = ((128, 1), (512, 4), (2048, 16))
DSA_HEADS = 8
DSA_SPAN = 2048
MLA_HEADS = 16
MLA_Q_LORA, MLA_KV_LORA, MLA_NOPE, MLA_ROPE, MLA_V = 256, 128, 64, 32, 64
MLA_THETA = 10000.0
MLA_VT_ROWS = 80
RET_HEADS, RET_QK, RET_V = 4, 256, 512
RET_THETA = 10000.0
SSD_D_INNER, SSD_HEADS, SSD_GROUPS, SSD_STATE, SSD_CONV, SSD_P = 2048, 32, 4, 128, 4, 64
SSD_CONV_DIM = SSD_D_INNER + 2 * SSD_GROUPS * SSD_STATE
MOE_GROUPS, MOE_EPG, MOE_EXPERTS, MOE_TOPK, MOE_FF = 4, 8, 32, 2, 512
MOE_BLOCK = 256
TOK_TILE = 256


def _cparams(*sem):
    return pltpu.CompilerParams(dimension_semantics=sem, vmem_limit_bytes=VMEM_LIMIT)


def _rms(x, g):
    return x * lax.rsqrt(jnp.mean(x * x, axis=-1, keepdims=True) + EPS) * g


def _dot(a, b):
    return jnp.dot(a, b, preferred_element_type=F32)


def _dot_nt(a, b):
    return lax.dot_general(a, b, (((1,), (1,)), ((), ())), preferred_element_type=F32)


def _norm_linear_kernel(x_ref, g_ref, w_ref, o_ref, h_scr, *, head_major):
    @pl.when(pl.program_id(1) == 0)
    def _():
        h_scr[...] = _rms(x_ref[...], g_ref[...]).astype(BF16)

    r = _dot(h_scr[...], w_ref[...])
    if head_major:
        for c in range(r.shape[1] // LANES):
            o_ref[c] = r[:, c * LANES:(c + 1) * LANES].astype(o_ref.dtype)
    else:
        o_ref[...] = r.astype(o_ref.dtype)


def _norm_linear(x, g, w, out_dtype, *, head_major=False, tm=1024, tn=1024):
    T, D = x.shape
    N = w.shape[1]
    tn = min(tn, N)
    assert T % tm == 0 and N % tn == 0 and tn % LANES == 0
    if head_major:
        out_shape = jax.ShapeDtypeStruct((N // LANES, T, LANES), out_dtype)
        out_spec = pl.BlockSpec((tn // LANES, tm, LANES), lambda i, j: (j, i, 0))
    else:
        out_shape = jax.ShapeDtypeStruct((T, N), out_dtype)
        out_spec = pl.BlockSpec((tm, tn), lambda i, j: (i, j))
    return pl.pallas_call(
        functools.partial(_norm_linear_kernel, head_major=head_major),
        grid=(T // tm, N // tn),
        in_specs=[pl.BlockSpec((tm, D), lambda i, j: (i, 0)),
                  pl.BlockSpec((1, D), lambda i, j: (0, 0)),
                  pl.BlockSpec((D, tn), lambda i, j: (0, j))],
        out_specs=out_spec,
        out_shape=out_shape,
        scratch_shapes=[pltpu.VMEM((tm, D), BF16)],
        compiler_params=_cparams("parallel", "arbitrary"),
    )(x, g.reshape(1, D), w)


def _linear_res_kernel(a_ref, w_ref, r_ref, o_ref, *, head_major):
    if head_major:
        a = jnp.concatenate([a_ref[c] for c in range(a_ref.shape[0])], axis=1)
    else:
        a = a_ref[...]
    o_ref[...] = r_ref[...] + _dot(a, w_ref[...])


def _linear_res(a, w, res, *, head_major=False, tm=512):
    T, D = res.shape
    K = w.shape[0]
    if head_major:
        a_spec = pl.BlockSpec((K // LANES, tm, LANES), lambda i: (0, i, 0))
    else:
        a_spec = pl.BlockSpec((tm, K), lambda i: (i, 0))
    return pl.pallas_call(
        functools.partial(_linear_res_kernel, head_major=head_major),
        grid=(T // tm,),
        in_specs=[a_spec,
                  pl.BlockSpec((K, D), lambda i: (0, 0)),
                  pl.BlockSpec((tm, D), lambda i: (i, 0))],
        out_specs=pl.BlockSpec((tm, D), lambda i: (i, 0)),
        out_shape=jax.ShapeDtypeStruct((T, D), F32),
        compiler_params=_cparams("parallel"),
    )(a, w, res)


def _dsa_group_kernel(sl_ref, q_ref, k_ref, kh_ref, v_ref, vh_ref, o_ref, sm_ref, sd_ref, *, dil, rows):
    W = LANES
    i = pl.program_id(1)
    h = pl.program_id(2)
    scale = float(W) ** -0.5
    slope = sl_ref[h] * float(dil)
    qi = lax.broadcasted_iota(jnp.int32, (W, 2 * W), 0)
    kj = lax.broadcasted_iota(jnp.int32, (W, 2 * W), 1)
    steps = W + qi - kj
    band = (steps >= 0) & (steps <= W)
    first_lim = jnp.where(i == 0, W, 0)
    band_first = band & (kj >= first_lim)
    bias = (-slope) * steps.astype(F32)
    lane = lax.broadcasted_iota(jnp.int32, (W, LANES), 1)

    @pl.when(h == 0)
    def _():
        sm_ref[...] = jnp.zeros(sm_ref.shape, F32)
        sd_ref[...] = jnp.ones(sd_ref.shape, F32)

    for r in range(dil):
        cs = slice(r * LANES, (r + 1) * LANES)
        for n in range(rows // W):
            rs = slice(n * W, (n + 1) * W)
            q = q_ref[rs, cs]
            if n == 0:
                kk = jnp.concatenate([kh_ref[:, cs], k_ref[rs, cs]], axis=0)
                vv = jnp.concatenate([vh_ref[:, cs], v_ref[rs, cs]], axis=0)
                valid = band_first
            else:
                kk = k_ref[(n - 1) * W:(n + 1) * W, cs]
                vv = v_ref[(n - 1) * W:(n + 1) * W, cs]
                valid = band
            s = _dot_nt(q, kk) * scale + bias
            s = jnp.where(valid, s, -jnp.inf)
            m = jnp.max(s, axis=-1, keepdims=True)
            e = jnp.exp(s - m)
            den = jnp.sum(e, axis=-1, keepdims=True)
            o = _dot(e.astype(BF16), vv) / den
            o_ref[rs, cs] = o.astype(o_ref.dtype)
            sm_ref[rs, cs] = jnp.where(lane == h, m, sm_ref[rs, cs])
            sd_ref[rs, cs] = jnp.where(lane == h, den, sd_ref[rs, cs])


def _dsa_proj_kernel(x_ref, g_ref, w_ref, o_ref, h_scr, r_scr, *, dil):
    @pl.when(pl.program_id(1) == 0)
    def _():
        h_scr[...] = _rms(x_ref[...], g_ref[...]).astype(BF16)

    res = _dot(h_scr[...], w_ref[...])
    rows = r_scr.shape[1] // dil
    for c in range(o_ref.shape[0]):
        if dil == 1:
            o_ref[c] = res[:, c * LANES:(c + 1) * LANES].astype(o_ref.dtype)
            continue
        r_scr[c] = res[:, c * LANES:(c + 1) * LANES]
        for r in range(dil):
            o_ref[c, :, r * LANES:(r + 1) * LANES] = r_scr[c, pl.ds(r, rows, stride=dil), :].astype(o_ref.dtype)


def _dsa_proj(x, g, w, dil, tm=1024):
    T, D = x.shape
    N = w.shape[1]
    tn = DSA_HEADS * LANES
    return pl.pallas_call(
        functools.partial(_dsa_proj_kernel, dil=dil),
        grid=(T // tm, N // tn),
        in_specs=[pl.BlockSpec((tm, D), lambda i, j: (i, 0)),
                  pl.BlockSpec((1, D), lambda i, j: (0, 0)),
                  pl.BlockSpec((D, tn), lambda i, j: (0, j))],
        out_specs=pl.BlockSpec((DSA_HEADS, tm // dil, dil * LANES), lambda i, j: (j, i, 0)),
        out_shape=jax.ShapeDtypeStruct((N // LANES, T // dil, dil * LANES), BF16),
        scratch_shapes=[pltpu.VMEM((tm, D), BF16), pltpu.VMEM((DSA_HEADS, tm, LANES), F32)],
        compiler_params=_cparams("parallel", "arbitrary"),
    )(x, g.reshape(1, D), w)


def _dsa_group(qkv, slopes, dil, B, S):
    H = DSA_HEADS
    L = S // dil
    rows = DSA_SPAN // dil
    width = dil * LANES
    qv = qkv.reshape(qkv.shape[0], B, L, width)
    hb = rows // LANES
    base = 0

    def qmap(o):
        return lambda b, i, h, sl: (base + o * H + h, b, i, 0)

    def hmap(o):
        return lambda b, i, h, sl: (base + o * H + h, b, jnp.maximum(i * hb - 1, 0), 0)

    blk = (None, None, rows, width)
    hblk = (None, None, LANES, width)
    o, sm, sd = pl.pallas_call(
        functools.partial(_dsa_group_kernel, dil=dil, rows=rows),
        grid_spec=pltpu.PrefetchScalarGridSpec(
            num_scalar_prefetch=1,
            grid=(B, L // rows, H),
            in_specs=[pl.BlockSpec(blk, qmap(0)),
                      pl.BlockSpec(blk, qmap(1)), pl.BlockSpec(hblk, hmap(1)),
                      pl.BlockSpec(blk, qmap(2)), pl.BlockSpec(hblk, hmap(2))],
            out_specs=[pl.BlockSpec(blk, lambda b, i, h, sl: (h, b, i, 0)),
                       pl.BlockSpec((None, rows, width), lambda b, i, h, sl: (b, i, 0)),
                       pl.BlockSpec((None, rows, width), lambda b, i, h, sl: (b, i, 0))]),
        out_shape=[jax.ShapeDtypeStruct((H, B, L, width), BF16),
                   jax.ShapeDtypeStruct((B, L, width), F32),
                   jax.ShapeDtypeStruct((B, L, width), F32)],
        compiler_params=_cparams("parallel", "parallel", "arbitrary"),
    )(slopes, qv, qv, qv, qv, qv)
    return o.reshape(H, B * L, width), sm.reshape(B * L, width), sd.reshape(B * L, width)


def _dsa_out_kernel(o0, o1, o2, m0, m1, m2, d0, d1, d2, w_ref, x_ref, out_ref, o_scr, s_scr):
    tm = x_ref.shape[0]
    dils = [d for _, d in DSA_PATTERNS]

    def token_major(ref, scr, dil):
        if dil == 1:
            return ref[...].astype(F32)
        for r in range(dil):
            scr[pl.ds(r, tm // dil, stride=dil), :] = ref[:, r * LANES:(r + 1) * LANES].astype(F32)
        return scr[...]

    ms = [token_major(m, s_scr.at[2 * g], dils[g]) for g, m in enumerate((m0, m1, m2))]
    dn = [token_major(d, s_scr.at[2 * g + 1], dils[g]) for g, d in enumerate((d0, d1, d2))]
    mx = jnp.maximum(jnp.maximum(ms[0], ms[1]), ms[2])
    ws = [d * jnp.exp(m - mx) for d, m in zip(dn, ms)]
    tot = ws[0] + ws[1] + ws[2]
    ws = [w / tot for w in ws]
    parts = []
    for h in range(DSA_HEADS):
        acc = None
        for g, o in enumerate((o0, o1, o2)):
            term = ws[g][:, h:h + 1] * token_major(o.at[h], o_scr.at[g, h], dils[g])
            acc = term if acc is None else acc + term
        parts.append(acc.astype(BF16))
    a = jnp.concatenate(parts, axis=1)
    out_ref[...] = x_ref[...] + _dot(a, w_ref[...])


def _dsa_out(os_, ms, ds, w, x, tm=512):
    T, D = x.shape
    H = DSA_HEADS
    ng = len(DSA_PATTERNS)
    ospecs = [pl.BlockSpec((H, tm // d, d * LANES), lambda i: (0, i, 0)) for _, d in DSA_PATTERNS]
    sspecs = [pl.BlockSpec((tm // d, d * LANES), lambda i: (i, 0)) for _, d in DSA_PATTERNS]
    return pl.pallas_call(
        _dsa_out_kernel,
        grid=(T // tm,),
        in_specs=ospecs + sspecs + sspecs + [pl.BlockSpec((H * LANES, D), lambda i: (0, 0)),
                                             pl.BlockSpec((tm, D), lambda i: (i, 0))],
        out_specs=pl.BlockSpec((tm, D), lambda i: (i, 0)),
        out_shape=jax.ShapeDtypeStruct((T, D), F32),
        scratch_shapes=[pltpu.VMEM((ng, H, tm, LANES), F32), pltpu.VMEM((2 * ng, tm, LANES), F32)],
        compiler_params=_cparams("parallel"),
    )(*os_, *ms, *ds, w, x)


def _dilated_attention(x, g, w_in, w_out, B, S):
    ng = len(DSA_PATTERNS)
    gw = 3 * DSA_HEADS * LANES
    slopes = 2.0 ** (-8.0 * jnp.arange(1, ng * DSA_HEADS + 1, dtype=F32) / (ng * DSA_HEADS))
    os_, ms, ds = [], [], []
    for gi, (window, dil) in enumerate(DSA_PATTERNS):
        assert window // dil == LANES
        qkv = _dsa_proj(x, g, w_in[:, gi * gw:(gi + 1) * gw].astype(BF16), dil)
        o, m, d = _dsa_group(qkv, slopes[gi * DSA_HEADS:(gi + 1) * DSA_HEADS], dil, B, S)
        os_.append(o)
        ms.append(m)
        ds.append(d)
    return _dsa_out(os_, ms, ds, w_out.astype(BF16), x)


def _mla_qkv_kernel(lat_ref, qn_ref, kvn_ref, wq_ref, wqr_ref, wk_ref, wv_ref, cos_ref, sin_ref,
                    q_ref, k_ref, vt_ref):
    lat = lat_ref[...]
    cos = cos_ref[...]
    sin = sin_ref[...]
    hq = _rms(lat[:, :MLA_Q_LORA], qn_ref[...]).astype(BF16)
    hkv = _rms(lat[:, MLA_Q_LORA:MLA_Q_LORA + MLA_KV_LORA], kvn_ref[...]).astype(BF16)
    o1 = MLA_Q_LORA + MLA_KV_LORA
    kr = lat[:, o1:o1 + LANES] * cos + lat[:, o1 + LANES:o1 + 2 * LANES] * sin
    q = _dot(hq, wq_ref[...])
    qr = _dot(hq, wqr_ref[...])
    k = _dot(hkv, wk_ref[...])
    v = _dot(hkv, wv_ref[...])
    lane = lax.broadcasted_iota(jnp.int32, cos.shape, 1)
    scale = float(MLA_NOPE + MLA_ROPE) ** -0.5 * math.log2(math.e)
    cos_q = jnp.where(lane < MLA_NOPE, 1.0, cos) * scale
    sin_q = sin * scale
    for h in range(MLA_HEADS):
        cs = slice(h * LANES, (h + 1) * LANES)
        q_ref[h] = (q[:, cs] * cos_q + qr[:, cs] * sin_q).astype(BF16)
        k_ref[h] = (k[:, cs] + kr).astype(BF16)
        vt_ref[h] = jnp.where(lane == MLA_V, 1.0, v[:, cs]).T[:MLA_VT_ROWS].astype(BF16)


def _mla_attn_kernel(q_ref, k_ref, vt_ref, o_ref, m_scr, acc_scr, sa_scr, sb_scr, *, tq, tk):
    qi = pl.program_id(2)
    heads = range(q_ref.shape[0])
    m_scr[...] = jnp.full(m_scr.shape, -jnp.inf, F32)
    acc_scr[...] = jnp.zeros(acc_scr.shape, F32)

    def scores(j, buf):
        for h in heads:
            buf[h] = _dot_nt(k_ref[h, pl.ds(pl.multiple_of(j * tk, tk), tk), :], q_ref[h])

    def update(j, buf, masked):
        off = pl.multiple_of(j * tk, tk)
        for h in heads:
            st = buf[h]
            if masked:
                kpos = off + lax.broadcasted_iota(jnp.int32, (tk, tq), 0)
                qpos = qi * tq + lax.broadcasted_iota(jnp.int32, (tk, tq), 1)
                st = jnp.where(kpos <= qpos, st, -jnp.inf)
            m_prev = m_scr[h]
            m_new = jnp.maximum(m_prev, jnp.max(st, axis=0, keepdims=True))
            p = jnp.exp2(st - m_new).astype(BF16)
            acc_scr[h] = jnp.exp2(m_prev - m_new) * acc_scr[h] + _dot(vt_ref[h, :, pl.ds(off, tk)], p)
            m_scr[h] = m_new

    scores(0, sa_scr)

    def body(jj, c):
        j = 2 * jj
        scores(j + 1, sb_scr)
        update(j, sa_scr, False)
        scores(j + 2, sa_scr)
        update(j + 1, sb_scr, False)
        return c

    lax.fori_loop(0, qi // 2, body, 0)

    @pl.when(qi % 2 == 0)
    def _():
        update(qi, sa_scr, True)

    @pl.when(qi % 2 == 1)
    def _():
        scores(qi, sb_scr)
        update(qi - 1, sa_scr, False)
        update(qi, sb_scr, True)

    for h in heads:
        acc = acc_scr[h]
        acc = jnp.concatenate([acc, jnp.zeros((LANES - MLA_VT_ROWS, tq), F32)], axis=0).T
        o_ref[h] = (acc / acc[:, MLA_V:MLA_V + 1]).astype(o_ref.dtype)


def _rope_tables(S, half, theta, lane_off):
    inv = theta ** (-jnp.arange(half, dtype=F32) / half)
    ang = jnp.arange(S, dtype=F32)[:, None] * inv
    cos = jnp.zeros((S, LANES), F32).at[:, lane_off:lane_off + 2 * half].set(jnp.tile(jnp.cos(ang), (1, 2)))
    sin = jnp.zeros((S, LANES), F32).at[:, lane_off:lane_off + 2 * half].set(jnp.tile(jnp.sin(ang), (1, 2)))
    return cos, sin


def _rot_half_cols(w, half):
    return jnp.concatenate([-w[..., half:], w[..., :half]], axis=-1)


def _pad_heads(w, heads, width):
    K = w.shape[0]
    w = w.reshape(K, heads, width)
    return jnp.pad(w, ((0, 0), (0, 0), (0, LANES - width))).reshape(K, heads * LANES)


def _mla(x, g, w_in, q_norm, w_qb, kv_norm, w_kvb, w_out, B, S, tm=512, tq=512, tk=512, hp=4):
    T, D = x.shape
    H = MLA_HEADS
    half = MLA_ROPE // 2
    o1 = MLA_Q_LORA + MLA_KV_LORA
    w_r = w_in[:, o1:]
    tile_r = jnp.zeros((D, LANES), F32).at[:, MLA_NOPE:MLA_NOPE + MLA_ROPE].set(w_r)
    tile_rr = jnp.zeros((D, LANES), F32).at[:, MLA_NOPE:MLA_NOPE + MLA_ROPE].set(_rot_half_cols(w_r, half))
    w_lat = jnp.concatenate([w_in[:, :o1], tile_r, tile_rr], axis=1).astype(BF16)
    lat = _norm_linear(x, g, w_lat, F32, tn=w_lat.shape[1])

    wq3 = w_qb.reshape(MLA_Q_LORA, H, MLA_NOPE + MLA_ROPE)
    wq_rot = jnp.concatenate([jnp.zeros_like(wq3[..., :MLA_NOPE]), _rot_half_cols(wq3[..., MLA_NOPE:], half)], axis=-1)
    wq = _pad_heads(w_qb, H, MLA_NOPE + MLA_ROPE).astype(BF16)
    wqr = _pad_heads(wq_rot.reshape(MLA_Q_LORA, -1), H, MLA_NOPE + MLA_ROPE).astype(BF16)
    wkv3 = w_kvb.reshape(MLA_KV_LORA, H, MLA_NOPE + MLA_V)
    wk = _pad_heads(wkv3[..., :MLA_NOPE].reshape(MLA_KV_LORA, -1), H, MLA_NOPE).astype(BF16)
    wv = _pad_heads(wkv3[..., MLA_NOPE:].reshape(MLA_KV_LORA, -1), H, MLA_V).astype(BF16)
    cos, sin = _rope_tables(S, half, MLA_THETA, MLA_NOPE)

    nS = S // tm
    full = lambda shp: pl.BlockSpec(shp, lambda i: (0,) * len(shp))
    hm_spec = pl.BlockSpec((H, tm, LANES), lambda i: (0, i, 0))
    hm_shape = jax.ShapeDtypeStruct((H, T, LANES), BF16)
    q, k, v = pl.pallas_call(
        _mla_qkv_kernel,
        grid=(T // tm,),
        in_specs=[pl.BlockSpec((tm, lat.shape[1]), lambda i: (i, 0)),
                  full((1, MLA_Q_LORA)), full((1, MLA_KV_LORA)),
                  full(wq.shape), full(wqr.shape), full(wk.shape), full(wv.shape),
                  pl.BlockSpec((tm, LANES), lambda i: (i % nS, 0)),
                  pl.BlockSpec((tm, LANES), lambda i: (i % nS, 0))],
        out_specs=[hm_spec, hm_spec, pl.BlockSpec((H, MLA_VT_ROWS, tm), lambda i: (0, 0, i))],
        out_shape=[hm_shape, hm_shape, jax.ShapeDtypeStruct((H, MLA_VT_ROWS, T), BF16)],
        compiler_params=_cparams("parallel"),
    )(lat, q_norm.reshape(1, -1), kv_norm.reshape(1, -1), wq, wqr, wk, wv, cos, sin)

    q4, k4 = (t.reshape(H, B, S, LANES) for t in (q, k))
    assert tq == tk and H % hp == 0
    o = pl.pallas_call(
        functools.partial(_mla_attn_kernel, tq=tq, tk=tk),
        grid=(B, H // hp, S // tq),
        in_specs=[pl.BlockSpec((hp, None, tq, LANES), lambda b, h, qi: (h, b, qi, 0)),
                  pl.BlockSpec((hp, None, S, LANES), lambda b, h, qi: (h, b, 0, 0), pipeline_mode=pl.Buffered(1)),
                  pl.BlockSpec((hp, MLA_VT_ROWS, S), lambda b, h, qi: (h, 0, b), pipeline_mode=pl.Buffered(1))],
        out_specs=pl.BlockSpec((hp, None, tq, LANES), lambda b, h, qi: (h, b, qi, 0)),
        out_shape=jax.ShapeDtypeStruct((H, B, S, LANES), BF16),
        scratch_shapes=[pltpu.VMEM((hp, 1, tq), F32), pltpu.VMEM((hp, MLA_VT_ROWS, tq), F32),
                        pltpu.VMEM((hp, tk, tq), F32), pltpu.VMEM((hp, tk, tq), F32)],
        compiler_params=_cparams("parallel", "parallel", "arbitrary"),
    )(q4, k4, v)

    w_o = jnp.pad(w_out.reshape(H, MLA_V, D), ((0, 0), (0, LANES - MLA_V), (0, 0))).reshape(H * LANES, D)
    return _linear_res(o.reshape(H, T, LANES), w_o.astype(BF16), x, head_major=True)


def _ret_kernel(q_ref, k_ref, v_ref, g_ref, cos_ref, sin_ref, gnw_ref, y_ref, state_scr, *, C):
    @pl.when(pl.program_id(1) == 0)
    def _():
        state_scr[...] = jnp.zeros(state_scr.shape, F32)

    half = RET_QK // 2
    cos = cos_ref[...]
    sin = sin_ref[...]
    ii = lax.broadcasted_iota(jnp.int32, (C, C), 0)
    jj = lax.broadcasted_iota(jnp.int32, (C, C), 1)
    diff = (ii - jj).astype(F32)
    causal = ii >= jj
    idx = lax.broadcasted_iota(jnp.int32, (C, 1), 0).astype(F32)

    def rope(t):
        t1, t2 = t[:, :half], t[:, half:]
        return jnp.concatenate([t1 * cos - t2 * sin, t1 * sin + t2 * cos], axis=1)

    for h in range(RET_HEADS):
        lg = math.log1p(-(2.0 ** (-5.0 - h)))
        intra = jnp.exp(jnp.where(causal, lg * diff, -jnp.inf))
        q_dec = jnp.exp(lg * (idx + 1.0))
        k_dec = jnp.exp(lg * (C - 1.0 - idx))
        c_dec = math.exp(lg * C)
        q = rope(q_ref[:, h * RET_QK:(h + 1) * RET_QK].astype(F32))
        k = rope(k_ref[:, h * RET_QK:(h + 1) * RET_QK].astype(F32)) * (RET_QK ** -0.5)
        v = v_ref[:, h * RET_V:(h + 1) * RET_V]
        st = state_scr[h]
        a = _dot_nt(q.astype(BF16), k.astype(BF16)) * intra
        o = _dot(a.astype(BF16), v) + _dot((q * q_dec).astype(BF16), st.astype(BF16))
        state_scr[h] = st * c_dec + _dot((k * k_dec).T.astype(BF16), v)
        mu = jnp.mean(o, axis=-1, keepdims=True)
        oc = o - mu
        var = jnp.mean(oc * oc, axis=-1, keepdims=True)
        on = oc * lax.rsqrt(var + EPS) * gnw_ref[:, h * RET_V:(h + 1) * RET_V]
        gt = g_ref[:, h * RET_V:(h + 1) * RET_V].astype(F32)
        y_ref[:, h * RET_V:(h + 1) * RET_V] = (gt * jax.nn.sigmoid(gt) * on).astype(y_ref.dtype)


def _retention(x, g, w_in, gn_w, w_out, B, S, C=128):
    T, D = x.shape
    nqk, nv = RET_HEADS * RET_QK, RET_HEADS * RET_V
    proj = _norm_linear(x, g, w_in.astype(BF16), BF16)
    half = RET_QK // 2
    ang = jnp.arange(S, dtype=F32)[:, None] * (RET_THETA ** (-jnp.arange(half, dtype=F32) / half))
    cos, sin = jnp.cos(ang), jnp.sin(ang)
    nc = S // C
    y = pl.pallas_call(
        functools.partial(_ret_kernel, C=C),
        grid=(B, nc),
        in_specs=[pl.BlockSpec((C, nqk), lambda b, c: (b * nc + c, 0)),
                  pl.BlockSpec((C, nqk), lambda b, c: (b * nc + c, 1)),
                  pl.BlockSpec((C, nv), lambda b, c: (b * nc + c, (2 * nqk) // nv)),
                  pl.BlockSpec((C, nv), lambda b, c: (b * nc + c, (2 * nqk) // nv + 1)),
                  pl.BlockSpec((C, RET_QK // 2), lambda b, c: (c, 0)),
                  pl.BlockSpec((C, RET_QK // 2), lambda b, c: (c, 0)),
                  pl.BlockSpec((1, nv), lambda b, c: (0, 0))],
        out_specs=pl.BlockSpec((C, nv), lambda b, c: (b * nc + c, 0)),
        out_shape=jax.ShapeDtypeStruct((T, nv), BF16),
        scratch_shapes=[pltpu.VMEM((RET_HEADS, RET_QK, RET_V), F32)],
        compiler_params=_cparams("parallel", "arbitrary"),
    )(proj, proj, proj, proj, cos, sin, gn_w.reshape(1, nv))
    return _linear_res(y, w_out.astype(BF16), x)


def _split3(a):
    a1 = a.astype(BF16)
    r = a - a1.astype(F32)
    a2 = r.astype(BF16)
    a3 = (r - a2.astype(F32)).astype(BF16)
    return a1, a2, a3


SSD_COLB = 1024


def _ssd_kernel(*refs, C):
    nz, nx = SSD_D_INNER // SSD_COLB, SSD_CONV_DIM // SSD_COLB
    z_refs, x_refs = refs[:nz], refs[nz:nz + nx]
    (dt_ref, cw_ref, cb_ref, dtb_ref, alog_ref, dsk_ref, nw_ref, y_ref,
     xe_scr, act_scr, st_scr, y_scr) = refs[nz + nx:]
    G, N, P = SSD_GROUPS, SSD_STATE, SSD_P
    DI = SSD_D_INNER
    pairs_per_group = (SSD_HEADS // G) // 2
    tiles_per_colb = SSD_COLB // LANES

    HALO = 8

    @pl.when(pl.program_id(1) == 0)
    def _():
        xe_scr[0:HALO, :] = jnp.zeros((HALO, xe_scr.shape[1]), F32)
        st_scr[...] = jnp.zeros(st_scr.shape, F32)

    for kx in range(nx):
        xe_scr[HALO:HALO + C, kx * SSD_COLB:(kx + 1) * SSD_COLB] = x_refs[kx][...].astype(F32)
    CB = 512
    for cb in range(SSD_CONV_DIM // CB):
        cs = slice(cb * CB, (cb + 1) * CB)
        acc = jnp.broadcast_to(cb_ref[:, cs], (C, CB))
        for kk in range(SSD_CONV):
            off = HALO - (SSD_CONV - 1) + kk
            acc = acc + cw_ref[kk:kk + 1, cs] * xe_scr[off:off + C, cs]
        act_scr[:, cs] = (acc * jax.nn.sigmoid(acc)).astype(BF16)
    xe_scr[0:HALO, :] = xe_scr[C:C + HALO, :]

    dtr = dt_ref[...] + dtb_ref[...]
    dt = jnp.maximum(dtr, 0.0) + jnp.log1p(jnp.exp(-jnp.abs(dtr)))
    a = dt * (-jnp.exp(alog_ref[...]))
    ii = lax.broadcasted_iota(jnp.int32, (C, C), 0)
    jj = lax.broadcasted_iota(jnp.int32, (C, C), 1)
    causal = ii >= jj
    tril = jnp.where(causal, 1.0, 0.0).astype(BF16)
    a1, a2, a3 = _split3(a)
    a_cs = _dot(tril, a1) + _dot(tril, a2) + _dot(tril, a3)
    a_end = a_cs[C - 1:C, :]
    w_end = jnp.exp(a_end - a_cs) * dt
    ea = jnp.exp(a_cs)
    eend = jnp.exp(a_end)
    a_cs_t = a_cs.T
    dt_t = dt.T
    lane = lax.broadcasted_iota(jnp.int32, (C, LANES), 1)
    lo = lane < P

    def pair(t, ha):
        return jnp.where(lo[:t.shape[0]], t[:, ha:ha + 1], t[:, ha + 1:ha + 2])

    for g in range(G):
        bm = act_scr[:, DI + g * N:DI + (g + 1) * N]
        cm = act_scr[:, DI + G * N + g * N:DI + G * N + (g + 1) * N]
        cbm = _dot_nt(cm, bm)
        bm_t = bm.astype(F32).T.astype(BF16)
        ssq = jnp.zeros((C, 1), F32)
        for pj in range(pairs_per_group):
            j = g * pairs_per_group + pj
            ha = 2 * j
            cs = slice(j * LANES, (j + 1) * LANES)
            xp = act_scr[:, cs]
            ys = []
            for hh in (ha, ha + 1):
                seg = a_cs[:, hh:hh + 1] - a_cs_t[hh:hh + 1, :]
                dec = jnp.exp(jnp.where(causal, seg, -jnp.inf))
                mm = cbm * dec * dt_t[hh:hh + 1, :]
                ys.append(_dot(mm.astype(BF16), xp))
            y = jnp.where(lo, ys[0], ys[1])
            st = st_scr[:, cs]
            y = y + pair(ea, ha) * _dot(cm, st.astype(BF16))
            xpf = xp.astype(F32)
            xw = (xpf * pair(w_end, ha)).astype(BF16)
            st_scr[:, cs] = st * pair(eend, ha) + _dot(bm_t, xw)
            y = y + pair(dsk_ref[...], ha) * xpf
            jz = j % tiles_per_colb
            zt = z_refs[j // tiles_per_colb][:, jz * LANES:(jz + 1) * LANES].astype(F32)
            y = y * (zt * jax.nn.sigmoid(zt))
            ssq = ssq + jnp.sum(y * y, axis=-1, keepdims=True)
            y_scr[:, cs] = y
        gw = pairs_per_group * LANES
        gs = slice(g * gw, (g + 1) * gw)
        rs = lax.rsqrt(ssq / float(gw) + EPS)
        y_ref[:, gs] = (y_scr[:, gs] * rs * nw_ref[:, gs]).astype(y_ref.dtype)


def _pad_lanes(v):
    return jnp.pad(v.astype(F32), (0, LANES - v.shape[0])).reshape(1, LANES)


def _ssd(x, g, w_in, conv_w, conv_b, dt_bias, A_log, D_skip, norm_w, w_out, B, S, C=128):
    T, D = x.shape
    DI, CD = SSD_D_INNER, SSD_CONV_DIM
    w_main = w_in[:, :DI + CD].astype(BF16)
    w_dt = jnp.pad(w_in[:, DI + CD:], ((0, 0), (0, LANES - SSD_HEADS))).astype(BF16)
    proj = _norm_linear(x, g, w_main, BF16)
    dt = _norm_linear(x, g, w_dt, F32)
    nc = S // C
    full = lambda shp: pl.BlockSpec(shp, lambda b, c: (0,) * len(shp))
    ncolb = (DI + CD) // SSD_COLB

    def colb(kc):
        return pl.BlockSpec((C, SSD_COLB), lambda b, c: (b * nc + c, kc))

    y = pl.pallas_call(
        functools.partial(_ssd_kernel, C=C),
        grid=(B, nc),
        in_specs=[colb(kc) for kc in range(ncolb)] + [
                  pl.BlockSpec((C, LANES), lambda b, c: (b * nc + c, 0)),
                  full((SSD_CONV, CD)), full((1, CD)), full((1, LANES)), full((1, LANES)), full((1, LANES)),
                  full((1, DI))],
        out_specs=pl.BlockSpec((C, DI), lambda b, c: (b * nc + c, 0)),
        out_shape=jax.ShapeDtypeStruct((T, DI), BF16),
        scratch_shapes=[pltpu.VMEM((C + 8, CD), F32), pltpu.VMEM((C, CD), BF16),
                        pltpu.VMEM((SSD_STATE, DI), F32), pltpu.VMEM((C, DI), F32)],
        compiler_params=_cparams("parallel", "arbitrary"),
    )(*([proj] * ncolb), dt, conv_w, conv_b.reshape(1, CD), _pad_lanes(dt_bias), _pad_lanes(A_log),
      _pad_lanes(D_skip), norm_w.reshape(1, DI))
    return _linear_res(y, w_out.astype(BF16), x)


ROW_TILE = 8


def _rows_to_tiles(ref, val):
    n = val.shape[0]
    for s in range(ROW_TILE):
        ref[pl.ds(s, n, stride=ROW_TILE), :] = val[:, s * LANES:(s + 1) * LANES]


def _tiles_to_rows(ref, n):
    return jnp.concatenate([ref[pl.ds(s, n, stride=ROW_TILE), :] for s in range(ROW_TILE)], axis=1)


def _router_kernel(x_ref, g_ref, w1_ref, w2_ref, b_ref, h_ref, route_ref, cnt_ref, carry_scr):
    E = MOE_EXPERTS

    @pl.when(pl.program_id(0) == 0)
    def _():
        carry_scr[...] = jnp.zeros(carry_scr.shape, F32)

    h = _rms(x_ref[...], g_ref[...])
    _rows_to_tiles(h_ref, h)
    tm = h.shape[0]
    h1 = h.astype(BF16)
    h2 = (h - h1.astype(F32)).astype(BF16)
    logits = _dot(h1, w1_ref[...]) + _dot(h1, w2_ref[...]) + _dot(h2, w1_ref[...]) + b_ref[...]
    lane = lax.broadcasted_iota(jnp.int32, (tm, LANES), 1)
    lanef = lane.astype(F32)
    big = float(LANES)
    glog = jnp.where((lane >= E) & (lane < E + MOE_GROUPS), logits, -jnp.inf)
    gmax = jnp.max(glog, axis=-1, keepdims=True)
    gval = 1.0 / jnp.sum(jnp.exp(glog - gmax), axis=-1, keepdims=True)
    gidx = jnp.min(jnp.where(glog == gmax, lanef, big), axis=-1, keepdims=True) - float(E)
    in_group = (lane < E) & ((lane // MOE_EPG).astype(F32) == gidx)
    el = jnp.where(in_group, logits, -jnp.inf)
    v1 = jnp.max(el, axis=-1, keepdims=True)
    i1 = jnp.min(jnp.where(el == v1, lanef, big), axis=-1, keepdims=True)
    el2 = jnp.where(lanef == i1, -jnp.inf, el)
    v2 = jnp.max(el2, axis=-1, keepdims=True)
    i2 = jnp.min(jnp.where(el2 == v2, lanef, big), axis=-1, keepdims=True)
    e2 = jnp.exp(v2 - v1)
    wa = gval / (1.0 + e2)
    wb = gval * e2 / (1.0 + e2)
    oh1 = jnp.where(lanef == i1, 1.0, 0.0)
    oh2 = jnp.where(lanef == i2, 1.0, 0.0)
    ohs = oh1 + oh2
    ii = lax.broadcasted_iota(jnp.int32, (tm, tm), 0)
    jj = lax.broadcasted_iota(jnp.int32, (tm, tm), 1)
    stril = jnp.where(ii > jj, 1.0, 0.0).astype(BF16)
    before = _dot(stril, ohs.astype(BF16)) + carry_scr[0:1, :]
    r1 = jnp.sum(before * oh1, axis=-1, keepdims=True)
    r2 = jnp.sum(before * oh2, axis=-1, keepdims=True)
    carry = carry_scr[0:1, :] + jnp.sum(ohs, axis=0, keepdims=True)
    carry_scr[...] = jnp.broadcast_to(carry, carry_scr.shape)
    cnt_ref[...] = jnp.broadcast_to(carry, cnt_ref.shape)
    route = jnp.zeros((tm, LANES), F32)
    for ln, val in enumerate((i1, i2, wa, wb, r1, r2)):
        route = jnp.where(lane == ln, val, route)
    route_ref[...] = route


def _dispatch_kernel(pos_ref, lo_ref, hi_ref, h_ref, xg_hbm, zero_scr, sem, zsem):
    ntok = pos_ref.shape[0] // MOE_TOPK
    base = pl.program_id(0) * TOK_TILE

    @pl.when(pl.program_id(0) == 0)
    def _():
        zero_scr[...] = jnp.zeros(zero_scr.shape, F32)
        blk = zero_scr.shape[0]

        def zero_copy(r, n):
            return pltpu.make_async_copy(zero_scr.at[pl.ds(0, n)], xg_hbm.at[pl.ds(pl.multiple_of(r * n, n), n)], zsem)

        def loop(lo, hi, n, wait):
            def body(r, c):
                cp = zero_copy(r, n)
                cp.wait() if wait else cp.start()
                return c
            lax.fori_loop(lo, hi, body, 0)

        tail_lo = hi_ref[MOE_EXPERTS - 1] * ROW_TILE // blk
        tail_hi = xg_hbm.shape[0] // blk
        for wait in (False, True):
            for e in range(MOE_EXPERTS):
                loop(lo_ref[e], hi_ref[e], ROW_TILE, wait)
            loop(tail_lo, tail_hi, blk, wait)

    def copy(t, k):
        row = pl.multiple_of(pos_ref[k * ntok + base + t] * ROW_TILE, ROW_TILE)
        return pltpu.make_async_copy(h_ref.at[pl.ds(t * ROW_TILE, ROW_TILE)], xg_hbm.at[pl.ds(row, ROW_TILE)], sem)

    for t in range(TOK_TILE):
        for k in range(MOE_TOPK):
            copy(t, k).start(priority=k)
    same_size = pltpu.make_async_copy(h_ref.at[pl.ds(0, ROW_TILE)], xg_hbm.at[pl.ds(0, ROW_TILE)], sem)
    for _ in range(TOK_TILE * MOE_TOPK):
        same_size.wait()


def _expert_kernel(be_ref, nu_ref, x_ref, wg_ref, wu_ref, wd_ref, o_ref, wg_scr, wu_scr, wd_scr):
    i = pl.program_id(0)
    used = i < nu_ref[0]
    n = x_ref.shape[0] // ROW_TILE

    @pl.when(used & ((i == 0) | (be_ref[i] != be_ref[jnp.maximum(i - 1, 0)])))
    def _():
        wg_scr[...] = wg_ref[...].astype(BF16)
        wu_scr[...] = wu_ref[...].astype(BF16)
        wd_scr[...] = wd_ref[...].astype(BF16)

    @pl.when(used)
    def _():
        xb = _tiles_to_rows(x_ref, n).astype(BF16)
        hg = _dot(xb, wg_scr[...])
        hu = _dot(xb, wu_scr[...])
        _rows_to_tiles(o_ref, _dot((hg * jax.nn.sigmoid(hg) * hu).astype(BF16), wd_scr[...]))

    @pl.when(jnp.logical_not(used))
    def _():
        o_ref[...] = jnp.zeros(o_ref.shape, F32)


def _ple_kernel(pos_ref, x_ref, rt_ref, p_ref, g_ref, gf_ref, wg_ref, wp_ref, yb_hbm, o_ref,
                buf_a, buf_b, sems, *, final):
    tp = x_ref.shape[0]
    i = pl.program_id(0)
    last = pl.num_programs(0) - 1
    slot = i % 2
    bufs = (buf_a, buf_b)

    def copy(tile, sl, t, k):
        row = pl.multiple_of(pos_ref[k * (pos_ref.shape[0] // MOE_TOPK) + tile * tp + t] * ROW_TILE, ROW_TILE)
        dst = bufs[k].at[sl, pl.ds(pl.multiple_of(t * ROW_TILE, ROW_TILE), ROW_TILE)]
        return pltpu.make_async_copy(yb_hbm.at[pl.ds(row, ROW_TILE)], dst, sems.at[sl])

    def wait_slot(sl):
        for k in range(MOE_TOPK):
            same_size = pltpu.make_async_copy(yb_hbm.at[pl.ds(0, ROW_TILE)], bufs[k].at[sl, pl.ds(0, ROW_TILE)],
                                              sems.at[sl])
            for _ in range(tp):
                same_size.wait()

    @pl.when(i == 0)
    def _():
        def first(t, c):
            for k in range(MOE_TOPK):
                copy(0, 0, t, k).start(priority=k)
            return c
        lax.fori_loop(0, tp, first, 0, unroll=8)

    wait_slot(slot)
    nxt = jnp.minimum(i + 1, last)
    for t in range(tp):
        for k in range(MOE_TOPK):
            copy(nxt, 1 - slot, t, k).start(priority=k)

    ple = _dot(p_ref[...].astype(BF16), wp_ref[...])
    rt = rt_ref[...]
    ya = _tiles_to_rows(buf_a.at[slot], tp)
    yb = _tiles_to_rows(buf_b.at[slot], tp)
    x1 = x_ref[...] + rt[:, 2:3] * ya + rt[:, 3:4] * yb
    gate = jax.nn.sigmoid(_dot(_rms(x1, g_ref[...]).astype(BF16), wg_ref[...]))
    out = x1 + ple * gate
    if final:
        out = _rms(out, gf_ref[...])
    o_ref[...] = out

    @pl.when(i == last)
    def _():
        wait_slot(1 - slot)


def _moe_ple(x, layer, p_all, norm_moe, w_group, b_group, w_expert, b_expert, w_gate, w_up, w_down,
             norm_ple, ple_w, ple_w_gate, norm_final, final, tm=512):
    T, D = x.shape
    E, BLK = MOE_EXPERTS, MOE_BLOCK
    A = T * MOE_TOPK
    nb = A // BLK + E
    P = nb * BLK

    w_r = jnp.zeros((D, LANES), F32).at[:, :E].set(w_expert).at[:, E:E + MOE_GROUPS].set(w_group)
    b_r = jnp.zeros((1, LANES), F32).at[0, :E].set(b_expert).at[0, E:E + MOE_GROUPS].set(b_group)
    w_r1 = w_r.astype(BF16)
    w_r2 = (w_r - w_r1.astype(F32)).astype(BF16)
    full = lambda shp: pl.BlockSpec(shp, lambda i: (0,) * len(shp))
    h, route, cnt = pl.pallas_call(
        _router_kernel,
        grid=(T // tm,),
        in_specs=[pl.BlockSpec((tm, D), lambda i: (i, 0)), full((1, D)), full((D, LANES)), full((D, LANES)),
                  full((1, LANES))],
        out_specs=[pl.BlockSpec((tm * ROW_TILE, LANES), lambda i: (i, 0)),
                   pl.BlockSpec((tm, LANES), lambda i: (i, 0)), full((8, LANES))],
        out_shape=[jax.ShapeDtypeStruct((T * ROW_TILE, LANES), F32), jax.ShapeDtypeStruct((T, LANES), F32),
                   jax.ShapeDtypeStruct((8, LANES), F32)],
        scratch_shapes=[pltpu.VMEM((8, LANES), F32)],
        compiler_params=_cparams("arbitrary"),
    )(x, norm_moe.reshape(1, D), w_r1, w_r2, b_r)

    counts = cnt[0, :E].astype(jnp.int32)
    padded = (counts + BLK - 1) // BLK * BLK
    pad_end = jnp.cumsum(padded)
    pad_start = pad_end - padded
    pos = jnp.concatenate([pad_start[route[:, k].astype(jnp.int32)] + route[:, 4 + k].astype(jnp.int32)
                           for k in range(MOE_TOPK)])
    blk_start = jnp.arange(nb, dtype=jnp.int32) * BLK
    blk_expert = jnp.minimum(jnp.sum((pad_end[None, :] <= blk_start[:, None]).astype(jnp.int32), axis=1), E - 1)
    n_used = (pad_end[-1:] // BLK).astype(jnp.int32)

    assert D == ROW_TILE * LANES
    xg = pl.pallas_call(
        _dispatch_kernel,
        grid_spec=pltpu.PrefetchScalarGridSpec(
            num_scalar_prefetch=3, grid=(T // TOK_TILE,),
            in_specs=[pl.BlockSpec((TOK_TILE * ROW_TILE, LANES), lambda i, ps, lo, hi: (i, 0))],
            out_specs=pl.BlockSpec(memory_space=pl.ANY),
            scratch_shapes=[pltpu.VMEM((BLK * ROW_TILE, LANES), F32), pltpu.SemaphoreType.DMA,
                            pltpu.SemaphoreType.DMA]),
        out_shape=jax.ShapeDtypeStruct((P * ROW_TILE, LANES), F32),
        compiler_params=_cparams("arbitrary"),
    )(pos, pad_start + counts, pad_end, h)

    def wmap(i, be, nu):
        return (layer, be[jnp.minimum(i, nu[0] - 1)], 0, 0)

    def xmap(i, be, nu):
        return (jnp.minimum(i, nu[0] - 1), 0)

    yb = pl.pallas_call(
        _expert_kernel,
        grid_spec=pltpu.PrefetchScalarGridSpec(
            num_scalar_prefetch=2, grid=(nb,),
            in_specs=[pl.BlockSpec((BLK * ROW_TILE, LANES), xmap),
                      pl.BlockSpec((None, None, D, MOE_FF), wmap), pl.BlockSpec((None, None, D, MOE_FF), wmap),
                      pl.BlockSpec((None, None, MOE_FF, D), wmap)],
            out_specs=pl.BlockSpec((BLK * ROW_TILE, LANES), lambda i, be, nu: (i, 0)),
            scratch_shapes=[pltpu.VMEM((D, MOE_FF), BF16), pltpu.VMEM((D, MOE_FF), BF16),
                            pltpu.VMEM((MOE_FF, D), BF16)]),
        out_shape=jax.ShapeDtypeStruct((P * ROW_TILE, LANES), F32),
        compiler_params=_cparams("arbitrary"),
    )(blk_expert, n_used, xg, w_gate, w_up, w_down)

    tp = TOK_TILE
    PD = p_all.shape[-1]
    return pl.pallas_call(
        functools.partial(_ple_kernel, final=final),
        grid_spec=pltpu.PrefetchScalarGridSpec(
            num_scalar_prefetch=1, grid=(T // tp,),
            in_specs=[pl.BlockSpec((tp, D), lambda i, ps: (i, 0)),
                      pl.BlockSpec((tp, LANES), lambda i, ps: (i, 0)),
                      pl.BlockSpec((None, tp, PD), lambda i, ps: (layer, i, 0)),
                      pl.BlockSpec((1, D), lambda i, ps: (0, 0)), pl.BlockSpec((1, D), lambda i, ps: (0, 0)),
                      pl.BlockSpec((D, D), lambda i, ps: (0, 0)), pl.BlockSpec((PD, D), lambda i, ps: (0, 0)),
                      pl.BlockSpec(memory_space=pl.ANY)],
            out_specs=pl.BlockSpec((tp, D), lambda i, ps: (i, 0)),
            scratch_shapes=[pltpu.VMEM((2, tp * ROW_TILE, LANES), F32), pltpu.VMEM((2, tp * ROW_TILE, LANES), F32),
                            pltpu.SemaphoreType.DMA((2,))]),
        out_shape=jax.ShapeDtypeStruct((T, D), F32),
        compiler_params=_cparams("arbitrary"),
    )(pos, x, route, p_all, norm_ple.reshape(1, D), norm_final.reshape(1, D),
      ple_w_gate.astype(BF16), ple_w.astype(BF16), yb)


def kernel(x, p, norm_mix, norm_moe, norm_ple, norm_final, dsa_w_in, dsa_w_out, mla_w_in, mla_q_norm, mla_w_qb,
           mla_kv_norm, mla_w_kvb, mla_w_out, ret_w_in, ret_gn_w, ret_w_out, ssd_w_in, ssd_conv_w, ssd_conv_b,
           ssd_dt_bias, ssd_A_log, ssd_D, ssd_norm_w, ssd_w_out, moe_w_group, moe_b_group, moe_w_expert,
           moe_b_expert, moe_w_gate, moe_w_up, moe_w_down, ple_w, ple_w_gate):
    B, S, D = x.shape
    depth = p.shape[0]
    xt = x.reshape(B * S, D)
    for i in range(depth):
        kind = i % 4
        if kind == 0:
            xt = _dilated_attention(xt, norm_mix[i], dsa_w_in, dsa_w_out, B, S)
        elif kind == 1:
            xt = _mla(xt, norm_mix[i], mla_w_in, mla_q_norm, mla_w_qb, mla_kv_norm, mla_w_kvb, mla_w_out, B, S)
        elif kind == 2:
            xt = _retention(xt, norm_mix[i], ret_w_in, ret_gn_w, ret_w_out, B, S)
        else:
            xt = _ssd(xt, norm_mix[i], ssd_w_in, ssd_conv_w, ssd_conv_b, ssd_dt_bias, ssd_A_log, ssd_D,
                      ssd_norm_w, ssd_w_out, B, S)
        xt = _moe_ple(xt, i, p.reshape(depth, B * S, -1), norm_moe[i], moe_w_group[i], moe_b_group[i],
                      moe_w_expert[i], moe_b_expert[i], moe_w_gate, moe_w_up, moe_w_down, norm_ple[i], ple_w[i],
                      ple_w_gate[i], norm_final, final=(i == depth - 1))
    return xt.reshape(B, S, D)
```

```python
import functools
import math

import jax
import jax.numpy as jnp
from jax import lax
from jax.experimental import pallas as pl
from jax.experimental.pallas import tpu as pltpu

F32 = jnp.float32
BF16 = jnp.bfloat16
EPS = 1e-6
LANES = 128
VMEM_LIMIT = 48 * 1024 * 1024

DSA_PATTERNS = ((128, 1), (512, 4), (2048, 16))
DSA_HEADS = 8
DSA_SPAN = 2048
MLA_HEADS = 16
MLA_Q_LORA, MLA_KV_LORA, MLA_NOPE, MLA_ROPE, MLA_V = 256, 128, 64, 32, 64
MLA_THETA = 10000.0
MLA_VT_ROWS = 80
RET_HEADS, RET_QK, RET_V = 4, 256, 512
RET_THETA = 10000.0
SSD_D_INNER, SSD_HEADS, SSD_GROUPS, SSD_STATE, SSD_CONV, SSD_P = 2048, 32, 4, 128, 4, 64
SSD_CONV_DIM = SSD_D_INNER + 2 * SSD_GROUPS * SSD_STATE
MOE_GROUPS, MOE_EPG, MOE_EXPERTS, MOE_TOPK, MOE_FF = 4, 8, 32, 2, 512
MOE_BLOCK = 256
TOK_TILE = 256


def _cparams(*sem):
    return pltpu.CompilerParams(dimension_semantics=sem, vmem_limit_bytes=VMEM_LIMIT)


def _rms(x, g):
    return x * lax.rsqrt(jnp.mean(x * x, axis=-1, keepdims=True) + EPS) * g


def _dot(a, b):
    return jnp.dot(a, b, preferred_element_type=F32)


def _dot_nt(a, b):
    return lax.dot_general(a, b, (((1,), (1,)), ((), ())), preferred_element_type=F32)


def _norm_linear_kernel(x_ref, g_ref, w_ref, o_ref, h_scr, *, head_major):
    @pl.when(pl.program_id(1) == 0)
    def _():
        h_scr[...] = _rms(x_ref[...], g_ref[...]).astype(BF16)

    r = _dot(h_scr[...], w_ref[...])
    if head_major:
        for c in range(r.shape[1] // LANES):
            o_ref[c] = r[:, c * LANES:(c + 1) * LANES].astype(o_ref.dtype)
    else:
        o_ref[...] = r.astype(o_ref.dtype)


def _norm_linear(x, g, w, out_dtype, *, head_major=False, tm=1024, tn=1024):
    T, D = x.shape
    N = w.shape[1]
    tn = min(tn, N)
    assert T % tm == 0 and N % tn == 0 and tn % LANES == 0
    if head_major:
        out_shape = jax.ShapeDtypeStruct((N // LANES, T, LANES), out_dtype)
        out_spec = pl.BlockSpec((tn // LANES, tm, LANES), lambda i, j: (j, i, 0))
    else:
        out_shape = jax.ShapeDtypeStruct((T, N), out_dtype)
        out_spec = pl.BlockSpec((tm, tn), lambda i, j: (i, j))
    return pl.pallas_call(
        functools.partial(_norm_linear_kernel, head_major=head_major),
        grid=(T // tm, N // tn),
        in_specs=[pl.BlockSpec((tm, D), lambda i, j: (i, 0)),
                  pl.BlockSpec((1, D), lambda i, j: (0, 0)),
                  pl.BlockSpec((D, tn), lambda i, j: (0, j))],
        out_specs=out_spec,
        out_shape=out_shape,
        scratch_shapes=[pltpu.VMEM((tm, D), BF16)],
        compiler_params=_cparams("parallel", "arbitrary"),
    )(x, g.reshape(1, D), w)


def _linear_res_kernel(a_ref, w_ref, r_ref, o_ref, *, head_major):
    if head_major:
        a = jnp.concatenate([a_ref[c] for c in range(a_ref.shape[0])], axis=1)
    else:
        a = a_ref[...]
    o_ref[...] = r_ref[...] + _dot(a, w_ref[...])


def _linear_res(a, w, res, *, head_major=False, tm=512):
    T, D = res.shape
    K = w.shape[0]
    if head_major:
        a_spec = pl.BlockSpec((K // LANES, tm, LANES), lambda i: (0, i, 0))
    else:
        a_spec = pl.BlockSpec((tm, K), lambda i: (i, 0))
    return pl.pallas_call(
        functools.partial(_linear_res_kernel, head_major=head_major),
        grid=(T // tm,),
        in_specs=[a_spec,
                  pl.BlockSpec((K, D), lambda i: (0, 0)),
                  pl.BlockSpec((tm, D), lambda i: (i, 0))],
        out_specs=pl.BlockSpec((tm, D), lambda i: (i, 0)),
        out_shape=jax.ShapeDtypeStruct((T, D), F32),
        compiler_params=_cparams("parallel"),
    )(a, w, res)


def _dsa_group_kernel(sl_ref, q_ref, k_ref, kh_ref, v_ref, vh_ref, o_ref, sm_ref, sd_ref, *, dil, rows):
    W = LANES
    i = pl.program_id(1)
    h = pl.program_id(2)
    scale = float(W) ** -0.5
    slope = sl_ref[h] * float(dil)
    qi = lax.broadcasted_iota(jnp.int32, (W, 2 * W), 0)
    kj = lax.broadcasted_iota(jnp.int32, (W, 2 * W), 1)
    steps = W + qi - kj
    band = (steps >= 0) & (steps <= W)
    first_lim = jnp.where(i == 0, W, 0)
    band_first = band & (kj >= first_lim)
    bias = (-slope) * steps.astype(F32)
    lane = lax.broadcasted_iota(jnp.int32, (W, LANES), 1)

    @pl.when(h == 0)
    def _():
        sm_ref[...] = jnp.zeros(sm_ref.shape, F32)
        sd_ref[...] = jnp.ones(sd_ref.shape, F32)

    for r in range(dil):
        cs = slice(r * LANES, (r + 1) * LANES)
        for n in range(rows // W):
            rs = slice(n * W, (n + 1) * W)
            q = q_ref[rs, cs]
            if n == 0:
                kk = jnp.concatenate([kh_ref[:, cs], k_ref[rs, cs]], axis=0)
                vv = jnp.concatenate([vh_ref[:, cs], v_ref[rs, cs]], axis=0)
                valid = band_first
            else:
                kk = k_ref[(n - 1) * W:(n + 1) * W, cs]
                vv = v_ref[(n - 1) * W:(n + 1) * W, cs]
                valid = band
            s = _dot_nt(q, kk) * scale + bias
            s = jnp.where(valid, s, -jnp.inf)
            m = jnp.max(s, axis=-1, keepdims=True)
            e = jnp.exp(s - m)
            den = jnp.sum(e, axis=-1, keepdims=True)
            o = _dot(e.astype(BF16), vv) / den
            o_ref[rs, cs] = o.astype(o_ref.dtype)
            sm_ref[rs, cs] = jnp.where(lane == h, m, sm_ref[rs, cs])
            sd_ref[rs, cs] = jnp.where(lane == h, den, sd_ref[rs, cs])


def _dsa_proj_kernel(x_ref, g_ref, w_ref, o_ref, h_scr, r_scr, *, dil):
    @pl.when(pl.program_id(1) == 0)
    def _():
        h_scr[...] = _rms(x_ref[...], g_ref[...]).astype(BF16)

    res = _dot(h_scr[...], w_ref[...])
    rows = r_scr.shape[1] // dil
    for c in range(o_ref.shape[0]):
        if dil == 1:
            o_ref[c] = res[:, c * LANES:(c + 1) * LANES].astype(o_ref.dtype)
            continue
        r_scr[c] = res[:, c * LANES:(c + 1) * LANES]
        for r in range(dil):
            o_ref[c, :, r * LANES:(r + 1) * LANES] = r_scr[c, pl.ds(r, rows, stride=dil), :].astype(o_ref.dtype)


def _dsa_proj(x, g, w, dil, tm=1024):
    T, D = x.shape
    N = w.shape[1]
    tn = DSA_HEADS * LANES
    return pl.pallas_call(
        functools.partial(_dsa_proj_kernel, dil=dil),
        grid=(T // tm, N // tn),
        in_specs=[pl.BlockSpec((tm, D), lambda i, j: (i, 0)),
                  pl.BlockSpec((1, D), lambda i, j: (0, 0)),
                  pl.BlockSpec((D, tn), lambda i, j: (0, j))],
        out_specs=pl.BlockSpec((DSA_HEADS, tm // dil, dil * LANES), lambda i, j: (j, i, 0)),
        out_shape=jax.ShapeDtypeStruct((N // LANES, T // dil, dil * LANES), BF16),
        scratch_shapes=[pltpu.VMEM((tm, D), BF16), pltpu.VMEM((DSA_HEADS, tm, LANES), F32)],
        compiler_params=_cparams("parallel", "arbitrary"),
    )(x, g.reshape(1, D), w)


def _dsa_group(qkv, slopes, dil, B, S):
    H = DSA_HEADS
    L = S // dil
    rows = DSA_SPAN // dil
    width = dil * LANES
    qv = qkv.reshape(qkv.shape[0], B, L, width)
    hb = rows // LANES
    base = 0

    def qmap(o):
        return lambda b, i, h, sl: (base + o * H + h, b, i, 0)

    def hmap(o):
        return lambda b, i, h, sl: (base + o * H + h, b, jnp.maximum(i * hb - 1, 0), 0)

    blk = (None, None, rows, width)
    hblk = (None, None, LANES, width)
    o, sm, sd = pl.pallas_call(
        functools.partial(_dsa_group_kernel, dil=dil, rows=rows),
        grid_spec=pltpu.PrefetchScalarGridSpec(
            num_scalar_prefetch=1,
            grid=(B, L // rows, H),
            in_specs=[pl.BlockSpec(blk, qmap(0)),
                      pl.BlockSpec(blk, qmap(1)), pl.BlockSpec(hblk, hmap(1)),
                      pl.BlockSpec(blk, qmap(2)), pl.BlockSpec(hblk, hmap(2))],
            out_specs=[pl.BlockSpec(blk, lambda b, i, h, sl: (h, b, i, 0)),
                       pl.BlockSpec((None, rows, width), lambda b, i, h, sl: (b, i, 0)),
                       pl.BlockSpec((None, rows, width), lambda b, i, h, sl: (b, i, 0))]),
        out_shape=[jax.ShapeDtypeStruct((H, B, L, width), BF16),
                   jax.ShapeDtypeStruct((B, L, width), F32),
                   jax.ShapeDtypeStruct((B, L, width), F32)],
        compiler_params=_cparams("parallel", "parallel", "arbitrary"),
    )(slopes, qv, qv, qv, qv, qv)
    return o.reshape(H, B * L, width), sm.reshape(B * L, width), sd.reshape(B * L, width)


def _dsa_out_kernel(o0, o1, o2, m0, m1, m2, d0, d1, d2, w_ref, x_ref, out_ref, o_scr, s_scr):
    tm = x_ref.shape[0]
    dils = [d for _, d in DSA_PATTERNS]

    def token_major(ref, scr, dil):
        if dil == 1:
            return ref[...].astype(F32)
        for r in range(dil):
            scr[pl.ds(r, tm // dil, stride=dil), :] = ref[:, r * LANES:(r + 1) * LANES].astype(F32)
        return scr[...]

    ms = [token_major(m, s_scr.at[2 * g], dils[g]) for g, m in enumerate((m0, m1, m2))]
    dn = [token_major(d, s_scr.at[2 * g + 1], dils[g]) for g, d in enumerate((d0, d1, d2))]
    mx = jnp.maximum(jnp.maximum(ms[0], ms[1]), ms[2])
    ws = [d * jnp.exp(m - mx) for d, m in zip(dn, ms)]
    tot = ws[0] + ws[1] + ws[2]
    ws = [w / tot for w in ws]
    parts = []
    for h in range(DSA_HEADS):
        acc = None
        for g, o in enumerate((o0, o1, o2)):
            term = ws[g][:, h:h + 1] * token_major(o.at[h], o_scr.at[g, h], dils[g])
            acc = term if acc is None else acc + term
        parts.append(acc.astype(BF16))
    a = jnp.concatenate(parts, axis=1)
    out_ref[...] = x_ref[...] + _dot(a, w_ref[...])


def _dsa_out(os_, ms, ds, w, x, tm=512):
    T, D = x.shape
    H = DSA_HEADS
    ng = len(DSA_PATTERNS)
    ospecs = [pl.BlockSpec((H, tm // d, d * LANES), lambda i: (0, i, 0)) for _, d in DSA_PATTERNS]
    sspecs = [pl.BlockSpec((tm // d, d * LANES), lambda i: (i, 0)) for _, d in DSA_PATTERNS]
    return pl.pallas_call(
        _dsa_out_kernel,
        grid=(T // tm,),
        in_specs=ospecs + sspecs + sspecs + [pl.BlockSpec((H * LANES, D), lambda i: (0, 0)),
                                             pl.BlockSpec((tm, D), lambda i: (i, 0))],
        out_specs=pl.BlockSpec((tm, D), lambda i: (i, 0)),
        out_shape=jax.ShapeDtypeStruct((T, D), F32),
        scratch_shapes=[pltpu.VMEM((ng, H, tm, LANES), F32), pltpu.VMEM((2 * ng, tm, LANES), F32)],
        compiler_params=_cparams("parallel"),
    )(*os_, *ms, *ds, w, x)


def _dilated_attention(x, g, w_in, w_out, B, S):
    ng = len(DSA_PATTERNS)
    gw = 3 * DSA_HEADS * LANES
    slopes = 2.0 ** (-8.0 * jnp.arange(1, ng * DSA_HEADS + 1, dtype=F32) / (ng * DSA_HEADS))
    os_, ms, ds = [], [], []
    for gi, (window, dil) in enumerate(DSA_PATTERNS):
        assert window // dil == LANES
        qkv = _dsa_proj(x, g, w_in[:, gi * gw:(gi + 1) * gw].astype(BF16), dil)
        o, m, d = _dsa_group(qkv, slopes[gi * DSA_HEADS:(gi + 1) * DSA_HEADS], dil, B, S)
        os_.append(o)
        ms.append(m)
        ds.append(d)
    return _dsa_out(os_, ms, ds, w_out.astype(BF16), x)


def _mla_qkv_kernel(lat_ref, qn_ref, kvn_ref, wq_ref, wqr_ref, wk_ref, wv_ref, cos_ref, sin_ref,
                    q_ref, k_ref, vt_ref):
    lat = lat_ref[...]
    cos = cos_ref[...]
    sin = sin_ref[...]
    hq = _rms(lat[:, :MLA_Q_LORA], qn_ref[...]).astype(BF16)
    hkv = _rms(lat[:, MLA_Q_LORA:MLA_Q_LORA + MLA_KV_LORA], kvn_ref[...]).astype(BF16)
    o1 = MLA_Q_LORA + MLA_KV_LORA
    kr = lat[:, o1:o1 + LANES] * cos + lat[:, o1 + LANES:o1 + 2 * LANES] * sin
    q = _dot(hq, wq_ref[...])
    qr = _dot(hq, wqr_ref[...])
    k = _dot(hkv, wk_ref[...])
    v = _dot(hkv, wv_ref[...])
    lane = lax.broadcasted_iota(jnp.int32, cos.shape, 1)
    scale = float(MLA_NOPE + MLA_ROPE) ** -0.5 * math.log2(math.e)
    cos_q = jnp.where(lane < MLA_NOPE, 1.0, cos) * scale
    sin_q = sin * scale
    for h in range(MLA_HEADS):
        cs = slice(h * LANES, (h + 1) * LANES)
        q_ref[h] = (q[:, cs] * cos_q + qr[:, cs] * sin_q).astype(BF16)
        k_ref[h] = (k[:, cs] + kr).astype(BF16)
        vt_ref[h] = jnp.where(lane == MLA_V, 1.0, v[:, cs]).T[:MLA_VT_ROWS].astype(BF16)


def _mla_attn_kernel(q_ref, k_ref, vt_ref, o_ref, m_scr, acc_scr, sa_scr, sb_scr, *, tq, tk):
    qi = pl.program_id(2)
    heads = range(q_ref.shape[0])
    m_scr[...] = jnp.full(m_scr.shape, -jnp.inf, F32)
    acc_scr[...] = jnp.zeros(acc_scr.shape, F32)

    def scores(j, buf):
        for h in heads:
            buf[h] = _dot_nt(k_ref[h, pl.ds(pl.multiple_of(j * tk, tk), tk), :], q_ref[h])

    def update(j, buf, masked):
        off = pl.multiple_of(j * tk, tk)
        for h in heads:
            st = buf[h]
            if masked:
                kpos = off + lax.broadcasted_iota(jnp.int32, (tk, tq), 0)
                qpos = qi * tq + lax.broadcasted_iota(jnp.int32, (tk, tq), 1)
                st = jnp.where(kpos <= qpos, st, -jnp.inf)
            m_prev = m_scr[h]
            m_new = jnp.maximum(m_prev, jnp.max(st, axis=0, keepdims=True))
            p = jnp.exp2(st - m_new).astype(BF16)
            acc_scr[h] = jnp.exp2(m_prev - m_new) * acc_scr[h] + _dot(vt_ref[h, :, pl.ds(off, tk)], p)
            m_scr[h] = m_new

    scores(0, sa_scr)

    def body(jj, c):
        j = 2 * jj
        scores(j + 1, sb_scr)
        update(j, sa_scr, False)
        scores(j + 2, sa_scr)
        update(j + 1, sb_scr, False)
        return c

    lax.fori_loop(0, qi // 2, body, 0)

    @pl.when(qi % 2 == 0)
    def _():
        update(qi, sa_scr, True)

    @pl.when(qi % 2 == 1)
    def _():
        scores(qi, sb_scr)
        update(qi - 1, sa_scr, False)
        update(qi, sb_scr, True)

    for h in heads:
        acc = acc_scr[h]
        acc = jnp.concatenate([acc, jnp.zeros((LANES - MLA_VT_ROWS, tq), F32)], axis=0).T
        o_ref[h] = (acc / acc[:, MLA_V:MLA_V + 1]).astype(o_ref.dtype)


def _rope_tables(S, half, theta, lane_off):
    inv = theta ** (-jnp.arange(half, dtype=F32) / half)
    ang = jnp.arange(S, dtype=F32)[:, None] * inv
    cos = jnp.zeros((S, LANES), F32).at[:, lane_off:lane_off + 2 * half].set(jnp.tile(jnp.cos(ang), (1, 2)))
    sin = jnp.zeros((S, LANES), F32).at[:, lane_off:lane_off + 2 * half].set(jnp.tile(jnp.sin(ang), (1, 2)))
    return cos, sin


def _rot_half_cols(w, half):
    return jnp.concatenate([-w[..., half:], w[..., :half]], axis=-1)


def _pad_heads(w, heads, width):
    K = w.shape[0]
    w = w.reshape(K, heads, width)
    return jnp.pad(w, ((0, 0), (0, 0), (0, LANES - width))).reshape(K, heads * LANES)


def _mla(x, g, w_in, q_norm, w_qb, kv_norm, w_kvb, w_out, B, S, tm=512, tq=512, tk=512, hp=4):
    T, D = x.shape
    H = MLA_HEADS
    half = MLA_ROPE // 2
    o1 = MLA_Q_LORA + MLA_KV_LORA
    w_r = w_in[:, o1:]
    tile_r = jnp.zeros((D, LANES), F32).at[:, MLA_NOPE:MLA_NOPE + MLA_ROPE].set(w_r)
    tile_rr = jnp.zeros((D, LANES), F32).at[:, MLA_NOPE:MLA_NOPE + MLA_ROPE].set(_rot_half_cols(w_r, half))
    w_lat = jnp.concatenate([w_in[:, :o1], tile_r, tile_rr], axis=1).astype(BF16)
    lat = _norm_linear(x, g, w_lat, F32, tn=w_lat.shape[1])

    wq3 = w_qb.reshape(MLA_Q_LORA, H, MLA_NOPE + MLA_ROPE)
    wq_rot = jnp.concatenate([jnp.zeros_like(wq3[..., :MLA_NOPE]), _rot_half_cols(wq3[..., MLA_NOPE:], half)], axis=-1)
    wq = _pad_heads(w_qb, H, MLA_NOPE + MLA_ROPE).astype(BF16)
    wqr = _pad_heads(wq_rot.reshape(MLA_Q_LORA, -1), H, MLA_NOPE + MLA_ROPE).astype(BF16)
    wkv3 = w_kvb.reshape(MLA_KV_LORA, H, MLA_NOPE + MLA_V)
    wk = _pad_heads(wkv3[..., :MLA_NOPE].reshape(MLA_KV_LORA, -1), H, MLA_NOPE).astype(BF16)
    wv = _pad_heads(wkv3[..., MLA_NOPE:].reshape(MLA_KV_LORA, -1), H, MLA_V).astype(BF16)
    cos, sin = _rope_tables(S, half, MLA_THETA, MLA_NOPE)

    nS = S // tm
    full = lambda shp: pl.BlockSpec(shp, lambda i: (0,) * len(shp))
    hm_spec = pl.BlockSpec((H, tm, LANES), lambda i: (0, i, 0))
    hm_shape = jax.ShapeDtypeStruct((H, T, LANES), BF16)
    q, k, v = pl.pallas_call(
        _mla_qkv_kernel,
        grid=(T // tm,),
        in_specs=[pl.BlockSpec((tm, lat.shape[1]), lambda i: (i, 0)),
                  full((1, MLA_Q_LORA)), full((1, MLA_KV_LORA)),
                  full(wq.shape), full(wqr.shape), full(wk.shape), full(wv.shape),
                  pl.BlockSpec((tm, LANES), lambda i: (i % nS, 0)),
                  pl.BlockSpec((tm, LANES), lambda i: (i % nS, 0))],
        out_specs=[hm_spec, hm_spec, pl.BlockSpec((H, MLA_VT_ROWS, tm), lambda i: (0, 0, i))],
        out_shape=[hm_shape, hm_shape, jax.ShapeDtypeStruct((H, MLA_VT_ROWS, T), BF16)],
        compiler_params=_cparams("parallel"),
    )(lat, q_norm.reshape(1, -1), kv_norm.reshape(1, -1), wq, wqr, wk, wv, cos, sin)

    q4, k4 = (t.reshape(H, B, S, LANES) for t in (q, k))
    assert tq == tk and H % hp == 0
    o = pl.pallas_call(
        functools.partial(_mla_attn_kernel, tq=tq, tk=tk),
        grid=(B, H // hp, S // tq),
        in_specs=[pl.BlockSpec((hp, None, tq, LANES), lambda b, h, qi: (h, b, qi, 0)),
                  pl.BlockSpec((hp, None, S, LANES), lambda b, h, qi: (h, b, 0, 0), pipeline_mode=pl.Buffered(1)),
                  pl.BlockSpec((hp, MLA_VT_ROWS, S), lambda b, h, qi: (h, 0, b), pipeline_mode=pl.Buffered(1))],
        out_specs=pl.BlockSpec((hp, None, tq, LANES), lambda b, h, qi: (h, b, qi, 0)),
        out_shape=jax.ShapeDtypeStruct((H, B, S, LANES), BF16),
        scratch_shapes=[pltpu.VMEM((hp, 1, tq), F32), pltpu.VMEM((hp, MLA_VT_ROWS, tq), F32),
                        pltpu.VMEM((hp, tk, tq), F32), pltpu.VMEM((hp, tk, tq), F32)],
        compiler_params=_cparams("parallel", "parallel", "arbitrary"),
    )(q4, k4, v)

    w_o = jnp.pad(w_out.reshape(H, MLA_V, D), ((0, 0), (0, LANES - MLA_V), (0, 0))).reshape(H * LANES, D)
    return _linear_res(o.reshape(H, T, LANES), w_o.astype(BF16), x, head_major=True)


def _ret_kernel(q_ref, k_ref, v_ref, g_ref, cos_ref, sin_ref, gnw_ref, y_ref, state_scr, *, C):
    @pl.when(pl.program_id(1) == 0)
    def _():
        state_scr[...] = jnp.zeros(state_scr.shape, F32)

    half = RET_QK // 2
    cos = cos_ref[...]
    sin = sin_ref[...]
    ii = lax.broadcasted_iota(jnp.int32, (C, C), 0)
    jj = lax.broadcasted_iota(jnp.int32, (C, C), 1)
    diff = (ii - jj).astype(F32)
    causal = ii >= jj
    idx = lax.broadcasted_iota(jnp.int32, (C, 1), 0).astype(F32)

    def rope(t):
        t1, t2 = t[:, :half], t[:, half:]
        return jnp.concatenate([t1 * cos - t2 * sin, t1 * sin + t2 * cos], axis=1)

    for h in range(RET_HEADS):
        lg = math.log1p(-(2.0 ** (-5.0 - h)))
        intra = jnp.exp(jnp.where(causal, lg * diff, -jnp.inf))
        q_dec = jnp.exp(lg * (idx + 1.0))
        k_dec = jnp.exp(lg * (C - 1.0 - idx))
        c_dec = math.exp(lg * C)
        q = rope(q_ref[:, h * RET_QK:(h + 1) * RET_QK].astype(F32))
        k = rope(k_ref[:, h * RET_QK:(h + 1) * RET_QK].astype(F32)) * (RET_QK ** -0.5)
        v = v_ref[:, h * RET_V:(h + 1) * RET_V]
        st = state_scr[h]
        a = _dot_nt(q.astype(BF16), k.astype(BF16)) * intra
        o = _dot(a.astype(BF16), v) + _dot((q * q_dec).astype(BF16), st.astype(BF16))
        state_scr[h] = st * c_dec + _dot((k * k_dec).T.astype(BF16), v)
        mu = jnp.mean(o, axis=-1, keepdims=True)
        oc = o - mu
        var = jnp.mean(oc * oc, axis=-1, keepdims=True)
        on = oc * lax.rsqrt(var + EPS) * gnw_ref[:, h * RET_V:(h + 1) * RET_V]
        gt = g_ref[:, h * RET_V:(h + 1) * RET_V].astype(F32)
        y_ref[:, h * RET_V:(h + 1) * RET_V] = (gt * jax.nn.sigmoid(gt) * on).astype(y_ref.dtype)


def _retention(x, g, w_in, gn_w, w_out, B, S, C=128):
    T, D = x.shape
    nqk, nv = RET_HEADS * RET_QK, RET_HEADS * RET_V
    proj = _norm_linear(x, g, w_in.astype(BF16), BF16)
    half = RET_QK // 2
    ang = jnp.arange(S, dtype=F32)[:, None] * (RET_THETA ** (-jnp.arange(half, dtype=F32) / half))
    cos, sin = jnp.cos(ang), jnp.sin(ang)
    nc = S // C
    y = pl.pallas_call(
        functools.partial(_ret_kernel, C=C),
        grid=(B, nc),
        in_specs=[pl.BlockSpec((C, nqk), lambda b, c: (b * nc + c, 0)),
                  pl.BlockSpec((C, nqk), lambda b, c: (b * nc + c, 1)),
                  pl.BlockSpec((C, nv), lambda b, c: (b * nc + c, (2 * nqk) // nv)),
                  pl.BlockSpec((C, nv), lambda b, c: (b * nc + c, (2 * nqk) // nv + 1)),
                  pl.BlockSpec((C, RET_QK // 2), lambda b, c: (c, 0)),
                  pl.BlockSpec((C, RET_QK // 2), lambda b, c: (c, 0)),
                  pl.BlockSpec((1, nv), lambda b, c: (0, 0))],
        out_specs=pl.BlockSpec((C, nv), lambda b, c: (b * nc + c, 0)),
        out_shape=jax.ShapeDtypeStruct((T, nv), BF16),
        scratch_shapes=[pltpu.VMEM((RET_HEADS, RET_QK, RET_V), F32)],
        compiler_params=_cparams("parallel", "arbitrary"),
    )(proj, proj, proj, proj, cos, sin, gn_w.reshape(1, nv))
    return _linear_res(y, w_out.astype(BF16), x)


def _split3(a):
    a1 = a.astype(BF16)
    r = a - a1.astype(F32)
    a2 = r.astype(BF16)
    a3 = (r - a2.astype(F32)).astype(BF16)
    return a1, a2, a3


SSD_COLB = 1024


def _ssd_kernel(*refs, C):
    nz, nx = SSD_D_INNER // SSD_COLB, SSD_CONV_DIM // SSD_COLB
    z_refs, x_refs = refs[:nz], refs[nz:nz + nx]
    (dt_ref, cw_ref, cb_ref, dtb_ref, alog_ref, dsk_ref, nw_ref, y_ref,
     xe_scr, act_scr, st_scr, y_scr) = refs[nz + nx:]
    G, N, P = SSD_GROUPS, SSD_STATE, SSD_P
    DI = SSD_D_INNER
    pairs_per_group = (SSD_HEADS // G) // 2
    tiles_per_colb = SSD_COLB // LANES

    HALO = 8

    @pl.when(pl.program_id(1) == 0)
    def _():
        xe_scr[0:HALO, :] = jnp.zeros((HALO, xe_scr.shape[1]), F32)
        st_scr[...] = jnp.zeros(st_scr.shape, F32)

    for kx in range(nx):
        xe_scr[HALO:HALO + C, kx * SSD_COLB:(kx + 1) * SSD_COLB] = x_refs[kx][...].astype(F32)
    CB = 512
    for cb in range(SSD_CONV_DIM // CB):
        cs = slice(cb * CB, (cb + 1) * CB)
        acc = jnp.broadcast_to(cb_ref[:, cs], (C, CB))
        for kk in range(SSD_CONV):
            off = HALO - (SSD_CONV - 1) + kk
            acc = acc + cw_ref[kk:kk + 1, cs] * xe_scr[off:off + C, cs]
        act_scr[:, cs] = (acc * jax.nn.sigmoid(acc)).astype(BF16)
    xe_scr[0:HALO, :] = xe_scr[C:C + HALO, :]

    dtr = dt_ref[...] + dtb_ref[...]
    dt = jnp.maximum(dtr, 0.0) + jnp.log1p(jnp.exp(-jnp.abs(dtr)))
    a = dt * (-jnp.exp(alog_ref[...]))
    ii = lax.broadcasted_iota(jnp.int32, (C, C), 0)
    jj = lax.broadcasted_iota(jnp.int32, (C, C), 1)
    causal = ii >= jj
    tril = jnp.where(causal, 1.0, 0.0).astype(BF16)
    a1, a2, a3 = _split3(a)
    a_cs = _dot(tril, a1) + _dot(tril, a2) + _dot(tril, a3)
    a_end = a_cs[C - 1:C, :]
    w_end = jnp.exp(a_end - a_cs) * dt
    ea = jnp.exp(a_cs)
    eend = jnp.exp(a_end)
    a_cs_t = a_cs.T
    dt_t = dt.T
    lane = lax.broadcasted_iota(jnp.int32, (C, LANES), 1)
    lo = lane < P

    def pair(t, ha):
        return jnp.where(lo[:t.shape[0]], t[:, ha:ha + 1], t[:, ha + 1:ha + 2])

    for g in range(G):
        bm = act_scr[:, DI + g * N:DI + (g + 1) * N]
        cm = act_scr[:, DI + G * N + g * N:DI + G * N + (g + 1) * N]
        cbm = _dot_nt(cm, bm)
        bm_t = bm.astype(F32).T.astype(BF16)
        ssq = jnp.zeros((C, 1), F32)
        for pj in range(pairs_per_group):
            j = g * pairs_per_group + pj
            ha = 2 * j
            cs = slice(j * LANES, (j + 1) * LANES)
            xp = act_scr[:, cs]
            ys = []
            for hh in (ha, ha + 1):
                seg = a_cs[:, hh:hh + 1] - a_cs_t[hh:hh + 1, :]
                dec = jnp.exp(jnp.where(causal, seg, -jnp.inf))
                mm = cbm * dec * dt_t[hh:hh + 1, :]
                ys.append(_dot(mm.astype(BF16), xp))
            y = jnp.where(lo, ys[0], ys[1])
            st = st_scr[:, cs]
            y = y + pair(ea, ha) * _dot(cm, st.astype(BF16))
            xpf = xp.astype(F32)
            xw = (xpf * pair(w_end, ha)).astype(BF16)
            st_scr[:, cs] = st * pair(eend, ha) + _dot(bm_t, xw)
            y = y + pair(dsk_ref[...], ha) * xpf
            jz = j % tiles_per_colb
            zt = z_refs[j // tiles_per_colb][:, jz * LANES:(jz + 1) * LANES].astype(F32)
            y = y * (zt * jax.nn.sigmoid(zt))
            ssq = ssq + jnp.sum(y * y, axis=-1, keepdims=True)
            y_scr[:, cs] = y
        gw = pairs_per_group * LANES
        gs = slice(g * gw, (g + 1) * gw)
        rs = lax.rsqrt(ssq / float(gw) + EPS)
        y_ref[:, gs] = (y_scr[:, gs] * rs * nw_ref[:, gs]).astype(y_ref.dtype)


def _pad_lanes(v):
    return jnp.pad(v.astype(F32), (0, LANES - v.shape[0])).reshape(1, LANES)


def _ssd(x, g, w_in, conv_w, conv_b, dt_bias, A_log, D_skip, norm_w, w_out, B, S, C=128):
    T, D = x.shape
    DI, CD = SSD_D_INNER, SSD_CONV_DIM
    w_main = w_in[:, :DI + CD].astype(BF16)
    w_dt = jnp.pad(w_in[:, DI + CD:], ((0, 0), (0, LANES - SSD_HEADS))).astype(BF16)
    proj = _norm_linear(x, g, w_main, BF16)
    dt = _norm_linear(x, g, w_dt, F32)
    nc = S // C
    full = lambda shp: pl.BlockSpec(shp, lambda b, c: (0,) * len(shp))
    ncolb = (DI + CD) // SSD_COLB

    def colb(kc):
        return pl.BlockSpec((C, SSD_COLB), lambda b, c: (b * nc + c, kc))

    y = pl.pallas_call(
        functools.partial(_ssd_kernel, C=C),
        grid=(B, nc),
        in_specs=[colb(kc) for kc in range(ncolb)] + [
                  pl.BlockSpec((C, LANES), lambda b, c: (b * nc + c, 0)),
                  full((SSD_CONV, CD)), full((1, CD)), full((1, LANES)), full((1, LANES)), full((1, LANES)),
                  full((1, DI))],
        out_specs=pl.BlockSpec((C, DI), lambda b, c: (b * nc + c, 0)),
        out_shape=jax.ShapeDtypeStruct((T, DI), BF16),
        scratch_shapes=[pltpu.VMEM((C + 8, CD), F32), pltpu.VMEM((C, CD), BF16),
                        pltpu.VMEM((SSD_STATE, DI), F32), pltpu.VMEM((C, DI), F32)],
        compiler_params=_cparams("parallel", "arbitrary"),
    )(*([proj] * ncolb), dt, conv_w, conv_b.reshape(1, CD), _pad_lanes(dt_bias), _pad_lanes(A_log),
      _pad_lanes(D_skip), norm_w.reshape(1, DI))
    return _linear_res(y, w_out.astype(BF16), x)


ROW_TILE = 8


def _rows_to_tiles(ref, val):
    n = val.shape[0]
    for s in range(ROW_TILE):
        ref[pl.ds(s, n, stride=ROW_TILE), :] = val[:, s * LANES:(s + 1) * LANES]


def _tiles_to_rows(ref, n):
    return jnp.concatenate([ref[pl.ds(s, n, stride=ROW_TILE), :] for s in range(ROW_TILE)], axis=1)


def _router_kernel(x_ref, g_ref, w1_ref, w2_ref, b_ref, h_ref, route_ref, cnt_ref, dense_ref, carry_scr):
    E = MOE_EXPERTS

    @pl.when(pl.program_id(0) == 0)
    def _():
        carry_scr[...] = jnp.zeros(carry_scr.shape, F32)

    h = _rms(x_ref[...], g_ref[...])
    _rows_to_tiles(h_ref, h)
    tm = h.shape[0]
    h1 = h.astype(BF16)
    h2 = (h - h1.astype(F32)).astype(BF16)
    logits = _dot(h1, w1_ref[...]) + _dot(h1, w2_ref[...]) + _dot(h2, w1_ref[...]) + b_ref[...]
    lane = lax.broadcasted_iota(jnp.int32, (tm, LANES), 1)
    lanef = lane.astype(F32)
    big = float(LANES)
    glog = jnp.where((lane >= E) & (lane < E + MOE_GROUPS), logits, -jnp.inf)
    gmax = jnp.max(glog, axis=-1, keepdims=True)
    gval = 1.0 / jnp.sum(jnp.exp(glog - gmax), axis=-1, keepdims=True)
    gidx = jnp.min(jnp.where(glog == gmax, lanef, big), axis=-1, keepdims=True) - float(E)
    in_group = (lane < E) & ((lane // MOE_EPG).astype(F32) == gidx)
    el = jnp.where(in_group, logits, -jnp.inf)
    v1 = jnp.max(el, axis=-1, keepdims=True)
    i1 = jnp.min(jnp.where(el == v1, lanef, big), axis=-1, keepdims=True)
    el2 = jnp.where(lanef == i1, -jnp.inf, el)
    v2 = jnp.max(el2, axis=-1, keepdims=True)
    i2 = jnp.min(jnp.where(el2 == v2, lanef, big), axis=-1, keepdims=True)
    e2 = jnp.exp(v2 - v1)
    wa = gval / (1.0 + e2)
    wb = gval * e2 / (1.0 + e2)
    oh1 = jnp.where(lanef == i1, 1.0, 0.0)
    oh2 = jnp.where(lanef == i2, 1.0, 0.0)
    ohs = oh1 + oh2
    ii = lax.broadcasted_iota(jnp.int32, (tm, tm), 0)
    jj = lax.broadcasted_iota(jnp.int32, (tm, tm), 1)
    stril = jnp.where(ii > jj, 1.0, 0.0).astype(BF16)
    before = _dot(stril, ohs.astype(BF16)) + carry_scr[0:1, :]
    r1 = jnp.sum(before * oh1, axis=-1, keepdims=True)
    r2 = jnp.sum(before * oh2, axis=-1, keepdims=True)
    carry = carry_scr[0:1, :] + jnp.sum(ohs, axis=0, keepdims=True)
    carry_scr[...] = jnp.broadcast_to(carry, carry_scr.shape)
    cnt_ref[...] = jnp.broadcast_to(carry, cnt_ref.shape)
    route = jnp.zeros((tm, LANES), F32)
    for ln, val in enumerate((i1, i2, wa, wb, r1, r2)):
        route = jnp.where(lane == ln, val, route)
    route_ref[...] = route
    eye = (lax.broadcasted_iota(jnp.int32, (LANES, LANES), 0) == lax.broadcasted_iota(jnp.int32, (LANES, LANES), 1))
    rows = []
    for val in (i1, i2, r1, r2):
        vb = jnp.broadcast_to(val, (tm, LANES))
        for r in range(tm // LANES):
            rows.append(jnp.sum(jnp.where(eye, vb[r * LANES:(r + 1) * LANES], 0.0), axis=0, keepdims=True))
    dense_ref[...] = jnp.concatenate(rows, axis=0)


def _dispatch_kernel(pos_ref, lo_ref, hi_ref, h_ref, xg_hbm, zero_scr, sem, zsem):
    ntok = pos_ref.shape[0] // MOE_TOPK
    base = pl.program_id(0) * TOK_TILE

    @pl.when(pl.program_id(0) == 0)
    def _():
        zero_scr[...] = jnp.zeros(zero_scr.shape, F32)
        blk = zero_scr.shape[0]

        def zero_copy(r, n):
            return pltpu.make_async_copy(zero_scr.at[pl.ds(0, n)], xg_hbm.at[pl.ds(pl.multiple_of(r * n, n), n)], zsem)

        def loop(lo, hi, n, wait):
            def body(r, c):
                cp = zero_copy(r, n)
                cp.wait() if wait else cp.start()
                return c
            lax.fori_loop(lo, hi, body, 0)

        tail_lo = hi_ref[MOE_EXPERTS - 1] * ROW_TILE // blk
        tail_hi = xg_hbm.shape[0] // blk
        for wait in (False, True):
            for e in range(MOE_EXPERTS):
                loop(lo_ref[e], hi_ref[e], ROW_TILE, wait)
            loop(tail_lo, tail_hi, blk, wait)

    def copy(t, k):
        row = pl.multiple_of(pos_ref[k * ntok + base + t] * ROW_TILE, ROW_TILE)
        return pltpu.make_async_copy(h_ref.at[pl.ds(t * ROW_TILE, ROW_TILE)], xg_hbm.at[pl.ds(row, ROW_TILE)], sem)

    for t in range(TOK_TILE):
        for k in range(MOE_TOPK):
            copy(t, k).start(priority=k)
    same_size = pltpu.make_async_copy(h_ref.at[pl.ds(0, ROW_TILE)], xg_hbm.at[pl.ds(0, ROW_TILE)], sem)
    for _ in range(TOK_TILE * MOE_TOPK):
        same_size.wait()


def _expert_kernel(be_ref, nu_ref, x_ref, wg_ref, wu_ref, wd_ref, o_ref, wg_scr, wu_scr, wd_scr):
    i = pl.program_id(0)
    used = i < nu_ref[0]
    n = x_ref.shape[0] // ROW_TILE

    @pl.when(used & ((i == 0) | (be_ref[i] != be_ref[jnp.maximum(i - 1, 0)])))
    def _():
        wg_scr[...] = wg_ref[...].astype(BF16)
        wu_scr[...] = wu_ref[...].astype(BF16)
        wd_scr[...] = wd_ref[...].astype(BF16)

    @pl.when(used)
    def _():
        xb = _tiles_to_rows(x_ref, n).astype(BF16)
        hg = _dot(xb, wg_scr[...])
        hu = _dot(xb, wu_scr[...])
        _rows_to_tiles(o_ref, _dot((hg * jax.nn.sigmoid(hg) * hu).astype(BF16), wd_scr[...]))

    @pl.when(jnp.logical_not(used))
    def _():
        o_ref[...] = jnp.zeros(o_ref.shape, F32)


def _ple_kernel(pos_ref, x_ref, rt_ref, p_ref, g_ref, gf_ref, wg_ref, wp_ref, yb_hbm, o_ref,
                buf_a, buf_b, sems, *, final):
    tp = x_ref.shape[0]
    i = pl.program_id(0)
    last = pl.num_programs(0) - 1
    slot = i % 2
    bufs = (buf_a, buf_b)

    def copy(tile, sl, t, k):
        row = pl.multiple_of(pos_ref[k * (pos_ref.shape[0] // MOE_TOPK) + tile * tp + t] * ROW_TILE, ROW_TILE)
        dst = bufs[k].at[sl, pl.ds(pl.multiple_of(t * ROW_TILE, ROW_TILE), ROW_TILE)]
        return pltpu.make_async_copy(yb_hbm.at[pl.ds(row, ROW_TILE)], dst, sems.at[sl])

    def wait_slot(sl):
        for k in range(MOE_TOPK):
            same_size = pltpu.make_async_copy(yb_hbm.at[pl.ds(0, ROW_TILE)], bufs[k].at[sl, pl.ds(0, ROW_TILE)],
                                              sems.at[sl])
            for _ in range(tp):
                same_size.wait()

    @pl.when(i == 0)
    def _():
        def first(t, c):
            for k in range(MOE_TOPK):
                copy(0, 0, t, k).start(priority=k)
            return c
        lax.fori_loop(0, tp, first, 0, unroll=8)

    wait_slot(slot)
    nxt = jnp.minimum(i + 1, last)
    for t in range(tp):
        for k in range(MOE_TOPK):
            copy(nxt, 1 - slot, t, k).start(priority=k)

    ple = _dot(p_ref[...].astype(BF16), wp_ref[...])
    rt = rt_ref[...]
    ya = _tiles_to_rows(buf_a.at[slot], tp)
    yb = _tiles_to_rows(buf_b.at[slot], tp)
    x1 = x_ref[...] + rt[:, 2:3] * ya + rt[:, 3:4] * yb
    gate = jax.nn.sigmoid(_dot(_rms(x1, g_ref[...]).astype(BF16), wg_ref[...]))
    out = x1 + ple * gate
    if final:
        out = _rms(out, gf_ref[...])
    o_ref[...] = out

    @pl.when(i == last)
    def _():
        wait_slot(1 - slot)


def _moe_ple(x, layer, p_all, norm_moe, w_group, b_group, w_expert, b_expert, w_gate, w_up, w_down,
             norm_ple, ple_w, ple_w_gate, norm_final, final, tm=512):
    T, D = x.shape
    E, BLK = MOE_EXPERTS, MOE_BLOCK
    A = T * MOE_TOPK
    nb = A // BLK + E
    P = nb * BLK

    w_r = jnp.zeros((D, LANES), F32).at[:, :E].set(w_expert).at[:, E:E + MOE_GROUPS].set(w_group)
    b_r = jnp.zeros((1, LANES), F32).at[0, :E].set(b_expert).at[0, E:E + MOE_GROUPS].set(b_group)
    w_r1 = w_r.astype(BF16)
    w_r2 = (w_r - w_r1.astype(F32)).astype(BF16)
    full = lambda shp: pl.BlockSpec(shp, lambda i: (0,) * len(shp))
    nsub = tm // LANES
    h, route, cnt, dense = pl.pallas_call(
        _router_kernel,
        grid=(T // tm,),
        in_specs=[pl.BlockSpec((tm, D), lambda i: (i, 0)), full((1, D)), full((D, LANES)), full((D, LANES)),
                  full((1, LANES))],
        out_specs=[pl.BlockSpec((tm * ROW_TILE, LANES), lambda i: (i, 0)),
                   pl.BlockSpec((tm, LANES), lambda i: (i, 0)), full((8, LANES)),
                   pl.BlockSpec((2 * MOE_TOPK * nsub, LANES), lambda i: (i, 0))],
        out_shape=[jax.ShapeDtypeStruct((T * ROW_TILE, LANES), F32), jax.ShapeDtypeStruct((T, LANES), F32),
                   jax.ShapeDtypeStruct((8, LANES), F32),
                   jax.ShapeDtypeStruct((T // tm * 2 * MOE_TOPK * nsub, LANES), F32)],
        scratch_shapes=[pltpu.VMEM((8, LANES), F32)],
        compiler_params=_cparams("arbitrary"),
    )(x, norm_moe.reshape(1, D), w_r1, w_r2, b_r)

    counts = cnt[0, :E].astype(jnp.int32)
    padded = (counts + BLK - 1) // BLK * BLK
    pad_end = jnp.cumsum(padded)
    pad_start = pad_end - padded
    dense = dense.reshape(T // tm, 2 * MOE_TOPK, nsub * LANES).astype(jnp.int32)
    pos = jnp.concatenate([(pad_start[dense[:, k]] + dense[:, MOE_TOPK + k]).reshape(T)
                           for k in range(MOE_TOPK)])
    blk_start = jnp.arange(nb, dtype=jnp.int32) * BLK
    blk_expert = jnp.minimum(jnp.sum((pad_end[None, :] <= blk_start[:, None]).astype(jnp.int32), axis=1), E - 1)
    n_used = (pad_end[-1:] // BLK).astype(jnp.int32)

    assert D == ROW_TILE * LANES
    xg = pl.pallas_call(
        _dispatch_kernel,
        grid_spec=pltpu.PrefetchScalarGridSpec(
            num_scalar_prefetch=3, grid=(T // TOK_TILE,),
            in_specs=[pl.BlockSpec((TOK_TILE * ROW_TILE, LANES), lambda i, ps, lo, hi: (i, 0))],
            out_specs=pl.BlockSpec(memory_space=pl.ANY),
            scratch_shapes=[pltpu.VMEM((BLK * ROW_TILE, LANES), F32), pltpu.SemaphoreType.DMA,
                            pltpu.SemaphoreType.DMA]),
        out_shape=jax.ShapeDtypeStruct((P * ROW_TILE, LANES), F32),
        compiler_params=_cparams("arbitrary"),
    )(pos, pad_start + counts, pad_end, h)

    def wmap(i, be, nu):
        return (layer, be[jnp.minimum(i, nu[0] - 1)], 0, 0)

    def xmap(i, be, nu):
        return (jnp.minimum(i, nu[0] - 1), 0)

    yb = pl.pallas_call(
        _expert_kernel,
        grid_spec=pltpu.PrefetchScalarGridSpec(
            num_scalar_prefetch=2, grid=(nb,),
            in_specs=[pl.BlockSpec((BLK * ROW_TILE, LANES), xmap),
                      pl.BlockSpec((None, None, D, MOE_FF), wmap), pl.BlockSpec((None, None, D, MOE_FF), wmap),
                      pl.BlockSpec((None, None, MOE_FF, D), wmap)],
            out_specs=pl.BlockSpec((BLK * ROW_TILE, LANES), lambda i, be, nu: (i, 0)),
            scratch_shapes=[pltpu.VMEM((D, MOE_FF), BF16), pltpu.VMEM((D, MOE_FF), BF16),
                            pltpu.VMEM((MOE_FF, D), BF16)]),
        out_shape=jax.ShapeDtypeStruct((P * ROW_TILE, LANES), F32),
        compiler_params=_cparams("arbitrary"),
    )(blk_expert, n_used, xg, w_gate, w_up, w_down)

    tp = TOK_TILE
    PD = p_all.shape[-1]
    return pl.pallas_call(
        functools.partial(_ple_kernel, final=final),
        grid_spec=pltpu.PrefetchScalarGridSpec(
            num_scalar_prefetch=1, grid=(T // tp,),
            in_specs=[pl.BlockSpec((tp, D), lambda i, ps: (i, 0)),
                      pl.BlockSpec((tp, LANES), lambda i, ps: (i, 0)),
                      pl.BlockSpec((None, tp, PD), lambda i, ps: (layer, i, 0)),
                      pl.BlockSpec((1, D), lambda i, ps: (0, 0)), pl.BlockSpec((1, D), lambda i, ps: (0, 0)),
                      pl.BlockSpec((D, D), lambda i, ps: (0, 0)), pl.BlockSpec((PD, D), lambda i, ps: (0, 0)),
                      pl.BlockSpec(memory_space=pl.ANY)],
            out_specs=pl.BlockSpec((tp, D), lambda i, ps: (i, 0)),
            scratch_shapes=[pltpu.VMEM((2, tp * ROW_TILE, LANES), F32), pltpu.VMEM((2, tp * ROW_TILE, LANES), F32),
                            pltpu.SemaphoreType.DMA((2,))]),
        out_shape=jax.ShapeDtypeStruct((T, D), F32),
        compiler_params=_cparams("arbitrary"),
    )(pos, x, route, p_all, norm_ple.reshape(1, D), norm_final.reshape(1, D),
      ple_w_gate.astype(BF16), ple_w.astype(BF16), yb)


def kernel(x, p, norm_mix, norm_moe, norm_ple, norm_final, dsa_w_in, dsa_w_out, mla_w_in, mla_q_norm, mla_w_qb,
           mla_kv_norm, mla_w_kvb, mla_w_out, ret_w_in, ret_gn_w, ret_w_out, ssd_w_in, ssd_conv_w, ssd_conv_b,
           ssd_dt_bias, ssd_A_log, ssd_D, ssd_norm_w, ssd_w_out, moe_w_group, moe_b_group, moe_w_expert,
           moe_b_expert, moe_w_gate, moe_w_up, moe_w_down, ple_w, ple_w_gate):
    B, S, D = x.shape
    depth = p.shape[0]
    xt = x.reshape(B * S, D)
    for i in range(depth):
        kind = i % 4
        if kind == 0:
            xt = _dilated_attention(xt, norm_mix[i], dsa_w_in, dsa_w_out, B, S)
        elif kind == 1:
            xt = _mla(xt, norm_mix[i], mla_w_in, mla_q_norm, mla_w_qb, mla_kv_norm, mla_w_kvb, mla_w_out, B, S)
        elif kind == 2:
            xt = _retention(xt, norm_mix[i], ret_w_in, ret_gn_w, ret_w_out, B, S)
        else:
            xt = _ssd(xt, norm_mix[i], ssd_w_in, ssd_conv_w, ssd_conv_b, ssd_dt_bias, ssd_A_log, ssd_D,
                      ssd_norm_w, ssd_w_out, B, S)
        xt = _moe_ple(xt, i, p.reshape(depth, B * S, -1), norm_moe[i], moe_w_group[i], moe_b_group[i],
                      moe_w_expert[i], moe_b_expert[i], moe_w_gate, moe_w_up, moe_w_down, norm_ple[i], ple_w[i],
                      ple_w_gate[i], norm_final, final=(i == depth - 1))
    return xt.reshape(B, S, D)
```

```python
import functools
import math

import jax
import jax.numpy as jnp
from jax import lax
from jax.experimental import pallas as pl
from jax.experimental.pallas import tpu as pltpu

F32 = jnp.float32
BF16 = jnp.bfloat16
EPS = 1e-6
LANES = 128
VMEM_LIMIT = 48 * 1024 * 1024

DSA_PATTERNS = ((128, 1), (512, 4), (2048, 16))
DSA_HEADS = 8
DSA_SPAN = 2048
MLA_HEADS = 16
MLA_Q_LORA, MLA_KV_LORA, MLA_NOPE, MLA_ROPE, MLA_V = 256, 128, 64, 32, 64
MLA_THETA = 10000.0
MLA_VT_ROWS = 80
RET_HEADS, RET_QK, RET_V = 4, 256, 512
RET_THETA = 10000.0
SSD_D_INNER, SSD_HEADS, SSD_GROUPS, SSD_STATE, SSD_CONV, SSD_P = 2048, 32, 4, 128, 4, 64
SSD_CONV_DIM = SSD_D_INNER + 2 * SSD_GROUPS * SSD_STATE
MOE_GROUPS, MOE_EPG, MOE_EXPERTS, MOE_TOPK, MOE_FF = 4, 8, 32, 2, 512
MOE_BLOCK = 256
TOK_TILE = 256


def _cparams(*sem):
    return pltpu.CompilerParams(dimension_semantics=sem, vmem_limit_bytes=VMEM_LIMIT)


def _rms(x, g):
    return x * lax.rsqrt(jnp.mean(x * x, axis=-1, keepdims=True) + EPS) * g


def _dot(a, b):
    return jnp.dot(a, b, preferred_element_type=F32)


def _dot_nt(a, b):
    return lax.dot_general(a, b, (((1,), (1,)), ((), ())), preferred_element_type=F32)


def _norm_linear_kernel(x_ref, g_ref, w_ref, o_ref, h_scr, *, head_major):
    @pl.when(pl.program_id(1) == 0)
    def _():
        h_scr[...] = _rms(x_ref[...], g_ref[...]).astype(BF16)

    r = _dot(h_scr[...], w_ref[...])
    if head_major:
        for c in range(r.shape[1] // LANES):
            o_ref[c] = r[:, c * LANES:(c + 1) * LANES].astype(o_ref.dtype)
    else:
        o_ref[...] = r.astype(o_ref.dtype)


def _norm_linear(x, g, w, out_dtype, *, head_major=False, tm=1024, tn=1024):
    T, D = x.shape
    N = w.shape[1]
    tn = min(tn, N)
    assert T % tm == 0 and N % tn == 0 and tn % LANES == 0
    if head_major:
        out_shape = jax.ShapeDtypeStruct((N // LANES, T, LANES), out_dtype)
        out_spec = pl.BlockSpec((tn // LANES, tm, LANES), lambda i, j: (j, i, 0))
    else:
        out_shape = jax.ShapeDtypeStruct((T, N), out_dtype)
        out_spec = pl.BlockSpec((tm, tn), lambda i, j: (i, j))
    return pl.pallas_call(
        functools.partial(_norm_linear_kernel, head_major=head_major),
        grid=(T // tm, N // tn),
        in_specs=[pl.BlockSpec((tm, D), lambda i, j: (i, 0)),
                  pl.BlockSpec((1, D), lambda i, j: (0, 0)),
                  pl.BlockSpec((D, tn), lambda i, j: (0, j))],
        out_specs=out_spec,
        out_shape=out_shape,
        scratch_shapes=[pltpu.VMEM((tm, D), BF16)],
        compiler_params=_cparams("parallel", "arbitrary"),
    )(x, g.reshape(1, D), w)


def _linear_res_kernel(a_ref, w_ref, r_ref, o_ref, *, head_major):
    if head_major:
        a = jnp.concatenate([a_ref[c] for c in range(a_ref.shape[0])], axis=1)
    else:
        a = a_ref[...]
    o_ref[...] = r_ref[...] + _dot(a, w_ref[...])


def _linear_res(a, w, res, *, head_major=False, tm=512):
    T, D = res.shape
    K = w.shape[0]
    if head_major:
        a_spec = pl.BlockSpec((K // LANES, tm, LANES), lambda i: (0, i, 0))
    else:
        a_spec = pl.BlockSpec((tm, K), lambda i: (i, 0))
    return pl.pallas_call(
        functools.partial(_linear_res_kernel, head_major=head_major),
        grid=(T // tm,),
        in_specs=[a_spec,
                  pl.BlockSpec((K, D), lambda i: (0, 0)),
                  pl.BlockSpec((tm, D), lambda i: (i, 0))],
        out_specs=pl.BlockSpec((tm, D), lambda i: (i, 0)),
        out_shape=jax.ShapeDtypeStruct((T, D), F32),
        compiler_params=_cparams("parallel"),
    )(a, w, res)


def _dsa_group_kernel(sl_ref, q_ref, k_ref, kh_ref, v_ref, vh_ref, o_ref, sm_ref, sd_ref, *, dil, rows):
    W = LANES
    i = pl.program_id(1)
    h = pl.program_id(2)
    scale = float(W) ** -0.5
    slope = sl_ref[h] * float(dil)
    qi = lax.broadcasted_iota(jnp.int32, (W, 2 * W), 0)
    kj = lax.broadcasted_iota(jnp.int32, (W, 2 * W), 1)
    steps = W + qi - kj
    band = (steps >= 0) & (steps <= W)
    first_lim = jnp.where(i == 0, W, 0)
    band_first = band & (kj >= first_lim)
    bias = (-slope) * steps.astype(F32)
    lane = lax.broadcasted_iota(jnp.int32, (W, LANES), 1)

    @pl.when(h == 0)
    def _():
        sm_ref[...] = jnp.zeros(sm_ref.shape, F32)
        sd_ref[...] = jnp.ones(sd_ref.shape, F32)

    for r in range(dil):
        cs = slice(r * LANES, (r + 1) * LANES)
        for n in range(rows // W):
            rs = slice(n * W, (n + 1) * W)
            q = q_ref[rs, cs]
            if n == 0:
                kk = jnp.concatenate([kh_ref[:, cs], k_ref[rs, cs]], axis=0)
                vv = jnp.concatenate([vh_ref[:, cs], v_ref[rs, cs]], axis=0)
                valid = band_first
            else:
                kk = k_ref[(n - 1) * W:(n + 1) * W, cs]
                vv = v_ref[(n - 1) * W:(n + 1) * W, cs]
                valid = band
            s = _dot_nt(q, kk) * scale + bias
            s = jnp.where(valid, s, -jnp.inf)
            m = jnp.max(s, axis=-1, keepdims=True)
            e = jnp.exp(s - m)
            den = jnp.sum(e, axis=-1, keepdims=True)
            o = _dot(e.astype(BF16), vv) / den
            o_ref[rs, cs] = o.astype(o_ref.dtype)
            sm_ref[rs, cs] = jnp.where(lane == h, m, sm_ref[rs, cs])
            sd_ref[rs, cs] = jnp.where(lane == h, den, sd_ref[rs, cs])


def _dsa_proj_kernel(x_ref, g_ref, w_ref, o_ref, h_scr, r_scr, *, dil):
    @pl.when(pl.program_id(1) == 0)
    def _():
        h_scr[...] = _rms(x_ref[...], g_ref[...]).astype(BF16)

    res = _dot(h_scr[...], w_ref[...])
    rows = r_scr.shape[1] // dil
    for c in range(o_ref.shape[0]):
        if dil == 1:
            o_ref[c] = res[:, c * LANES:(c + 1) * LANES].astype(o_ref.dtype)
            continue
        r_scr[c] = res[:, c * LANES:(c + 1) * LANES]
        for r in range(dil):
            o_ref[c, :, r * LANES:(r + 1) * LANES] = r_scr[c, pl.ds(r, rows, stride=dil), :].astype(o_ref.dtype)


def _dsa_proj(x, g, w, dil, tm=1024):
    T, D = x.shape
    N = w.shape[1]
    tn = DSA_HEADS * LANES
    return pl.pallas_call(
        functools.partial(_dsa_proj_kernel, dil=dil),
        grid=(T // tm, N // tn),
        in_specs=[pl.BlockSpec((tm, D), lambda i, j: (i, 0)),
                  pl.BlockSpec((1, D), lambda i, j: (0, 0)),
                  pl.BlockSpec((D, tn), lambda i, j: (0, j))],
        out_specs=pl.BlockSpec((DSA_HEADS, tm // dil, dil * LANES), lambda i, j: (j, i, 0)),
        out_shape=jax.ShapeDtypeStruct((N // LANES, T // dil, dil * LANES), BF16),
        scratch_shapes=[pltpu.VMEM((tm, D), BF16), pltpu.VMEM((DSA_HEADS, tm, LANES), F32)],
        compiler_params=_cparams("parallel", "arbitrary"),
    )(x, g.reshape(1, D), w)


def _dsa_group(qkv, slopes, dil, B, S):
    H = DSA_HEADS
    L = S // dil
    rows = DSA_SPAN // dil
    width = dil * LANES
    qv = qkv.reshape(qkv.shape[0], B, L, width)
    hb = rows // LANES
    base = 0

    def qmap(o):
        return lambda b, i, h, sl: (base + o * H + h, b, i, 0)

    def hmap(o):
        return lambda b, i, h, sl: (base + o * H + h, b, jnp.maximum(i * hb - 1, 0), 0)

    blk = (None, None, rows, width)
    hblk = (None, None, LANES, width)
    o, sm, sd = pl.pallas_call(
        functools.partial(_dsa_group_kernel, dil=dil, rows=rows),
        grid_spec=pltpu.PrefetchScalarGridSpec(
            num_scalar_prefetch=1,
            grid=(B, L // rows, H),
            in_specs=[pl.BlockSpec(blk, qmap(0)),
                      pl.BlockSpec(blk, qmap(1)), pl.BlockSpec(hblk, hmap(1)),
                      pl.BlockSpec(blk, qmap(2)), pl.BlockSpec(hblk, hmap(2))],
            out_specs=[pl.BlockSpec(blk, lambda b, i, h, sl: (h, b, i, 0)),
                       pl.BlockSpec((None, rows, width), lambda b, i, h, sl: (b, i, 0)),
                       pl.BlockSpec((None, rows, width), lambda b, i, h, sl: (b, i, 0))]),
        out_shape=[jax.ShapeDtypeStruct((H, B, L, width), BF16),
                   jax.ShapeDtypeStruct((B, L, width), F32),
                   jax.ShapeDtypeStruct((B, L, width), F32)],
        compiler_params=_cparams("parallel", "parallel", "arbitrary"),
    )(slopes, qv, qv, qv, qv, qv)
    return o.reshape(H, B * L, width), sm.reshape(B * L, width), sd.reshape(B * L, width)


def _dsa_out_kernel(o0, o1, o2, m0, m1, m2, d0, d1, d2, w_ref, x_ref, out_ref, o_scr, s_scr):
    tm = x_ref.shape[0]
    dils = [d for _, d in DSA_PATTERNS]

    def token_major(ref, scr, dil):
        if dil == 1:
            return ref[...].astype(F32)
        for r in range(dil):
            scr[pl.ds(r, tm // dil, stride=dil), :] = ref[:, r * LANES:(r + 1) * LANES].astype(F32)
        return scr[...]

    ms = [token_major(m, s_scr.at[2 * g], dils[g]) for g, m in enumerate((m0, m1, m2))]
    dn = [token_major(d, s_scr.at[2 * g + 1], dils[g]) for g, d in enumerate((d0, d1, d2))]
    mx = jnp.maximum(jnp.maximum(ms[0], ms[1]), ms[2])
    ws = [d * jnp.exp(m - mx) for d, m in zip(dn, ms)]
    tot = ws[0] + ws[1] + ws[2]
    ws = [w / tot for w in ws]
    parts = []
    for h in range(DSA_HEADS):
        acc = None
        for g, o in enumerate((o0, o1, o2)):
            term = ws[g][:, h:h + 1] * token_major(o.at[h], o_scr.at[g, h], dils[g])
            acc = term if acc is None else acc + term
        parts.append(acc.astype(BF16))
    a = jnp.concatenate(parts, axis=1)
    out_ref[...] = x_ref[...] + _dot(a, w_ref[...])


def _dsa_out(os_, ms, ds, w, x, tm=512):
    T, D = x.shape
    H = DSA_HEADS
    ng = len(DSA_PATTERNS)
    ospecs = [pl.BlockSpec((H, tm // d, d * LANES), lambda i: (0, i, 0)) for _, d in DSA_PATTERNS]
    sspecs = [pl.BlockSpec((tm // d, d * LANES), lambda i: (i, 0)) for _, d in DSA_PATTERNS]
    return pl.pallas_call(
        _dsa_out_kernel,
        grid=(T // tm,),
        in_specs=ospecs + sspecs + sspecs + [pl.BlockSpec((H * LANES, D), lambda i: (0, 0)),
                                             pl.BlockSpec((tm, D), lambda i: (i, 0))],
        out_specs=pl.BlockSpec((tm, D), lambda i: (i, 0)),
        out_shape=jax.ShapeDtypeStruct((T, D), F32),
        scratch_shapes=[pltpu.VMEM((ng, H, tm, LANES), F32), pltpu.VMEM((2 * ng, tm, LANES), F32)],
        compiler_params=_cparams("parallel"),
    )(*os_, *ms, *ds, w, x)


def _dilated_attention(x, g, w_in, w_out, B, S):
    ng = len(DSA_PATTERNS)
    gw = 3 * DSA_HEADS * LANES
    slopes = 2.0 ** (-8.0 * jnp.arange(1, ng * DSA_HEADS + 1, dtype=F32) / (ng * DSA_HEADS))
    os_, ms, ds = [], [], []
    for gi, (window, dil) in enumerate(DSA_PATTERNS):
        assert window // dil == LANES
        qkv = _dsa_proj(x, g, w_in[:, gi * gw:(gi + 1) * gw].astype(BF16), dil)
        o, m, d = _dsa_group(qkv, slopes[gi * DSA_HEADS:(gi + 1) * DSA_HEADS], dil, B, S)
        os_.append(o)
        ms.append(m)
        ds.append(d)
    return _dsa_out(os_, ms, ds, w_out.astype(BF16), x)


def _mla_qkv_kernel(lat_ref, qn_ref, kvn_ref, wq_ref, wqr_ref, wk_ref, wv_ref, cos_ref, sin_ref,
                    q_ref, k_ref, vt_ref):
    lat = lat_ref[...]
    cos = cos_ref[...]
    sin = sin_ref[...]
    hq = _rms(lat[:, :MLA_Q_LORA], qn_ref[...]).astype(BF16)
    hkv = _rms(lat[:, MLA_Q_LORA:MLA_Q_LORA + MLA_KV_LORA], kvn_ref[...]).astype(BF16)
    o1 = MLA_Q_LORA + MLA_KV_LORA
    kr = lat[:, o1:o1 + LANES] * cos + lat[:, o1 + LANES:o1 + 2 * LANES] * sin
    q = _dot(hq, wq_ref[...])
    qr = _dot(hq, wqr_ref[...])
    k = _dot(hkv, wk_ref[...])
    v = _dot(hkv, wv_ref[...])
    lane = lax.broadcasted_iota(jnp.int32, cos.shape, 1)
    scale = float(MLA_NOPE + MLA_ROPE) ** -0.5 * math.log2(math.e)
    cos_q = jnp.where(lane < MLA_NOPE, 1.0, cos) * scale
    sin_q = sin * scale
    for h in range(MLA_HEADS):
        cs = slice(h * LANES, (h + 1) * LANES)
        q_ref[h] = (q[:, cs] * cos_q + qr[:, cs] * sin_q).astype(BF16)
        k_ref[h] = (k[:, cs] + kr).astype(BF16)
        vt_ref[h] = jnp.where(lane == MLA_V, 1.0, v[:, cs]).T[:MLA_VT_ROWS].astype(BF16)


def _mla_attn_kernel(q_ref, k_ref, vt_ref, o_ref, m_scr, acc_scr, sa_scr, sb_scr, *, tq, tk):
    qi = pl.program_id(2)
    heads = range(q_ref.shape[0])
    m_scr[...] = jnp.full(m_scr.shape, -jnp.inf, F32)
    acc_scr[...] = jnp.zeros(acc_scr.shape, F32)

    def scores(j, buf):
        for h in heads:
            buf[h] = _dot_nt(k_ref[h, pl.ds(pl.multiple_of(j * tk, tk), tk), :], q_ref[h])

    def update(j, buf, masked):
        off = pl.multiple_of(j * tk, tk)
        for h in heads:
            st = buf[h]
            if masked:
                kpos = off + lax.broadcasted_iota(jnp.int32, (tk, tq), 0)
                qpos = qi * tq + lax.broadcasted_iota(jnp.int32, (tk, tq), 1)
                st = jnp.where(kpos <= qpos, st, -jnp.inf)
            m_prev = m_scr[h]
            m_new = jnp.maximum(m_prev, jnp.max(st, axis=0, keepdims=True))
            p = jnp.exp2(st - m_new).astype(BF16)
            acc_scr[h] = jnp.exp2(m_prev - m_new) * acc_scr[h] + _dot(vt_ref[h, :, pl.ds(off, tk)], p)
            m_scr[h] = m_new

    scores(0, sa_scr)

    def body(jj, c):
        j = 2 * jj
        scores(j + 1, sb_scr)
        update(j, sa_scr, False)
        scores(j + 2, sa_scr)
        update(j + 1, sb_scr, False)
        return c

    lax.fori_loop(0, qi // 2, body, 0)

    @pl.when(qi % 2 == 0)
    def _():
        update(qi, sa_scr, True)

    @pl.when(qi % 2 == 1)
    def _():
        scores(qi, sb_scr)
        update(qi - 1, sa_scr, False)
        update(qi, sb_scr, True)

    for h in heads:
        acc = acc_scr[h]
        acc = jnp.concatenate([acc, jnp.zeros((LANES - MLA_VT_ROWS, tq), F32)], axis=0).T
        o_ref[h] = (acc / acc[:, MLA_V:MLA_V + 1]).astype(o_ref.dtype)


def _rope_tables(S, half, theta, lane_off):
    inv = theta ** (-jnp.arange(half, dtype=F32) / half)
    ang = jnp.arange(S, dtype=F32)[:, None] * inv
    cos = jnp.zeros((S, LANES), F32).at[:, lane_off:lane_off + 2 * half].set(jnp.tile(jnp.cos(ang), (1, 2)))
    sin = jnp.zeros((S, LANES), F32).at[:, lane_off:lane_off + 2 * half].set(jnp.tile(jnp.sin(ang), (1, 2)))
    return cos, sin


def _rot_half_cols(w, half):
    return jnp.concatenate([-w[..., half:], w[..., :half]], axis=-1)


def _pad_heads(w, heads, width):
    K = w.shape[0]
    w = w.reshape(K, heads, width)
    return jnp.pad(w, ((0, 0), (0, 0), (0, LANES - width))).reshape(K, heads * LANES)


def _mla(x, g, w_in, q_norm, w_qb, kv_norm, w_kvb, w_out, B, S, tm=512, tq=512, tk=512, hp=4):
    T, D = x.shape
    H = MLA_HEADS
    half = MLA_ROPE // 2
    o1 = MLA_Q_LORA + MLA_KV_LORA
    w_r = w_in[:, o1:]
    tile_r = jnp.zeros((D, LANES), F32).at[:, MLA_NOPE:MLA_NOPE + MLA_ROPE].set(w_r)
    tile_rr = jnp.zeros((D, LANES), F32).at[:, MLA_NOPE:MLA_NOPE + MLA_ROPE].set(_rot_half_cols(w_r, half))
    w_lat = jnp.concatenate([w_in[:, :o1], tile_r, tile_rr], axis=1).astype(BF16)
    lat = _norm_linear(x, g, w_lat, F32, tn=w_lat.shape[1])

    wq3 = w_qb.reshape(MLA_Q_LORA, H, MLA_NOPE + MLA_ROPE)
    wq_rot = jnp.concatenate([jnp.zeros_like(wq3[..., :MLA_NOPE]), _rot_half_cols(wq3[..., MLA_NOPE:], half)], axis=-1)
    wq = _pad_heads(w_qb, H, MLA_NOPE + MLA_ROPE).astype(BF16)
    wqr = _pad_heads(wq_rot.reshape(MLA_Q_LORA, -1), H, MLA_NOPE + MLA_ROPE).astype(BF16)
    wkv3 = w_kvb.reshape(MLA_KV_LORA, H, MLA_NOPE + MLA_V)
    wk = _pad_heads(wkv3[..., :MLA_NOPE].reshape(MLA_KV_LORA, -1), H, MLA_NOPE).astype(BF16)
    wv = _pad_heads(wkv3[..., MLA_NOPE:].reshape(MLA_KV_LORA, -1), H, MLA_V).astype(BF16)
    cos, sin = _rope_tables(S, half, MLA_THETA, MLA_NOPE)

    nS = S // tm
    full = lambda shp: pl.BlockSpec(shp, lambda i: (0,) * len(shp))
    hm_spec = pl.BlockSpec((H, tm, LANES), lambda i: (0, i, 0))
    hm_shape = jax.ShapeDtypeStruct((H, T, LANES), BF16)
    q, k, v = pl.pallas_call(
        _mla_qkv_kernel,
        grid=(T // tm,),
        in_specs=[pl.BlockSpec((tm, lat.shape[1]), lambda i: (i, 0)),
                  full((1, MLA_Q_LORA)), full((1, MLA_KV_LORA)),
                  full(wq.shape), full(wqr.shape), full(wk.shape), full(wv.shape),
                  pl.BlockSpec((tm, LANES), lambda i: (i % nS, 0)),
                  pl.BlockSpec((tm, LANES), lambda i: (i % nS, 0))],
        out_specs=[hm_spec, hm_spec, pl.BlockSpec((H, MLA_VT_ROWS, tm), lambda i: (0, 0, i))],
        out_shape=[hm_shape, hm_shape, jax.ShapeDtypeStruct((H, MLA_VT_ROWS, T), BF16)],
        compiler_params=_cparams("parallel"),
    )(lat, q_norm.reshape(1, -1), kv_norm.reshape(1, -1), wq, wqr, wk, wv, cos, sin)

    q4, k4 = (t.reshape(H, B, S, LANES) for t in (q, k))
    assert tq == tk and H % hp == 0
    o = pl.pallas_call(
        functools.partial(_mla_attn_kernel, tq=tq, tk=tk),
        grid=(B, H // hp, S // tq),
        in_specs=[pl.BlockSpec((hp, None, tq, LANES), lambda b, h, qi: (h, b, qi, 0)),
                  pl.BlockSpec((hp, None, S, LANES), lambda b, h, qi: (h, b, 0, 0), pipeline_mode=pl.Buffered(1)),
                  pl.BlockSpec((hp, MLA_VT_ROWS, S), lambda b, h, qi: (h, 0, b), pipeline_mode=pl.Buffered(1))],
        out_specs=pl.BlockSpec((hp, None, tq, LANES), lambda b, h, qi: (h, b, qi, 0)),
        out_shape=jax.ShapeDtypeStruct((H, B, S, LANES), BF16),
        scratch_shapes=[pltpu.VMEM((hp, 1, tq), F32), pltpu.VMEM((hp, MLA_VT_ROWS, tq), F32),
                        pltpu.VMEM((hp, tk, tq), F32), pltpu.VMEM((hp, tk, tq), F32)],
        compiler_params=_cparams("parallel", "parallel", "arbitrary"),
    )(q4, k4, v)

    w_o = jnp.pad(w_out.reshape(H, MLA_V, D), ((0, 0), (0, LANES - MLA_V), (0, 0))).reshape(H * LANES, D)
    return _linear_res(o.reshape(H, T, LANES), w_o.astype(BF16), x, head_major=True)


def _ret_kernel(q_ref, k_ref, v_ref, g_ref, cos_ref, sin_ref, gnw_ref, y_ref, state_scr, *, C):
    @pl.when(pl.program_id(1) == 0)
    def _():
        state_scr[...] = jnp.zeros(state_scr.shape, F32)

    half = RET_QK // 2
    cos = cos_ref[...]
    sin = sin_ref[...]
    ii = lax.broadcasted_iota(jnp.int32, (C, C), 0)
    jj = lax.broadcasted_iota(jnp.int32, (C, C), 1)
    diff = (ii - jj).astype(F32)
    causal = ii >= jj
    idx = lax.broadcasted_iota(jnp.int32, (C, 1), 0).astype(F32)

    def rope(t):
        t1, t2 = t[:, :half], t[:, half:]
        return jnp.concatenate([t1 * cos - t2 * sin, t1 * sin + t2 * cos], axis=1)

    for h in range(RET_HEADS):
        lg = math.log1p(-(2.0 ** (-5.0 - h)))
        intra = jnp.exp(jnp.where(causal, lg * diff, -jnp.inf))
        q_dec = jnp.exp(lg * (idx + 1.0))
        k_dec = jnp.exp(lg * (C - 1.0 - idx))
        c_dec = math.exp(lg * C)
        q = rope(q_ref[:, h * RET_QK:(h + 1) * RET_QK].astype(F32))
        k = rope(k_ref[:, h * RET_QK:(h + 1) * RET_QK].astype(F32)) * (RET_QK ** -0.5)
        v = v_ref[:, h * RET_V:(h + 1) * RET_V]
        st = state_scr[h]
        a = _dot_nt(q.astype(BF16), k.astype(BF16)) * intra
        o = _dot(a.astype(BF16), v) + _dot((q * q_dec).astype(BF16), st.astype(BF16))
        state_scr[h] = st * c_dec + _dot((k * k_dec).T.astype(BF16), v)
        mu = jnp.mean(o, axis=-1, keepdims=True)
        oc = o - mu
        var = jnp.mean(oc * oc, axis=-1, keepdims=True)
        on = oc * lax.rsqrt(var + EPS) * gnw_ref[:, h * RET_V:(h + 1) * RET_V]
        gt = g_ref[:, h * RET_V:(h + 1) * RET_V].astype(F32)
        y_ref[:, h * RET_V:(h + 1) * RET_V] = (gt * jax.nn.sigmoid(gt) * on).astype(y_ref.dtype)


def _retention(x, g, w_in, gn_w, w_out, B, S, C=128):
    T, D = x.shape
    nqk, nv = RET_HEADS * RET_QK, RET_HEADS * RET_V
    proj = _norm_linear(x, g, w_in.astype(BF16), BF16)
    half = RET_QK // 2
    ang = jnp.arange(S, dtype=F32)[:, None] * (RET_THETA ** (-jnp.arange(half, dtype=F32) / half))
    cos, sin = jnp.cos(ang), jnp.sin(ang)
    nc = S // C
    y = pl.pallas_call(
        functools.partial(_ret_kernel, C=C),
        grid=(B, nc),
        in_specs=[pl.BlockSpec((C, nqk), lambda b, c: (b * nc + c, 0)),
                  pl.BlockSpec((C, nqk), lambda b, c: (b * nc + c, 1)),
                  pl.BlockSpec((C, nv), lambda b, c: (b * nc + c, (2 * nqk) // nv)),
                  pl.BlockSpec((C, nv), lambda b, c: (b * nc + c, (2 * nqk) // nv + 1)),
                  pl.BlockSpec((C, RET_QK // 2), lambda b, c: (c, 0)),
                  pl.BlockSpec((C, RET_QK // 2), lambda b, c: (c, 0)),
                  pl.BlockSpec((1, nv), lambda b, c: (0, 0))],
        out_specs=pl.BlockSpec((C, nv), lambda b, c: (b * nc + c, 0)),
        out_shape=jax.ShapeDtypeStruct((T, nv), BF16),
        scratch_shapes=[pltpu.VMEM((RET_HEADS, RET_QK, RET_V), F32)],
        compiler_params=_cparams("parallel", "arbitrary"),
    )(proj, proj, proj, proj, cos, sin, gn_w.reshape(1, nv))
    return _linear_res(y, w_out.astype(BF16), x)


def _split3(a):
    a1 = a.astype(BF16)
    r = a - a1.astype(F32)
    a2 = r.astype(BF16)
    a3 = (r - a2.astype(F32)).astype(BF16)
    return a1, a2, a3


SSD_COLB = 1024


def _ssd_kernel(*refs, C):
    nz, nx = SSD_D_INNER // SSD_COLB, SSD_CONV_DIM // SSD_COLB
    z_refs, x_refs = refs[:nz], refs[nz:nz + nx]
    (dt_ref, cw_ref, cb_ref, dtb_ref, alog_ref, dsk_ref, nw_ref, y_ref,
     xe_scr, act_scr, st_scr, y_scr) = refs[nz + nx:]
    G, N, P = SSD_GROUPS, SSD_STATE, SSD_P
    DI = SSD_D_INNER
    pairs_per_group = (SSD_HEADS // G) // 2
    tiles_per_colb = SSD_COLB // LANES

    HALO = 8

    @pl.when(pl.program_id(1) == 0)
    def _():
        xe_scr[0:HALO, :] = jnp.zeros((HALO, xe_scr.shape[1]), F32)
        st_scr[...] = jnp.zeros(st_scr.shape, F32)

    for kx in range(nx):
        xe_scr[HALO:HALO + C, kx * SSD_COLB:(kx + 1) * SSD_COLB] = x_refs[kx][...].astype(F32)
    CB = 512
    for cb in range(SSD_CONV_DIM // CB):
        cs = slice(cb * CB, (cb + 1) * CB)
        acc = jnp.broadcast_to(cb_ref[:, cs], (C, CB))
        for kk in range(SSD_CONV):
            off = HALO - (SSD_CONV - 1) + kk
            acc = acc + cw_ref[kk:kk + 1, cs] * xe_scr[off:off + C, cs]
        act_scr[:, cs] = (acc * jax.nn.sigmoid(acc)).astype(BF16)
    xe_scr[0:HALO, :] = xe_scr[C:C + HALO, :]

    dtr = dt_ref[...] + dtb_ref[...]
    dt = jnp.maximum(dtr, 0.0) + jnp.log1p(jnp.exp(-jnp.abs(dtr)))
    a = dt * (-jnp.exp(alog_ref[...]))
    ii = lax.broadcasted_iota(jnp.int32, (C, C), 0)
    jj = lax.broadcasted_iota(jnp.int32, (C, C), 1)
    causal = ii >= jj
    tril = jnp.where(causal, 1.0, 0.0).astype(BF16)
    a1, a2, a3 = _split3(a)
    a_cs = _dot(tril, a1) + _dot(tril, a2) + _dot(tril, a3)
    a_end = a_cs[C - 1:C, :]
    w_end = jnp.exp(a_end - a_cs) * dt
    ea = jnp.exp(a_cs)
    eend = jnp.exp(a_end)
    a_cs_t = a_cs.T
    dt_t = dt.T
    lane = lax.broadcasted_iota(jnp.int32, (C, LANES), 1)
    lo = lane < P

    def pair(t, ha):
        return jnp.where(lo[:t.shape[0]], t[:, ha:ha + 1], t[:, ha + 1:ha + 2])

    for g in range(G):
        bm = act_scr[:, DI + g * N:DI + (g + 1) * N]
        cm = act_scr[:, DI + G * N + g * N:DI + G * N + (g + 1) * N]
        cbm = _dot_nt(cm, bm)
        bm_t = bm.astype(F32).T.astype(BF16)
        ssq = jnp.zeros((C, 1), F32)
        for pj in range(pairs_per_group):
            j = g * pairs_per_group + pj
            ha = 2 * j
            cs = slice(j * LANES, (j + 1) * LANES)
            xp = act_scr[:, cs]
            ys = []
            for hh in (ha, ha + 1):
                seg = a_cs[:, hh:hh + 1] - a_cs_t[hh:hh + 1, :]
                dec = jnp.exp(jnp.where(causal, seg, -jnp.inf))
                mm = cbm * dec * dt_t[hh:hh + 1, :]
                ys.append(_dot(mm.astype(BF16), xp))
            y = jnp.where(lo, ys[0], ys[1])
            st = st_scr[:, cs]
            y = y + pair(ea, ha) * _dot(cm, st.astype(BF16))
            xpf = xp.astype(F32)
            xw = (xpf * pair(w_end, ha)).astype(BF16)
            st_scr[:, cs] = st * pair(eend, ha) + _dot(bm_t, xw)
            y = y + pair(dsk_ref[...], ha) * xpf
            jz = j % tiles_per_colb
            zt = z_refs[j // tiles_per_colb][:, jz * LANES:(jz + 1) * LANES].astype(F32)
            y = y * (zt * jax.nn.sigmoid(zt))
            ssq = ssq + jnp.sum(y * y, axis=-1, keepdims=True)
            y_scr[:, cs] = y
        gw = pairs_per_group * LANES
        gs = slice(g * gw, (g + 1) * gw)
        rs = lax.rsqrt(ssq / float(gw) + EPS)
        y_ref[:, gs] = (y_scr[:, gs] * rs * nw_ref[:, gs]).astype(y_ref.dtype)


def _pad_lanes(v):
    return jnp.pad(v.astype(F32), (0, LANES - v.shape[0])).reshape(1, LANES)


def _ssd(x, g, w_in, conv_w, conv_b, dt_bias, A_log, D_skip, norm_w, w_out, B, S, C=128):
    T, D = x.shape
    DI, CD = SSD_D_INNER, SSD_CONV_DIM
    w_main = w_in[:, :DI + CD].astype(BF16)
    w_dt = jnp.pad(w_in[:, DI + CD:], ((0, 0), (0, LANES - SSD_HEADS))).astype(BF16)
    proj = _norm_linear(x, g, w_main, BF16)
    dt = _norm_linear(x, g, w_dt, F32)
    nc = S // C
    full = lambda shp: pl.BlockSpec(shp, lambda b, c: (0,) * len(shp))
    ncolb = (DI + CD) // SSD_COLB

    def colb(kc):
        return pl.BlockSpec((C, SSD_COLB), lambda b, c: (b * nc + c, kc))

    y = pl.pallas_call(
        functools.partial(_ssd_kernel, C=C),
        grid=(B, nc),
        in_specs=[colb(kc) for kc in range(ncolb)] + [
                  pl.BlockSpec((C, LANES), lambda b, c: (b * nc + c, 0)),
                  full((SSD_CONV, CD)), full((1, CD)), full((1, LANES)), full((1, LANES)), full((1, LANES)),
                  full((1, DI))],
        out_specs=pl.BlockSpec((C, DI), lambda b, c: (b * nc + c, 0)),
        out_shape=jax.ShapeDtypeStruct((T, DI), BF16),
        scratch_shapes=[pltpu.VMEM((C + 8, CD), F32), pltpu.VMEM((C, CD), BF16),
                        pltpu.VMEM((SSD_STATE, DI), F32), pltpu.VMEM((C, DI), F32)],
        compiler_params=_cparams("parallel", "arbitrary"),
    )(*([proj] * ncolb), dt, conv_w, conv_b.reshape(1, CD), _pad_lanes(dt_bias), _pad_lanes(A_log),
      _pad_lanes(D_skip), norm_w.reshape(1, DI))
    return _linear_res(y, w_out.astype(BF16), x)


ROW_TILE = 8


def _rows_to_tiles(ref, val):
    n = val.shape[0]
    for s in range(ROW_TILE):
        ref[pl.ds(s, n, stride=ROW_TILE), :] = val[:, s * LANES:(s + 1) * LANES]


def _tiles_to_rows(ref, n):
    return jnp.concatenate([ref[pl.ds(s, n, stride=ROW_TILE), :] for s in range(ROW_TILE)], axis=1)


def _router_kernel(x_ref, g_ref, w1_ref, w2_ref, b_ref, h_ref, route_ref, cnt_ref, dense_ref, carry_scr):
    E = MOE_EXPERTS

    @pl.when(pl.program_id(0) == 0)
    def _():
        carry_scr[...] = jnp.zeros(carry_scr.shape, F32)

    h = _rms(x_ref[...], g_ref[...])
    _rows_to_tiles(h_ref, h)
    tm = h.shape[0]
    h1 = h.astype(BF16)
    h2 = (h - h1.astype(F32)).astype(BF16)
    logits = _dot(h1, w1_ref[...]) + _dot(h1, w2_ref[...]) + _dot(h2, w1_ref[...]) + b_ref[...]
    lane = lax.broadcasted_iota(jnp.int32, (tm, LANES), 1)
    lanef = lane.astype(F32)
    big = float(LANES)
    glog = jnp.where((lane >= E) & (lane < E + MOE_GROUPS), logits, -jnp.inf)
    gmax = jnp.max(glog, axis=-1, keepdims=True)
    gval = 1.0 / jnp.sum(jnp.exp(glog - gmax), axis=-1, keepdims=True)
    gidx = jnp.min(jnp.where(glog == gmax, lanef, big), axis=-1, keepdims=True) - float(E)
    in_group = (lane < E) & ((lane // MOE_EPG).astype(F32) == gidx)
    el = jnp.where(in_group, logits, -jnp.inf)
    v1 = jnp.max(el, axis=-1, keepdims=True)
    i1 = jnp.min(jnp.where(el == v1, lanef, big), axis=-1, keepdims=True)
    el2 = jnp.where(lanef == i1, -jnp.inf, el)
    v2 = jnp.max(el2, axis=-1, keepdims=True)
    i2 = jnp.min(jnp.where(el2 == v2, lanef, big), axis=-1, keepdims=True)
    e2 = jnp.exp(v2 - v1)
    wa = gval / (1.0 + e2)
    wb = gval * e2 / (1.0 + e2)
    oh1 = jnp.where(lanef == i1, 1.0, 0.0)
    oh2 = jnp.where(lanef == i2, 1.0, 0.0)
    ohs = oh1 + oh2
    ii = lax.broadcasted_iota(jnp.int32, (tm, tm), 0)
    jj = lax.broadcasted_iota(jnp.int32, (tm, tm), 1)
    stril = jnp.where(ii > jj, 1.0, 0.0).astype(BF16)
    before = _dot(stril, ohs.astype(BF16)) + carry_scr[0:1, :]
    r1 = jnp.sum(before * oh1, axis=-1, keepdims=True)
    r2 = jnp.sum(before * oh2, axis=-1, keepdims=True)
    carry = carry_scr[0:1, :] + jnp.sum(ohs, axis=0, keepdims=True)
    carry_scr[...] = jnp.broadcast_to(carry, carry_scr.shape)
    cnt_ref[...] = jnp.broadcast_to(carry, cnt_ref.shape)
    route = jnp.zeros((tm, LANES), F32)
    for ln, val in enumerate((i1, i2, wa, wb, r1, r2)):
        route = jnp.where(lane == ln, val, route)
    route_ref[...] = route
    eye = (lax.broadcasted_iota(jnp.int32, (LANES, LANES), 0) == lax.broadcasted_iota(jnp.int32, (LANES, LANES), 1))
    rows = []
    for val in (i1, i2, r1, r2):
        vb = jnp.broadcast_to(val, (tm, LANES))
        for r in range(tm // LANES):
            rows.append(jnp.sum(jnp.where(eye, vb[r * LANES:(r + 1) * LANES], 0.0), axis=0, keepdims=True))
    dense_ref[...] = jnp.concatenate(rows, axis=0)


def _dispatch_kernel(pos_ref, lo_ref, hi_ref, h_ref, xg_hbm, zero_scr, sem, zsem):
    ntok = pos_ref.shape[0] // MOE_TOPK
    base = pl.program_id(0) * TOK_TILE

    @pl.when(pl.program_id(0) == 0)
    def _():
        zero_scr[...] = jnp.zeros(zero_scr.shape, F32)
        blk = zero_scr.shape[0]

        def zero_copy(r, n):
            return pltpu.make_async_copy(zero_scr.at[pl.ds(0, n)], xg_hbm.at[pl.ds(pl.multiple_of(r * n, n), n)], zsem)

        def loop(lo, hi, n, wait):
            def body(r, c):
                cp = zero_copy(r, n)
                cp.wait() if wait else cp.start()
                return c
            lax.fori_loop(lo, hi, body, 0)

        tail_lo = hi_ref[MOE_EXPERTS - 1] * ROW_TILE // blk
        tail_hi = xg_hbm.shape[0] // blk
        for wait in (False, True):
            for e in range(MOE_EXPERTS):
                loop(lo_ref[e], hi_ref[e], ROW_TILE, wait)
            loop(tail_lo, tail_hi, blk, wait)

    def copy(t, k):
        row = pl.multiple_of(pos_ref[k * ntok + base + t] * ROW_TILE, ROW_TILE)
        return pltpu.make_async_copy(h_ref.at[pl.ds(t * ROW_TILE, ROW_TILE)], xg_hbm.at[pl.ds(row, ROW_TILE)], sem)

    for t in range(TOK_TILE):
        for k in range(MOE_TOPK):
            copy(t, k).start(priority=k)
    same_size = pltpu.make_async_copy(h_ref.at[pl.ds(0, ROW_TILE)], xg_hbm.at[pl.ds(0, ROW_TILE)], sem)
    for _ in range(TOK_TILE * MOE_TOPK):
        same_size.wait()


def _expert_kernel(be_ref, nu_ref, x_ref, wg_ref, wu_ref, wd_ref, o_ref, wg_scr, wu_scr, wd_scr):
    i = pl.program_id(0)
    used = i < nu_ref[0]
    n = x_ref.shape[0] // ROW_TILE

    @pl.when(used & ((i == 0) | (be_ref[i] != be_ref[jnp.maximum(i - 1, 0)])))
    def _():
        wg_scr[...] = wg_ref[...].astype(BF16)
        wu_scr[...] = wu_ref[...].astype(BF16)
        wd_scr[...] = wd_ref[...].astype(BF16)

    @pl.when(used)
    def _():
        xb = _tiles_to_rows(x_ref, n).astype(BF16)
        hg = _dot(xb, wg_scr[...])
        hu = _dot(xb, wu_scr[...])
        _rows_to_tiles(o_ref, _dot((hg * jax.nn.sigmoid(hg) * hu).astype(BF16), wd_scr[...]))

    @pl.when(jnp.logical_not(used))
    def _():
        o_ref[...] = jnp.zeros(o_ref.shape, F32)


def _ple_kernel(pos_ref, x_ref, rt_ref, p_ref, g_ref, gf_ref, wg_ref, wp_ref, yb_hbm, o_ref,
                buf_a, buf_b, sems, *, final):
    tp = x_ref.shape[0]
    i = pl.program_id(0)
    last = pl.num_programs(0) - 1
    slot = i % 2
    bufs = (buf_a, buf_b)

    def copy(tile, sl, t, k):
        row = pl.multiple_of(pos_ref[k * (pos_ref.shape[0] // MOE_TOPK) + tile * tp + t] * ROW_TILE, ROW_TILE)
        dst = bufs[k].at[sl, pl.ds(pl.multiple_of(t * ROW_TILE, ROW_TILE), ROW_TILE)]
        return pltpu.make_async_copy(yb_hbm.at[pl.ds(row, ROW_TILE)], dst, sems.at[sl])

    def wait_slot(sl):
        for k in range(MOE_TOPK):
            same_size = pltpu.make_async_copy(yb_hbm.at[pl.ds(0, ROW_TILE)], bufs[k].at[sl, pl.ds(0, ROW_TILE)],
                                              sems.at[sl])
            for _ in range(tp):
                same_size.wait()

    @pl.when(i == 0)
    def _():
        def first(t, c):
            for k in range(MOE_TOPK):
                copy(0, 0, t, k).start(priority=k)
            return c
        lax.fori_loop(0, tp, first, 0, unroll=8)

    wait_slot(slot)
    nxt = jnp.minimum(i + 1, last)
    for t in range(tp):
        for k in range(MOE_TOPK):
            copy(nxt, 1 - slot, t, k).start(priority=k)

    ple = _dot(p_ref[...].astype(BF16), wp_ref[...])
    rt = rt_ref[...]
    ya = _tiles_to_rows(buf_a.at[slot], tp)
    yb = _tiles_to_rows(buf_b.at[slot], tp)
    x1 = x_ref[...] + rt[:, 2:3] * ya + rt[:, 3:4] * yb
    gate = jax.nn.sigmoid(_dot(_rms(x1, g_ref[...]).astype(BF16), wg_ref[...]))
    out = x1 + ple * gate
    if final:
        out = _rms(out, gf_ref[...])
    o_ref[...] = out

    @pl.when(i == last)
    def _():
        wait_slot(1 - slot)


def _moe_ple(x, layer, p_all, norm_moe, w_group, b_group, w_expert, b_expert, w_gate, w_up, w_down,
             norm_ple, ple_w, ple_w_gate, norm_final, final, tm=512):
    T, D = x.shape
    E, BLK = MOE_EXPERTS, MOE_BLOCK
    A = T * MOE_TOPK
    nb = A // BLK + E
    P = nb * BLK

    w_r = jnp.zeros((D, LANES), F32).at[:, :E].set(w_expert).at[:, E:E + MOE_GROUPS].set(w_group)
    b_r = jnp.zeros((1, LANES), F32).at[0, :E].set(b_expert).at[0, E:E + MOE_GROUPS].set(b_group)
    w_r1 = w_r.astype(BF16)
    w_r2 = (w_r - w_r1.astype(F32)).astype(BF16)
    full = lambda shp: pl.BlockSpec(shp, lambda i: (0,) * len(shp))
    nsub = tm // LANES
    h, route, cnt, dense = pl.pallas_call(
        _router_kernel,
        grid=(T // tm,),
        in_specs=[pl.BlockSpec((tm, D), lambda i: (i, 0)), full((1, D)), full((D, LANES)), full((D, LANES)),
                  full((1, LANES))],
        out_specs=[pl.BlockSpec((tm * ROW_TILE, LANES), lambda i: (i, 0)),
                   pl.BlockSpec((tm, LANES), lambda i: (i, 0)), full((8, LANES)),
                   pl.BlockSpec((2 * MOE_TOPK * nsub, LANES), lambda i: (i, 0))],
        out_shape=[jax.ShapeDtypeStruct((T * ROW_TILE, LANES), F32), jax.ShapeDtypeStruct((T, LANES), F32),
                   jax.ShapeDtypeStruct((8, LANES), F32),
                   jax.ShapeDtypeStruct((T // tm * 2 * MOE_TOPK * nsub, LANES), F32)],
        scratch_shapes=[pltpu.VMEM((8, LANES), F32)],
        compiler_params=_cparams("arbitrary"),
    )(x, norm_moe.reshape(1, D), w_r1, w_r2, b_r)

    counts = cnt[0, :E].astype(jnp.int32)
    padded = (counts + BLK - 1) // BLK * BLK
    pad_end = jnp.cumsum(padded)
    pad_start = pad_end - padded
    dense = dense.reshape(T // tm, 2 * MOE_TOPK, nsub * LANES).astype(jnp.int32)
    def slot(k):
        e, acc = dense[:, k], dense[:, MOE_TOPK + k]
        for j in range(E):
            acc = acc + jnp.where(e == j, pad_start[j], 0)
        return acc.reshape(T)

    pos = jnp.concatenate([slot(k) for k in range(MOE_TOPK)])
    blk_start = jnp.arange(nb, dtype=jnp.int32) * BLK
    blk_expert = jnp.minimum(jnp.sum((pad_end[None, :] <= blk_start[:, None]).astype(jnp.int32), axis=1), E - 1)
    n_used = (pad_end[-1:] // BLK).astype(jnp.int32)

    assert D == ROW_TILE * LANES
    xg = pl.pallas_call(
        _dispatch_kernel,
        grid_spec=pltpu.PrefetchScalarGridSpec(
            num_scalar_prefetch=3, grid=(T // TOK_TILE,),
            in_specs=[pl.BlockSpec((TOK_TILE * ROW_TILE, LANES), lambda i, ps, lo, hi: (i, 0))],
            out_specs=pl.BlockSpec(memory_space=pl.ANY),
            scratch_shapes=[pltpu.VMEM((BLK * ROW_TILE, LANES), F32), pltpu.SemaphoreType.DMA,
                            pltpu.SemaphoreType.DMA]),
        out_shape=jax.ShapeDtypeStruct((P * ROW_TILE, LANES), F32),
        compiler_params=_cparams("arbitrary"),
    )(pos, pad_start + counts, pad_end, h)

    def wmap(i, be, nu):
        return (layer, be[jnp.minimum(i, nu[0] - 1)], 0, 0)

    def xmap(i, be, nu):
        return (jnp.minimum(i, nu[0] - 1), 0)

    yb = pl.pallas_call(
        _expert_kernel,
        grid_spec=pltpu.PrefetchScalarGridSpec(
            num_scalar_prefetch=2, grid=(nb,),
            in_specs=[pl.BlockSpec((BLK * ROW_TILE, LANES), xmap),
                      pl.BlockSpec((None, None, D, MOE_FF), wmap), pl.BlockSpec((None, None, D, MOE_FF), wmap),
                      pl.BlockSpec((None, None, MOE_FF, D), wmap)],
            out_specs=pl.BlockSpec((BLK * ROW_TILE, LANES), lambda i, be, nu: (i, 0)),
            scratch_shapes=[pltpu.VMEM((D, MOE_FF), BF16), pltpu.VMEM((D, MOE_FF), BF16),
                            pltpu.VMEM((MOE_FF, D), BF16)]),
        out_shape=jax.ShapeDtypeStruct((P * ROW_TILE, LANES), F32),
        compiler_params=_cparams("arbitrary"),
    )(blk_expert, n_used, xg, w_gate, w_up, w_down)

    tp = TOK_TILE
    PD = p_all.shape[-1]
    return pl.pallas_call(
        functools.partial(_ple_kernel, final=final),
        grid_spec=pltpu.PrefetchScalarGridSpec(
            num_scalar_prefetch=1, grid=(T // tp,),
            in_specs=[pl.BlockSpec((tp, D), lambda i, ps: (i, 0)),
                      pl.BlockSpec((tp, LANES), lambda i, ps: (i, 0)),
                      pl.BlockSpec((None, tp, PD), lambda i, ps: (layer, i, 0)),
                      pl.BlockSpec((1, D), lambda i, ps: (0, 0)), pl.BlockSpec((1, D), lambda i, ps: (0, 0)),
                      pl.BlockSpec((D, D), lambda i, ps: (0, 0)), pl.BlockSpec((PD, D), lambda i, ps: (0, 0)),
                      pl.BlockSpec(memory_space=pl.ANY)],
            out_specs=pl.BlockSpec((tp, D), lambda i, ps: (i, 0)),
            scratch_shapes=[pltpu.VMEM((2, tp * ROW_TILE, LANES), F32), pltpu.VMEM((2, tp * ROW_TILE, LANES), F32),
                            pltpu.SemaphoreType.DMA((2,))]),
        out_shape=jax.ShapeDtypeStruct((T, D), F32),
        compiler_params=_cparams("arbitrary"),
    )(pos, x, route, p_all, norm_ple.reshape(1, D), norm_final.reshape(1, D),
      ple_w_gate.astype(BF16), ple_w.astype(BF16), yb)


def kernel(x, p, norm_mix, norm_moe, norm_ple, norm_final, dsa_w_in, dsa_w_out, mla_w_in, mla_q_norm, mla_w_qb,
           mla_kv_norm, mla_w_kvb, mla_w_out, ret_w_in, ret_gn_w, ret_w_out, ssd_w_in, ssd_conv_w, ssd_conv_b,
           ssd_dt_bias, ssd_A_log, ssd_D, ssd_norm_w, ssd_w_out, moe_w_group, moe_b_group, moe_w_expert,
           moe_b_expert, moe_w_gate, moe_w_up, moe_w_down, ple_w, ple_w_gate):
    B, S, D = x.shape
    depth = p.shape[0]
    xt = x.reshape(B * S, D)
    for i in range(depth):
        kind = i % 4
        if kind == 0:
            xt = _dilated_attention(xt, norm_mix[i], dsa_w_in, dsa_w_out, B, S)
        elif kind == 1:
            xt = _mla(xt, norm_mix[i], mla_w_in, mla_q_norm, mla_w_qb, mla_kv_norm, mla_w_kvb, mla_w_out, B, S)
        elif kind == 2:
            xt = _retention(xt, norm_mix[i], ret_w_in, ret_gn_w, ret_w_out, B, S)
        else:
            xt = _ssd(xt, norm_mix[i], ssd_w_in, ssd_conv_w, ssd_conv_b, ssd_dt_bias, ssd_A_log, ssd_D,
                      ssd_norm_w, ssd_w_out, B, S)
        xt = _moe_ple(xt, i, p.reshape(depth, B * S, -1), norm_moe[i], moe_w_group[i], moe_b_group[i],
                      moe_w_expert[i], moe_b_expert[i], moe_w_gate, moe_w_up, moe_w_down, norm_ple[i], ple_w[i],
                      ple_w_gate[i], norm_final, final=(i == depth - 1))
    return xt.reshape(B, S, D)
```

```python
import functools
import math

import jax
import jax.numpy as jnp
from jax import lax
from jax.experimental import pallas as pl
from jax.experimental.pallas import tpu as pltpu

F32 = jnp.float32
BF16 = jnp.bfloat16
EPS = 1e-6
LANES = 128
VMEM_LIMIT = 48 * 1024 * 1024

DSA_PATTERNS = ((128, 1), (512, 4), (2048, 16))
DSA_HEADS = 8
DSA_SPAN = 2048
MLA_HEADS = 16
MLA_Q_LORA, MLA_KV_LORA, MLA_NOPE, MLA_ROPE, MLA_V = 256, 128, 64, 32, 64
MLA_THETA = 10000.0
MLA_VT_ROWS = 80
RET_HEADS, RET_QK, RET_V = 4, 256, 512
RET_THETA = 10000.0
SSD_D_INNER, SSD_HEADS, SSD_GROUPS, SSD_STATE, SSD_CONV, SSD_P = 2048, 32, 4, 128, 4, 64
SSD_CONV_DIM = SSD_D_INNER + 2 * SSD_GROUPS * SSD_STATE
MOE_GROUPS, MOE_EPG, MOE_EXPERTS, MOE_TOPK, MOE_FF = 4, 8, 32, 2, 512
MOE_BLOCK = 256
TOK_TILE = 256


def _cparams(*sem):
    return pltpu.CompilerParams(dimension_semantics=sem, vmem_limit_bytes=VMEM_LIMIT)


def _rms(x, g):
    return x * lax.rsqrt(jnp.mean(x * x, axis=-1, keepdims=True) + EPS) * g


def _dot(a, b):
    return jnp.dot(a, b, preferred_element_type=F32)


def _dot_nt(a, b):
    return lax.dot_general(a, b, (((1,), (1,)), ((), ())), preferred_element_type=F32)


def _norm_linear_kernel(x_ref, g_ref, w_ref, o_ref, h_scr, *, head_major):
    @pl.when(pl.program_id(1) == 0)
    def _():
        h_scr[...] = _rms(x_ref[...], g_ref[...]).astype(BF16)

    r = _dot(h_scr[...], w_ref[...])
    if head_major:
        for c in range(r.shape[1] // LANES):
            o_ref[c] = r[:, c * LANES:(c + 1) * LANES].astype(o_ref.dtype)
    else:
        o_ref[...] = r.astype(o_ref.dtype)


def _norm_linear(x, g, w, out_dtype, *, head_major=False, tm=1024, tn=1024):
    T, D = x.shape
    N = w.shape[1]
    tn = min(tn, N)
    assert T % tm == 0 and N % tn == 0 and tn % LANES == 0
    if head_major:
        out_shape = jax.ShapeDtypeStruct((N // LANES, T, LANES), out_dtype)
        out_spec = pl.BlockSpec((tn // LANES, tm, LANES), lambda i, j: (j, i, 0))
    else:
        out_shape = jax.ShapeDtypeStruct((T, N), out_dtype)
        out_spec = pl.BlockSpec((tm, tn), lambda i, j: (i, j))
    return pl.pallas_call(
        functools.partial(_norm_linear_kernel, head_major=head_major),
        grid=(T // tm, N // tn),
        in_specs=[pl.BlockSpec((tm, D), lambda i, j: (i, 0)),
                  pl.BlockSpec((1, D), lambda i, j: (0, 0)),
                  pl.BlockSpec((D, tn), lambda i, j: (0, j))],
        out_specs=out_spec,
        out_shape=out_shape,
        scratch_shapes=[pltpu.VMEM((tm, D), BF16)],
        compiler_params=_cparams("parallel", "arbitrary"),
    )(x, g.reshape(1, D), w)


def _linear_res_kernel(a_ref, w_ref, r_ref, o_ref, *, head_major):
    if head_major:
        a = jnp.concatenate([a_ref[c] for c in range(a_ref.shape[0])], axis=1)
    else:
        a = a_ref[...]
    o_ref[...] = r_ref[...] + _dot(a, w_ref[...])


def _linear_res(a, w, res, *, head_major=False, tm=512):
    T, D = res.shape
    K = w.shape[0]
    if head_major:
        a_spec = pl.BlockSpec((K // LANES, tm, LANES), lambda i: (0, i, 0))
    else:
        a_spec = pl.BlockSpec((tm, K), lambda i: (i, 0))
    return pl.pallas_call(
        functools.partial(_linear_res_kernel, head_major=head_major),
        grid=(T // tm,),
        in_specs=[a_spec,
                  pl.BlockSpec((K, D), lambda i: (0, 0)),
                  pl.BlockSpec((tm, D), lambda i: (i, 0))],
        out_specs=pl.BlockSpec((tm, D), lambda i: (i, 0)),
        out_shape=jax.ShapeDtypeStruct((T, D), F32),
        compiler_params=_cparams("parallel"),
    )(a, w, res)


def _dsa_group_kernel(sl_ref, q_ref, k_ref, kh_ref, v_ref, vh_ref, o_ref, sm_ref, sd_ref, *, dil, rows):
    W = LANES
    i = pl.program_id(1)
    h = pl.program_id(2)
    scale = float(W) ** -0.5
    slope = sl_ref[h] * float(dil)
    qi = lax.broadcasted_iota(jnp.int32, (W, 2 * W), 0)
    kj = lax.broadcasted_iota(jnp.int32, (W, 2 * W), 1)
    steps = W + qi - kj
    band = (steps >= 0) & (steps <= W)
    first_lim = jnp.where(i == 0, W, 0)
    band_first = band & (kj >= first_lim)
    bias = (-slope) * steps.astype(F32)
    lane = lax.broadcasted_iota(jnp.int32, (W, LANES), 1)

    @pl.when(h == 0)
    def _():
        sm_ref[...] = jnp.zeros(sm_ref.shape, F32)
        sd_ref[...] = jnp.ones(sd_ref.shape, F32)

    for r in range(dil):
        cs = slice(r * LANES, (r + 1) * LANES)
        for n in range(rows // W):
            rs = slice(n * W, (n + 1) * W)
            q = q_ref[rs, cs]
            if n == 0:
                kk = jnp.concatenate([kh_ref[:, cs], k_ref[rs, cs]], axis=0)
                vv = jnp.concatenate([vh_ref[:, cs], v_ref[rs, cs]], axis=0)
                valid = band_first
            else:
                kk = k_ref[(n - 1) * W:(n + 1) * W, cs]
                vv = v_ref[(n - 1) * W:(n + 1) * W, cs]
                valid = band
            s = _dot_nt(q, kk) * scale + bias
            s = jnp.where(valid, s, -jnp.inf)
            m = jnp.max(s, axis=-1, keepdims=True)
            e = jnp.exp(s - m)
            den = jnp.sum(e, axis=-1, keepdims=True)
            o = _dot(e.astype(BF16), vv) / den
            o_ref[rs, cs] = o.astype(o_ref.dtype)
            sm_ref[rs, cs] = jnp.where(lane == h, m, sm_ref[rs, cs])
            sd_ref[rs, cs] = jnp.where(lane == h, den, sd_ref[rs, cs])


def _dsa_proj_kernel(x_ref, g_ref, w_ref, o_ref, h_scr, r_scr, *, dil):
    @pl.when(pl.program_id(1) == 0)
    def _():
        h_scr[...] = _rms(x_ref[...], g_ref[...]).astype(BF16)

    res = _dot(h_scr[...], w_ref[...])
    rows = r_scr.shape[1] // dil
    for c in range(o_ref.shape[0]):
        if dil == 1:
            o_ref[c] = res[:, c * LANES:(c + 1) * LANES].astype(o_ref.dtype)
            continue
        r_scr[c] = res[:, c * LANES:(c + 1) * LANES]
        for r in range(dil):
            o_ref[c, :, r * LANES:(r + 1) * LANES] = r_scr[c, pl.ds(r, rows, stride=dil), :].astype(o_ref.dtype)


def _dsa_proj(x, g, w, dil, tm=1024):
    T, D = x.shape
    N = w.shape[1]
    tn = DSA_HEADS * LANES
    return pl.pallas_call(
        functools.partial(_dsa_proj_kernel, dil=dil),
        grid=(T // tm, N // tn),
        in_specs=[pl.BlockSpec((tm, D), lambda i, j: (i, 0)),
                  pl.BlockSpec((1, D), lambda i, j: (0, 0)),
                  pl.BlockSpec((D, tn), lambda i, j: (0, j))],
        out_specs=pl.BlockSpec((DSA_HEADS, tm // dil, dil * LANES), lambda i, j: (j, i, 0)),
        out_shape=jax.ShapeDtypeStruct((N // LANES, T // dil, dil * LANES), BF16),
        scratch_shapes=[pltpu.VMEM((tm, D), BF16), pltpu.VMEM((DSA_HEADS, tm, LANES), F32)],
        compiler_params=_cparams("parallel", "arbitrary"),
    )(x, g.reshape(1, D), w)


def _dsa_group(qkv, slopes, dil, B, S):
    H = DSA_HEADS
    L = S // dil
    rows = DSA_SPAN // dil
    width = dil * LANES
    qv = qkv.reshape(qkv.shape[0], B, L, width)
    hb = rows // LANES
    base = 0

    def qmap(o):
        return lambda b, i, h, sl: (base + o * H + h, b, i, 0)

    def hmap(o):
        return lambda b, i, h, sl: (base + o * H + h, b, jnp.maximum(i * hb - 1, 0), 0)

    blk = (None, None, rows, width)
    hblk = (None, None, LANES, width)
    o, sm, sd = pl.pallas_call(
        functools.partial(_dsa_group_kernel, dil=dil, rows=rows),
        grid_spec=pltpu.PrefetchScalarGridSpec(
            num_scalar_prefetch=1,
            grid=(B, L // rows, H),
            in_specs=[pl.BlockSpec(blk, qmap(0)),
                      pl.BlockSpec(blk, qmap(1)), pl.BlockSpec(hblk, hmap(1)),
                      pl.BlockSpec(blk, qmap(2)), pl.BlockSpec(hblk, hmap(2))],
            out_specs=[pl.BlockSpec(blk, lambda b, i, h, sl: (h, b, i, 0)),
                       pl.BlockSpec((None, rows, width), lambda b, i, h, sl: (b, i, 0)),
                       pl.BlockSpec((None, rows, width), lambda b, i, h, sl: (b, i, 0))]),
        out_shape=[jax.ShapeDtypeStruct((H, B, L, width), BF16),
                   jax.ShapeDtypeStruct((B, L, width), F32),
                   jax.ShapeDtypeStruct((B, L, width), F32)],
        compiler_params=_cparams("parallel", "parallel", "arbitrary"),
    )(slopes, qv, qv, qv, qv, qv)
    return o.reshape(H, B * L, width), sm.reshape(B * L, width), sd.reshape(B * L, width)


def _dsa_out_kernel(o0, o1, o2, m0, m1, m2, d0, d1, d2, w_ref, x_ref, out_ref, o_scr, s_scr):
    tm = x_ref.shape[0]
    dils = [d for _, d in DSA_PATTERNS]

    def token_major(ref, scr, dil):
        if dil == 1:
            return ref[...].astype(F32)
        for r in range(dil):
            scr[pl.ds(r, tm // dil, stride=dil), :] = ref[:, r * LANES:(r + 1) * LANES].astype(F32)
        return scr[...]

    ms = [token_major(m, s_scr.at[2 * g], dils[g]) for g, m in enumerate((m0, m1, m2))]
    dn = [token_major(d, s_scr.at[2 * g + 1], dils[g]) for g, d in enumerate((d0, d1, d2))]
    mx = jnp.maximum(jnp.maximum(ms[0], ms[1]), ms[2])
    ws = [d * jnp.exp(m - mx) for d, m in zip(dn, ms)]
    tot = ws[0] + ws[1] + ws[2]
    ws = [w / tot for w in ws]
    parts = []
    for h in range(DSA_HEADS):
        acc = None
        for g, o in enumerate((o0, o1, o2)):
            term = ws[g][:, h:h + 1] * token_major(o.at[h], o_scr.at[g, h], dils[g])
            acc = term if acc is None else acc + term
        parts.append(acc.astype(BF16))
    a = jnp.concatenate(parts, axis=1)
    out_ref[...] = x_ref[...] + _dot(a, w_ref[...])


def _dsa_out(os_, ms, ds, w, x, tm=512):
    T, D = x.shape
    H = DSA_HEADS
    ng = len(DSA_PATTERNS)
    ospecs = [pl.BlockSpec((H, tm // d, d * LANES), lambda i: (0, i, 0)) for _, d in DSA_PATTERNS]
    sspecs = [pl.BlockSpec((tm // d, d * LANES), lambda i: (i, 0)) for _, d in DSA_PATTERNS]
    return pl.pallas_call(
        _dsa_out_kernel,
        grid=(T // tm,),
        in_specs=ospecs + sspecs + sspecs + [pl.BlockSpec((H * LANES, D), lambda i: (0, 0)),
                                             pl.BlockSpec((tm, D), lambda i: (i, 0))],
        out_specs=pl.BlockSpec((tm, D), lambda i: (i, 0)),
        out_shape=jax.ShapeDtypeStruct((T, D), F32),
        scratch_shapes=[pltpu.VMEM((ng, H, tm, LANES), F32), pltpu.VMEM((2 * ng, tm, LANES), F32)],
        compiler_params=_cparams("parallel"),
    )(*os_, *ms, *ds, w, x)


def _dilated_attention(x, g, w_in, w_out, B, S):
    ng = len(DSA_PATTERNS)
    gw = 3 * DSA_HEADS * LANES
    slopes = 2.0 ** (-8.0 * jnp.arange(1, ng * DSA_HEADS + 1, dtype=F32) / (ng * DSA_HEADS))
    os_, ms, ds = [], [], []
    for gi, (window, dil) in enumerate(DSA_PATTERNS):
        assert window // dil == LANES
        qkv = _dsa_proj(x, g, w_in[:, gi * gw:(gi + 1) * gw].astype(BF16), dil)
        o, m, d = _dsa_group(qkv, slopes[gi * DSA_HEADS:(gi + 1) * DSA_HEADS], dil, B, S)
        os_.append(o)
        ms.append(m)
        ds.append(d)
    return _dsa_out(os_, ms, ds, w_out.astype(BF16), x)


def _mla_qkv_kernel(lat_ref, qn_ref, kvn_ref, wq_ref, wqr_ref, wk_ref, wv_ref, cos_ref, sin_ref,
                    q_ref, k_ref, vt_ref):
    lat = lat_ref[...]
    cos = cos_ref[...]
    sin = sin_ref[...]
    hq = _rms(lat[:, :MLA_Q_LORA], qn_ref[...]).astype(BF16)
    hkv = _rms(lat[:, MLA_Q_LORA:MLA_Q_LORA + MLA_KV_LORA], kvn_ref[...]).astype(BF16)
    o1 = MLA_Q_LORA + MLA_KV_LORA
    kr = lat[:, o1:o1 + LANES] * cos + lat[:, o1 + LANES:o1 + 2 * LANES] * sin
    q = _dot(hq, wq_ref[...])
    qr = _dot(hq, wqr_ref[...])
    k = _dot(hkv, wk_ref[...])
    v = _dot(hkv, wv_ref[...])
    lane = lax.broadcasted_iota(jnp.int32, cos.shape, 1)
    scale = float(MLA_NOPE + MLA_ROPE) ** -0.5 * math.log2(math.e)
    cos_q = jnp.where(lane < MLA_NOPE, 1.0, cos) * scale
    sin_q = sin * scale
    for h in range(MLA_HEADS):
        cs = slice(h * LANES, (h + 1) * LANES)
        q_ref[h] = (q[:, cs] * cos_q + qr[:, cs] * sin_q).astype(BF16)
        k_ref[h] = (k[:, cs] + kr).astype(BF16)
        vt_ref[h] = jnp.where(lane == MLA_V, 1.0, v[:, cs]).T[:MLA_VT_ROWS].astype(BF16)


def _mla_attn_kernel(q_ref, k_ref, vt_ref, o_ref, m_scr, acc_scr, sa_scr, sb_scr, *, tq, tk):
    qi = pl.program_id(2)
    heads = range(q_ref.shape[0])
    m_scr[...] = jnp.full(m_scr.shape, -jnp.inf, F32)
    acc_scr[...] = jnp.zeros(acc_scr.shape, F32)

    def scores(j, buf):
        for h in heads:
            buf[h] = _dot_nt(k_ref[h, pl.ds(pl.multiple_of(j * tk, tk), tk), :], q_ref[h])

    def update(j, buf, masked):
        off = pl.multiple_of(j * tk, tk)
        for h in heads:
            st = buf[h]
            if masked:
                kpos = off + lax.broadcasted_iota(jnp.int32, (tk, tq), 0)
                qpos = qi * tq + lax.broadcasted_iota(jnp.int32, (tk, tq), 1)
                st = jnp.where(kpos <= qpos, st, -jnp.inf)
            m_prev = m_scr[h]
            m_new = jnp.maximum(m_prev, jnp.max(st, axis=0, keepdims=True))
            p = jnp.exp2(st - m_new).astype(BF16)
            acc_scr[h] = jnp.exp2(m_prev - m_new) * acc_scr[h] + _dot(vt_ref[h, :, pl.ds(off, tk)], p)
            m_scr[h] = m_new

    scores(0, sa_scr)

    def body(jj, c):
        j = 2 * jj
        scores(j + 1, sb_scr)
        update(j, sa_scr, False)
        scores(j + 2, sa_scr)
        update(j + 1, sb_scr, False)
        return c

    lax.fori_loop(0, qi // 2, body, 0)

    @pl.when(qi % 2 == 0)
    def _():
        update(qi, sa_scr, True)

    @pl.when(qi % 2 == 1)
    def _():
        scores(qi, sb_scr)
        update(qi - 1, sa_scr, False)
        update(qi, sb_scr, True)

    for h in heads:
        acc = acc_scr[h]
        acc = jnp.concatenate([acc, jnp.zeros((LANES - MLA_VT_ROWS, tq), F32)], axis=0).T
        o_ref[h] = (acc / acc[:, MLA_V:MLA_V + 1]).astype(o_ref.dtype)


def _rope_tables(S, half, theta, lane_off):
    inv = theta ** (-jnp.arange(half, dtype=F32) / half)
    ang = jnp.arange(S, dtype=F32)[:, None] * inv
    cos = jnp.zeros((S, LANES), F32).at[:, lane_off:lane_off + 2 * half].set(jnp.tile(jnp.cos(ang), (1, 2)))
    sin = jnp.zeros((S, LANES), F32).at[:, lane_off:lane_off + 2 * half].set(jnp.tile(jnp.sin(ang), (1, 2)))
    return cos, sin


def _rot_half_cols(w, half):
    return jnp.concatenate([-w[..., half:], w[..., :half]], axis=-1)


def _pad_heads(w, heads, width):
    K = w.shape[0]
    w = w.reshape(K, heads, width)
    return jnp.pad(w, ((0, 0), (0, 0), (0, LANES - width))).reshape(K, heads * LANES)


def _mla(x, g, w_in, q_norm, w_qb, kv_norm, w_kvb, w_out, B, S, tm=512, tq=512, tk=512, hp=4):
    T, D = x.shape
    H = MLA_HEADS
    half = MLA_ROPE // 2
    o1 = MLA_Q_LORA + MLA_KV_LORA
    w_r = w_in[:, o1:]
    tile_r = jnp.zeros((D, LANES), F32).at[:, MLA_NOPE:MLA_NOPE + MLA_ROPE].set(w_r)
    tile_rr = jnp.zeros((D, LANES), F32).at[:, MLA_NOPE:MLA_NOPE + MLA_ROPE].set(_rot_half_cols(w_r, half))
    w_lat = jnp.concatenate([w_in[:, :o1], tile_r, tile_rr], axis=1).astype(BF16)
    lat = _norm_linear(x, g, w_lat, F32, tn=w_lat.shape[1])

    wq3 = w_qb.reshape(MLA_Q_LORA, H, MLA_NOPE + MLA_ROPE)
    wq_rot = jnp.concatenate([jnp.zeros_like(wq3[..., :MLA_NOPE]), _rot_half_cols(wq3[..., MLA_NOPE:], half)], axis=-1)
    wq = _pad_heads(w_qb, H, MLA_NOPE + MLA_ROPE).astype(BF16)
    wqr = _pad_heads(wq_rot.reshape(MLA_Q_LORA, -1), H, MLA_NOPE + MLA_ROPE).astype(BF16)
    wkv3 = w_kvb.reshape(MLA_KV_LORA, H, MLA_NOPE + MLA_V)
    wk = _pad_heads(wkv3[..., :MLA_NOPE].reshape(MLA_KV_LORA, -1), H, MLA_NOPE).astype(BF16)
    wv = _pad_heads(wkv3[..., MLA_NOPE:].reshape(MLA_KV_LORA, -1), H, MLA_V).astype(BF16)
    cos, sin = _rope_tables(S, half, MLA_THETA, MLA_NOPE)

    nS = S // tm
    full = lambda shp: pl.BlockSpec(shp, lambda i: (0,) * len(shp))
    hm_spec = pl.BlockSpec((H, tm, LANES), lambda i: (0, i, 0))
    hm_shape = jax.ShapeDtypeStruct((H, T, LANES), BF16)
    q, k, v = pl.pallas_call(
        _mla_qkv_kernel,
        grid=(T // tm,),
        in_specs=[pl.BlockSpec((tm, lat.shape[1]), lambda i: (i, 0)),
                  full((1, MLA_Q_LORA)), full((1, MLA_KV_LORA)),
                  full(wq.shape), full(wqr.shape), full(wk.shape), full(wv.shape),
                  pl.BlockSpec((tm, LANES), lambda i: (i % nS, 0)),
                  pl.BlockSpec((tm, LANES), lambda i: (i % nS, 0))],
        out_specs=[hm_spec, hm_spec, pl.BlockSpec((H, MLA_VT_ROWS, tm), lambda i: (0, 0, i))],
        out_shape=[hm_shape, hm_shape, jax.ShapeDtypeStruct((H, MLA_VT_ROWS, T), BF16)],
        compiler_params=_cparams("parallel"),
    )(lat, q_norm.reshape(1, -1), kv_norm.reshape(1, -1), wq, wqr, wk, wv, cos, sin)

    q4, k4 = (t.reshape(H, B, S, LANES) for t in (q, k))
    assert tq == tk and H % hp == 0
    o = pl.pallas_call(
        functools.partial(_mla_attn_kernel, tq=tq, tk=tk),
        grid=(B, H // hp, S // tq),
        in_specs=[pl.BlockSpec((hp, None, tq, LANES), lambda b, h, qi: (h, b, qi, 0)),
                  pl.BlockSpec((hp, None, S, LANES), lambda b, h, qi: (h, b, 0, 0), pipeline_mode=pl.Buffered(1)),
                  pl.BlockSpec((hp, MLA_VT_ROWS, S), lambda b, h, qi: (h, 0, b), pipeline_mode=pl.Buffered(1))],
        out_specs=pl.BlockSpec((hp, None, tq, LANES), lambda b, h, qi: (h, b, qi, 0)),
        out_shape=jax.ShapeDtypeStruct((H, B, S, LANES), BF16),
        scratch_shapes=[pltpu.VMEM((hp, 1, tq), F32), pltpu.VMEM((hp, MLA_VT_ROWS, tq), F32),
                        pltpu.VMEM((hp, tk, tq), F32), pltpu.VMEM((hp, tk, tq), F32)],
        compiler_params=_cparams("parallel", "parallel", "arbitrary"),
    )(q4, k4, v)

    w_o = jnp.pad(w_out.reshape(H, MLA_V, D), ((0, 0), (0, LANES - MLA_V), (0, 0))).reshape(H * LANES, D)
    return _linear_res(o.reshape(H, T, LANES), w_o.astype(BF16), x, head_major=True)


def _ret_kernel(q_ref, k_ref, v_ref, g_ref, cos_ref, sin_ref, gnw_ref, y_ref, state_scr, *, C):
    @pl.when(pl.program_id(1) == 0)
    def _():
        state_scr[...] = jnp.zeros(state_scr.shape, F32)

    half = RET_QK // 2
    cos = cos_ref[...]
    sin = sin_ref[...]
    ii = lax.broadcasted_iota(jnp.int32, (C, C), 0)
    jj = lax.broadcasted_iota(jnp.int32, (C, C), 1)
    diff = (ii - jj).astype(F32)
    causal = ii >= jj
    idx = lax.broadcasted_iota(jnp.int32, (C, 1), 0).astype(F32)

    def rope(t):
        t1, t2 = t[:, :half], t[:, half:]
        return jnp.concatenate([t1 * cos - t2 * sin, t1 * sin + t2 * cos], axis=1)

    for h in range(RET_HEADS):
        lg = math.log1p(-(2.0 ** (-5.0 - h)))
        intra = jnp.exp(jnp.where(causal, lg * diff, -jnp.inf))
        q_dec = jnp.exp(lg * (idx + 1.0))
        k_dec = jnp.exp(lg * (C - 1.0 - idx))
        c_dec = math.exp(lg * C)
        q = rope(q_ref[:, h * RET_QK:(h + 1) * RET_QK].astype(F32))
        k = rope(k_ref[:, h * RET_QK:(h + 1) * RET_QK].astype(F32)) * (RET_QK ** -0.5)
        v = v_ref[:, h * RET_V:(h + 1) * RET_V]
        st = state_scr[h]
        a = _dot_nt(q.astype(BF16), k.astype(BF16)) * intra
        o = _dot(a.astype(BF16), v) + _dot((q * q_dec).astype(BF16), st.astype(BF16))
        state_scr[h] = st * c_dec + _dot((k * k_dec).T.astype(BF16), v)
        mu = jnp.mean(o, axis=-1, keepdims=True)
        oc = o - mu
        var = jnp.mean(oc * oc, axis=-1, keepdims=True)
        on = oc * lax.rsqrt(var + EPS) * gnw_ref[:, h * RET_V:(h + 1) * RET_V]
        gt = g_ref[:, h * RET_V:(h + 1) * RET_V].astype(F32)
        y_ref[:, h * RET_V:(h + 1) * RET_V] = (gt * jax.nn.sigmoid(gt) * on).astype(y_ref.dtype)


def _retention(x, g, w_in, gn_w, w_out, B, S, C=128):
    T, D = x.shape
    nqk, nv = RET_HEADS * RET_QK, RET_HEADS * RET_V
    proj = _norm_linear(x, g, w_in.astype(BF16), BF16)
    half = RET_QK // 2
    ang = jnp.arange(S, dtype=F32)[:, None] * (RET_THETA ** (-jnp.arange(half, dtype=F32) / half))
    cos, sin = jnp.cos(ang), jnp.sin(ang)
    nc = S // C
    y = pl.pallas_call(
        functools.partial(_ret_kernel, C=C),
        grid=(B, nc),
        in_specs=[pl.BlockSpec((C, nqk), lambda b, c: (b * nc + c, 0)),
                  pl.BlockSpec((C, nqk), lambda b, c: (b * nc + c, 1)),
                  pl.BlockSpec((C, nv), lambda b, c: (b * nc + c, (2 * nqk) // nv)),
                  pl.BlockSpec((C, nv), lambda b, c: (b * nc + c, (2 * nqk) // nv + 1)),
                  pl.BlockSpec((C, RET_QK // 2), lambda b, c: (c, 0)),
                  pl.BlockSpec((C, RET_QK // 2), lambda b, c: (c, 0)),
                  pl.BlockSpec((1, nv), lambda b, c: (0, 0))],
        out_specs=pl.BlockSpec((C, nv), lambda b, c: (b * nc + c, 0)),
        out_shape=jax.ShapeDtypeStruct((T, nv), BF16),
        scratch_shapes=[pltpu.VMEM((RET_HEADS, RET_QK, RET_V), F32)],
        compiler_params=_cparams("parallel", "arbitrary"),
    )(proj, proj, proj, proj, cos, sin, gn_w.reshape(1, nv))
    return _linear_res(y, w_out.astype(BF16), x)


def _split3(a):
    a1 = a.astype(BF16)
    r = a - a1.astype(F32)
    a2 = r.astype(BF16)
    a3 = (r - a2.astype(F32)).astype(BF16)
    return a1, a2, a3


SSD_COLB = 1024


def _ssd_kernel(*refs, C):
    nz, nx = SSD_D_INNER // SSD_COLB, SSD_CONV_DIM // SSD_COLB
    z_refs, x_refs = refs[:nz], refs[nz:nz + nx]
    (dt_ref, cw_ref, cb_ref, dtb_ref, alog_ref, dsk_ref, nw_ref, y_ref,
     xe_scr, act_scr, st_scr, y_scr) = refs[nz + nx:]
    G, N, P = SSD_GROUPS, SSD_STATE, SSD_P
    DI = SSD_D_INNER
    pairs_per_group = (SSD_HEADS // G) // 2
    tiles_per_colb = SSD_COLB // LANES

    HALO = 8

    @pl.when(pl.program_id(1) == 0)
    def _():
        xe_scr[0:HALO, :] = jnp.zeros((HALO, xe_scr.shape[1]), F32)
        st_scr[...] = jnp.zeros(st_scr.shape, F32)

    for kx in range(nx):
        xe_scr[HALO:HALO + C, kx * SSD_COLB:(kx + 1) * SSD_COLB] = x_refs[kx][...].astype(F32)
    CB = 512
    for cb in range(SSD_CONV_DIM // CB):
        cs = slice(cb * CB, (cb + 1) * CB)
        acc = jnp.broadcast_to(cb_ref[:, cs], (C, CB))
        for kk in range(SSD_CONV):
            off = HALO - (SSD_CONV - 1) + kk
            acc = acc + cw_ref[kk:kk + 1, cs] * xe_scr[off:off + C, cs]
        act_scr[:, cs] = (acc * jax.nn.sigmoid(acc)).astype(BF16)
    xe_scr[0:HALO, :] = xe_scr[C:C + HALO, :]

    dtr = dt_ref[...] + dtb_ref[...]
    dt = jnp.maximum(dtr, 0.0) + jnp.log1p(jnp.exp(-jnp.abs(dtr)))
    a = dt * (-jnp.exp(alog_ref[...]))
    ii = lax.broadcasted_iota(jnp.int32, (C, C), 0)
    jj = lax.broadcasted_iota(jnp.int32, (C, C), 1)
    causal = ii >= jj
    tril = jnp.where(causal, 1.0, 0.0).astype(BF16)
    a1, a2, a3 = _split3(a)
    a_cs = _dot(tril, a1) + _dot(tril, a2) + _dot(tril, a3)
    a_end = a_cs[C - 1:C, :]
    w_end = jnp.exp(a_end - a_cs) * dt
    ea = jnp.exp(a_cs)
    eend = jnp.exp(a_end)
    a_cs_t = a_cs.T
    dt_t = dt.T
    lane = lax.broadcasted_iota(jnp.int32, (C, LANES), 1)
    lo = lane < P

    def pair(t, ha):
        return jnp.where(lo[:t.shape[0]], t[:, ha:ha + 1], t[:, ha + 1:ha + 2])

    for g in range(G):
        bm = act_scr[:, DI + g * N:DI + (g + 1) * N]
        cm = act_scr[:, DI + G * N + g * N:DI + G * N + (g + 1) * N]
        cbm = _dot_nt(cm, bm)
        bm_t = bm.astype(F32).T.astype(BF16)
        ssq = jnp.zeros((C, 1), F32)
        for pj in range(pairs_per_group):
            j = g * pairs_per_group + pj
            ha = 2 * j
            cs = slice(j * LANES, (j + 1) * LANES)
            xp = act_scr[:, cs]
            ys = []
            for hh in (ha, ha + 1):
                seg = a_cs[:, hh:hh + 1] - a_cs_t[hh:hh + 1, :]
                dec = jnp.exp(jnp.where(causal, seg, -jnp.inf))
                mm = cbm * dec * dt_t[hh:hh + 1, :]
                ys.append(_dot(mm.astype(BF16), xp))
            y = jnp.where(lo, ys[0], ys[1])
            st = st_scr[:, cs]
            y = y + pair(ea, ha) * _dot(cm, st.astype(BF16))
            xpf = xp.astype(F32)
            xw = (xpf * pair(w_end, ha)).astype(BF16)
            st_scr[:, cs] = st * pair(eend, ha) + _dot(bm_t, xw)
            y = y + pair(dsk_ref[...], ha) * xpf
            jz = j % tiles_per_colb
            zt = z_refs[j // tiles_per_colb][:, jz * LANES:(jz + 1) * LANES].astype(F32)
            y = y * (zt * jax.nn.sigmoid(zt))
            ssq = ssq + jnp.sum(y * y, axis=-1, keepdims=True)
            y_scr[:, cs] = y
        gw = pairs_per_group * LANES
        gs = slice(g * gw, (g + 1) * gw)
        rs = lax.rsqrt(ssq / float(gw) + EPS)
        y_ref[:, gs] = (y_scr[:, gs] * rs * nw_ref[:, gs]).astype(y_ref.dtype)


def _pad_lanes(v):
    return jnp.pad(v.astype(F32), (0, LANES - v.shape[0])).reshape(1, LANES)


def _ssd(x, g, w_in, conv_w, conv_b, dt_bias, A_log, D_skip, norm_w, w_out, B, S, C=256):
    T, D = x.shape
    DI, CD = SSD_D_INNER, SSD_CONV_DIM
    w_main = w_in[:, :DI + CD].astype(BF16)
    w_dt = jnp.pad(w_in[:, DI + CD:], ((0, 0), (0, LANES - SSD_HEADS))).astype(BF16)
    proj = _norm_linear(x, g, w_main, BF16)
    dt = _norm_linear(x, g, w_dt, F32)
    nc = S // C
    full = lambda shp: pl.BlockSpec(shp, lambda b, c: (0,) * len(shp))
    ncolb = (DI + CD) // SSD_COLB

    def colb(kc):
        return pl.BlockSpec((C, SSD_COLB), lambda b, c: (b * nc + c, kc))

    y = pl.pallas_call(
        functools.partial(_ssd_kernel, C=C),
        grid=(B, nc),
        in_specs=[colb(kc) for kc in range(ncolb)] + [
                  pl.BlockSpec((C, LANES), lambda b, c: (b * nc + c, 0)),
                  full((SSD_CONV, CD)), full((1, CD)), full((1, LANES)), full((1, LANES)), full((1, LANES)),
                  full((1, DI))],
        out_specs=pl.BlockSpec((C, DI), lambda b, c: (b * nc + c, 0)),
        out_shape=jax.ShapeDtypeStruct((T, DI), BF16),
        scratch_shapes=[pltpu.VMEM((C + 8, CD), F32), pltpu.VMEM((C, CD), BF16),
                        pltpu.VMEM((SSD_STATE, DI), F32), pltpu.VMEM((C, DI), F32)],
        compiler_params=_cparams("parallel", "arbitrary"),
    )(*([proj] * ncolb), dt, conv_w, conv_b.reshape(1, CD), _pad_lanes(dt_bias), _pad_lanes(A_log),
      _pad_lanes(D_skip), norm_w.reshape(1, DI))
    return _linear_res(y, w_out.astype(BF16), x)


ROW_TILE = 8


def _rows_to_tiles(ref, val):
    n = val.shape[0]
    for s in range(ROW_TILE):
        ref[pl.ds(s, n, stride=ROW_TILE), :] = val[:, s * LANES:(s + 1) * LANES]


def _tiles_to_rows(ref, n):
    return jnp.concatenate([ref[pl.ds(s, n, stride=ROW_TILE), :] for s in range(ROW_TILE)], axis=1)


def _router_kernel(x_ref, g_ref, w1_ref, w2_ref, b_ref, h_ref, route_ref, cnt_ref, dense_ref, carry_scr):
    E = MOE_EXPERTS

    @pl.when(pl.program_id(0) == 0)
    def _():
        carry_scr[...] = jnp.zeros(carry_scr.shape, F32)

    h = _rms(x_ref[...], g_ref[...])
    _rows_to_tiles(h_ref, h)
    tm = h.shape[0]
    h1 = h.astype(BF16)
    h2 = (h - h1.astype(F32)).astype(BF16)
    logits = _dot(h1, w1_ref[...]) + _dot(h1, w2_ref[...]) + _dot(h2, w1_ref[...]) + b_ref[...]
    lane = lax.broadcasted_iota(jnp.int32, (tm, LANES), 1)
    lanef = lane.astype(F32)
    big = float(LANES)
    glog = jnp.where((lane >= E) & (lane < E + MOE_GROUPS), logits, -jnp.inf)
    gmax = jnp.max(glog, axis=-1, keepdims=True)
    gval = 1.0 / jnp.sum(jnp.exp(glog - gmax), axis=-1, keepdims=True)
    gidx = jnp.min(jnp.where(glog == gmax, lanef, big), axis=-1, keepdims=True) - float(E)
    in_group = (lane < E) & ((lane // MOE_EPG).astype(F32) == gidx)
    el = jnp.where(in_group, logits, -jnp.inf)
    v1 = jnp.max(el, axis=-1, keepdims=True)
    i1 = jnp.min(jnp.where(el == v1, lanef, big), axis=-1, keepdims=True)
    el2 = jnp.where(lanef == i1, -jnp.inf, el)
    v2 = jnp.max(el2, axis=-1, keepdims=True)
    i2 = jnp.min(jnp.where(el2 == v2, lanef, big), axis=-1, keepdims=True)
    e2 = jnp.exp(v2 - v1)
    wa = gval / (1.0 + e2)
    wb = gval * e2 / (1.0 + e2)
    oh1 = jnp.where(lanef == i1, 1.0, 0.0)
    oh2 = jnp.where(lanef == i2, 1.0, 0.0)
    ohs = oh1 + oh2
    ii = lax.broadcasted_iota(jnp.int32, (tm, tm), 0)
    jj = lax.broadcasted_iota(jnp.int32, (tm, tm), 1)
    stril = jnp.where(ii > jj, 1.0, 0.0).astype(BF16)
    before = _dot(stril, ohs.astype(BF16)) + carry_scr[0:1, :]
    r1 = jnp.sum(before * oh1, axis=-1, keepdims=True)
    r2 = jnp.sum(before * oh2, axis=-1, keepdims=True)
    carry = carry_scr[0:1, :] + jnp.sum(ohs, axis=0, keepdims=True)
    carry_scr[...] = jnp.broadcast_to(carry, carry_scr.shape)
    cnt_ref[...] = jnp.broadcast_to(carry, cnt_ref.shape)
    route = jnp.zeros((tm, LANES), F32)
    for ln, val in enumerate((i1, i2, wa, wb, r1, r2)):
        route = jnp.where(lane == ln, val, route)
    route_ref[...] = route
    eye = (lax.broadcasted_iota(jnp.int32, (LANES, LANES), 0) == lax.broadcasted_iota(jnp.int32, (LANES, LANES), 1))
    rows = []
    for val in (i1, i2, r1, r2):
        vb = jnp.broadcast_to(val, (tm, LANES))
        for r in range(tm // LANES):
            rows.append(jnp.sum(jnp.where(eye, vb[r * LANES:(r + 1) * LANES], 0.0), axis=0, keepdims=True))
    dense_ref[...] = jnp.concatenate(rows, axis=0)


def _dispatch_kernel(pos_ref, lo_ref, hi_ref, h_ref, xg_hbm, zero_scr, sem, zsem):
    ntok = pos_ref.shape[0] // MOE_TOPK
    base = pl.program_id(0) * TOK_TILE

    @pl.when(pl.program_id(0) == 0)
    def _():
        zero_scr[...] = jnp.zeros(zero_scr.shape, F32)
        blk = zero_scr.shape[0]

        def zero_copy(r, n):
            return pltpu.make_async_copy(zero_scr.at[pl.ds(0, n)], xg_hbm.at[pl.ds(pl.multiple_of(r * n, n), n)], zsem)

        def loop(lo, hi, n, wait):
            def body(r, c):
                cp = zero_copy(r, n)
                cp.wait() if wait else cp.start()
                return c
            lax.fori_loop(lo, hi, body, 0)

        tail_lo = hi_ref[MOE_EXPERTS - 1] * ROW_TILE // blk
        tail_hi = xg_hbm.shape[0] // blk
        for wait in (False, True):
            for e in range(MOE_EXPERTS):
                loop(lo_ref[e], hi_ref[e], ROW_TILE, wait)
            loop(tail_lo, tail_hi, blk, wait)

    def copy(t, k):
        row = pl.multiple_of(pos_ref[k * ntok + base + t] * ROW_TILE, ROW_TILE)
        return pltpu.make_async_copy(h_ref.at[pl.ds(t * ROW_TILE, ROW_TILE)], xg_hbm.at[pl.ds(row, ROW_TILE)], sem)

    for t in range(TOK_TILE):
        for k in range(MOE_TOPK):
            copy(t, k).start(priority=k)
    same_size = pltpu.make_async_copy(h_ref.at[pl.ds(0, ROW_TILE)], xg_hbm.at[pl.ds(0, ROW_TILE)], sem)
    for _ in range(TOK_TILE * MOE_TOPK):
        same_size.wait()


def _expert_kernel(be_ref, nu_ref, x_ref, wg_ref, wu_ref, wd_ref, o_ref, wg_scr, wu_scr, wd_scr):
    i = pl.program_id(0)
    used = i < nu_ref[0]
    n = x_ref.shape[0] // ROW_TILE

    @pl.when(used & ((i == 0) | (be_ref[i] != be_ref[jnp.maximum(i - 1, 0)])))
    def _():
        wg_scr[...] = wg_ref[...].astype(BF16)
        wu_scr[...] = wu_ref[...].astype(BF16)
        wd_scr[...] = wd_ref[...].astype(BF16)

    @pl.when(used)
    def _():
        xb = _tiles_to_rows(x_ref, n).astype(BF16)
        hg = _dot(xb, wg_scr[...])
        hu = _dot(xb, wu_scr[...])
        _rows_to_tiles(o_ref, _dot((hg * jax.nn.sigmoid(hg) * hu).astype(BF16), wd_scr[...]))

    @pl.when(jnp.logical_not(used))
    def _():
        o_ref[...] = jnp.zeros(o_ref.shape, F32)


def _ple_kernel(pos_ref, x_ref, rt_ref, p_ref, g_ref, gf_ref, wg_ref, wp_ref, yb_hbm, o_ref,
                buf_a, buf_b, sems, *, final):
    tp = x_ref.shape[0]
    i = pl.program_id(0)
    last = pl.num_programs(0) - 1
    slot = i % 2
    bufs = (buf_a, buf_b)

    def copy(tile, sl, t, k):
        row = pl.multiple_of(pos_ref[k * (pos_ref.shape[0] // MOE_TOPK) + tile * tp + t] * ROW_TILE, ROW_TILE)
        dst = bufs[k].at[sl, pl.ds(pl.multiple_of(t * ROW_TILE, ROW_TILE), ROW_TILE)]
        return pltpu.make_async_copy(yb_hbm.at[pl.ds(row, ROW_TILE)], dst, sems.at[sl])

    def wait_slot(sl):
        for k in range(MOE_TOPK):
            same_size = pltpu.make_async_copy(yb_hbm.at[pl.ds(0, ROW_TILE)], bufs[k].at[sl, pl.ds(0, ROW_TILE)],
                                              sems.at[sl])
            for _ in range(tp):
                same_size.wait()

    @pl.when(i == 0)
    def _():
        def first(t, c):
            for k in range(MOE_TOPK):
                copy(0, 0, t, k).start(priority=k)
            return c
        lax.fori_loop(0, tp, first, 0, unroll=8)

    wait_slot(slot)
    nxt = jnp.minimum(i + 1, last)
    for t in range(tp):
        for k in range(MOE_TOPK):
            copy(nxt, 1 - slot, t, k).start(priority=k)

    ple = _dot(p_ref[...].astype(BF16), wp_ref[...])
    rt = rt_ref[...]
    ya = _tiles_to_rows(buf_a.at[slot], tp)
    yb = _tiles_to_rows(buf_b.at[slot], tp)
    x1 = x_ref[...] + rt[:, 2:3] * ya + rt[:, 3:4] * yb
    gate = jax.nn.sigmoid(_dot(_rms(x1, g_ref[...]).astype(BF16), wg_ref[...]))
    out = x1 + ple * gate
    if final:
        out = _rms(out, gf_ref[...])
    o_ref[...] = out

    @pl.when(i == last)
    def _():
        wait_slot(1 - slot)


def _moe_ple(x, layer, p_all, norm_moe, w_group, b_group, w_expert, b_expert, w_gate, w_up, w_down,
             norm_ple, ple_w, ple_w_gate, norm_final, final, tm=512):
    T, D = x.shape
    E, BLK = MOE_EXPERTS, MOE_BLOCK
    A = T * MOE_TOPK
    nb = A // BLK + E
    P = nb * BLK

    w_r = jnp.zeros((D, LANES), F32).at[:, :E].set(w_expert).at[:, E:E + MOE_GROUPS].set(w_group)
    b_r = jnp.zeros((1, LANES), F32).at[0, :E].set(b_expert).at[0, E:E + MOE_GROUPS].set(b_group)
    w_r1 = w_r.astype(BF16)
    w_r2 = (w_r - w_r1.astype(F32)).astype(BF16)
    full = lambda shp: pl.BlockSpec(shp, lambda i: (0,) * len(shp))
    nsub = tm // LANES
    h, route, cnt, dense = pl.pallas_call(
        _router_kernel,
        grid=(T // tm,),
        in_specs=[pl.BlockSpec((tm, D), lambda i: (i, 0)), full((1, D)), full((D, LANES)), full((D, LANES)),
                  full((1, LANES))],
        out_specs=[pl.BlockSpec((tm * ROW_TILE, LANES), lambda i: (i, 0)),
                   pl.BlockSpec((tm, LANES), lambda i: (i, 0)), full((8, LANES)),
                   pl.BlockSpec((2 * MOE_TOPK * nsub, LANES), lambda i: (i, 0))],
        out_shape=[jax.ShapeDtypeStruct((T * ROW_TILE, LANES), F32), jax.ShapeDtypeStruct((T, LANES), F32),
                   jax.ShapeDtypeStruct((8, LANES), F32),
                   jax.ShapeDtypeStruct((T // tm * 2 * MOE_TOPK * nsub, LANES), F32)],
        scratch_shapes=[pltpu.VMEM((8, LANES), F32)],
        compiler_params=_cparams("arbitrary"),
    )(x, norm_moe.reshape(1, D), w_r1, w_r2, b_r)

    counts = cnt[0, :E].astype(jnp.int32)
    padded = (counts + BLK - 1) // BLK * BLK
    pad_end = jnp.cumsum(padded)
    pad_start = pad_end - padded
    dense = dense.reshape(T // tm, 2 * MOE_TOPK, nsub * LANES).astype(jnp.int32)
    def slot(k):
        e, acc = dense[:, k], dense[:, MOE_TOPK + k]
        for j in range(E):
            acc = acc + jnp.where(e == j, pad_start[j], 0)
        return acc.reshape(T)

    pos = jnp.concatenate([slot(k) for k in range(MOE_TOPK)])
    blk_start = jnp.arange(nb, dtype=jnp.int32) * BLK
    blk_expert = jnp.minimum(jnp.sum((pad_end[None, :] <= blk_start[:, None]).astype(jnp.int32), axis=1), E - 1)
    n_used = (pad_end[-1:] // BLK).astype(jnp.int32)

    assert D == ROW_TILE * LANES
    xg = pl.pallas_call(
        _dispatch_kernel,
        grid_spec=pltpu.PrefetchScalarGridSpec(
            num_scalar_prefetch=3, grid=(T // TOK_TILE,),
            in_specs=[pl.BlockSpec((TOK_TILE * ROW_TILE, LANES), lambda i, ps, lo, hi: (i, 0))],
            out_specs=pl.BlockSpec(memory_space=pl.ANY),
            scratch_shapes=[pltpu.VMEM((BLK * ROW_TILE, LANES), F32), pltpu.SemaphoreType.DMA,
                            pltpu.SemaphoreType.DMA]),
        out_shape=jax.ShapeDtypeStruct((P * ROW_TILE, LANES), F32),
        compiler_params=_cparams("arbitrary"),
    )(pos, pad_start + counts, pad_end, h)

    def wmap(i, be, nu):
        return (layer, be[jnp.minimum(i, nu[0] - 1)], 0, 0)

    def xmap(i, be, nu):
        return (jnp.minimum(i, nu[0] - 1), 0)

    yb = pl.pallas_call(
        _expert_kernel,
        grid_spec=pltpu.PrefetchScalarGridSpec(
            num_scalar_prefetch=2, grid=(nb,),
            in_specs=[pl.BlockSpec((BLK * ROW_TILE, LANES), xmap),
                      pl.BlockSpec((None, None, D, MOE_FF), wmap), pl.BlockSpec((None, None, D, MOE_FF), wmap),
                      pl.BlockSpec((None, None, MOE_FF, D), wmap)],
            out_specs=pl.BlockSpec((BLK * ROW_TILE, LANES), lambda i, be, nu: (i, 0)),
            scratch_shapes=[pltpu.VMEM((D, MOE_FF), BF16), pltpu.VMEM((D, MOE_FF), BF16),
                            pltpu.VMEM((MOE_FF, D), BF16)]),
        out_shape=jax.ShapeDtypeStruct((P * ROW_TILE, LANES), F32),
        compiler_params=_cparams("arbitrary"),
    )(blk_expert, n_used, xg, w_gate, w_up, w_down)

    tp = TOK_TILE
    PD = p_all.shape[-1]
    return pl.pallas_call(
        functools.partial(_ple_kernel, final=final),
        grid_spec=pltpu.PrefetchScalarGridSpec(
            num_scalar_prefetch=1, grid=(T // tp,),
            in_specs=[pl.BlockSpec((tp, D), lambda i, ps: (i, 0)),
                      pl.BlockSpec((tp, LANES), lambda i, ps: (i, 0)),
                      pl.BlockSpec((None, tp, PD), lambda i, ps: (layer, i, 0)),
                      pl.BlockSpec((1, D), lambda i, ps: (0, 0)), pl.BlockSpec((1, D), lambda i, ps: (0, 0)),
                      pl.BlockSpec((D, D), lambda i, ps: (0, 0)), pl.BlockSpec((PD, D), lambda i, ps: (0, 0)),
                      pl.BlockSpec(memory_space=pl.ANY)],
            out_specs=pl.BlockSpec((tp, D), lambda i, ps: (i, 0)),
            scratch_shapes=[pltpu.VMEM((2, tp * ROW_TILE, LANES), F32), pltpu.VMEM((2, tp * ROW_TILE, LANES), F32),
                            pltpu.SemaphoreType.DMA((2,))]),
        out_shape=jax.ShapeDtypeStruct((T, D), F32),
        compiler_params=_cparams("arbitrary"),
    )(pos, x, route, p_all, norm_ple.reshape(1, D), norm_final.reshape(1, D),
      ple_w_gate.astype(BF16), ple_w.astype(BF16), yb)


def kernel(x, p, norm_mix, norm_moe, norm_ple, norm_final, dsa_w_in, dsa_w_out, mla_w_in, mla_q_norm, mla_w_qb,
           mla_kv_norm, mla_w_kvb, mla_w_out, ret_w_in, ret_gn_w, ret_w_out, ssd_w_in, ssd_conv_w, ssd_conv_b,
           ssd_dt_bias, ssd_A_log, ssd_D, ssd_norm_w, ssd_w_out, moe_w_group, moe_b_group, moe_w_expert,
           moe_b_expert, moe_w_gate, moe_w_up, moe_w_down, ple_w, ple_w_gate):
    B, S, D = x.shape
    depth = p.shape[0]
    xt = x.reshape(B * S, D)
    for i in range(depth):
        kind = i % 4
        if kind == 0:
            xt = _dilated_attention(xt, norm_mix[i], dsa_w_in, dsa_w_out, B, S)
        elif kind == 1:
            xt = _mla(xt, norm_mix[i], mla_w_in, mla_q_norm, mla_w_qb, mla_kv_norm, mla_w_kvb, mla_w_out, B, S)
        elif kind == 2:
            xt = _retention(xt, norm_mix[i], ret_w_in, ret_gn_w, ret_w_out, B, S)
        else:
            xt = _ssd(xt, norm_mix[i], ssd_w_in, ssd_conv_w, ssd_conv_b, ssd_dt_bias, ssd_A_log, ssd_D,
                      ssd_norm_w, ssd_w_out, B, S)
        xt = _moe_ple(xt, i, p.reshape(depth, B * S, -1), norm_moe[i], moe_w_group[i], moe_b_group[i],
                      moe_w_expert[i], moe_b_expert[i], moe_w_gate, moe_w_up, moe_w_down, norm_ple[i], ple_w[i],
                      ple_w_gate[i], norm_final, final=(i == depth - 1))
    return xt.reshape(B, S, D)
```

```python
import functools
import math

import jax
import jax.numpy as jnp
from jax import lax
from jax.experimental import pallas as pl
from jax.experimental.pallas import tpu as pltpu

F32 = jnp.float32
BF16 = jnp.bfloat16
EPS = 1e-6
LANES = 128
VMEM_LIMIT = 48 * 1024 * 1024

DSA_PATTERNS = ((128, 1), (512, 4), (2048, 16))
DSA_HEADS = 8
DSA_SPAN = 2048
MLA_HEADS = 16
MLA_Q_LORA, MLA_KV_LORA, MLA_NOPE, MLA_ROPE, MLA_V = 256, 128, 64, 32, 64
MLA_THETA = 10000.0
MLA_VT_ROWS = 80
RET_HEADS, RET_QK, RET_V = 4, 256, 512
RET_THETA = 10000.0
SSD_D_INNER, SSD_HEADS, SSD_GROUPS, SSD_STATE, SSD_CONV, SSD_P = 2048, 32, 4, 128, 4, 64
SSD_CONV_DIM = SSD_D_INNER + 2 * SSD_GROUPS * SSD_STATE
MOE_GROUPS, MOE_EPG, MOE_EXPERTS, MOE_TOPK, MOE_FF = 4, 8, 32, 2, 512
MOE_BLOCK = 256
TOK_TILE = 256


def _cparams(*sem):
    return pltpu.CompilerParams(dimension_semantics=sem, vmem_limit_bytes=VMEM_LIMIT)


def _rms(x, g):
    return x * lax.rsqrt(jnp.mean(x * x, axis=-1, keepdims=True) + EPS) * g


def _dot(a, b):
    return jnp.dot(a, b, preferred_element_type=F32)


def _dot_nt(a, b):
    return lax.dot_general(a, b, (((1,), (1,)), ((), ())), preferred_element_type=F32)


def _norm_linear_kernel(x_ref, g_ref, w_ref, o_ref, h_scr, *, head_major):
    @pl.when(pl.program_id(1) == 0)
    def _():
        h_scr[...] = _rms(x_ref[...], g_ref[...]).astype(BF16)

    r = _dot(h_scr[...], w_ref[...])
    if head_major:
        for c in range(r.shape[1] // LANES):
            o_ref[c] = r[:, c * LANES:(c + 1) * LANES].astype(o_ref.dtype)
    else:
        o_ref[...] = r.astype(o_ref.dtype)


def _norm_linear(x, g, w, out_dtype, *, head_major=False, tm=1024, tn=1024):
    T, D = x.shape
    N = w.shape[1]
    tn = min(tn, N)
    assert T % tm == 0 and N % tn == 0 and tn % LANES == 0
    if head_major:
        out_shape = jax.ShapeDtypeStruct((N // LANES, T, LANES), out_dtype)
        out_spec = pl.BlockSpec((tn // LANES, tm, LANES), lambda i, j: (j, i, 0))
    else:
        out_shape = jax.ShapeDtypeStruct((T, N), out_dtype)
        out_spec = pl.BlockSpec((tm, tn), lambda i, j: (i, j))
    return pl.pallas_call(
        functools.partial(_norm_linear_kernel, head_major=head_major),
        grid=(T // tm, N // tn),
        in_specs=[pl.BlockSpec((tm, D), lambda i, j: (i, 0)),
                  pl.BlockSpec((1, D), lambda i, j: (0, 0)),
                  pl.BlockSpec((D, tn), lambda i, j: (0, j))],
        out_specs=out_spec,
        out_shape=out_shape,
        scratch_shapes=[pltpu.VMEM((tm, D), BF16)],
        compiler_params=_cparams("parallel", "arbitrary"),
    )(x, g.reshape(1, D), w)


def _linear_res_kernel(a_ref, w_ref, r_ref, o_ref, *, head_major):
    if head_major:
        a = jnp.concatenate([a_ref[c] for c in range(a_ref.shape[0])], axis=1)
    else:
        a = a_ref[...]
    o_ref[...] = r_ref[...] + _dot(a, w_ref[...])


def _linear_res(a, w, res, *, head_major=False, tm=512):
    T, D = res.shape
    K = w.shape[0]
    if head_major:
        a_spec = pl.BlockSpec((K // LANES, tm, LANES), lambda i: (0, i, 0))
    else:
        a_spec = pl.BlockSpec((tm, K), lambda i: (i, 0))
    return pl.pallas_call(
        functools.partial(_linear_res_kernel, head_major=head_major),
        grid=(T // tm,),
        in_specs=[a_spec,
                  pl.BlockSpec((K, D), lambda i: (0, 0)),
                  pl.BlockSpec((tm, D), lambda i: (i, 0))],
        out_specs=pl.BlockSpec((tm, D), lambda i: (i, 0)),
        out_shape=jax.ShapeDtypeStruct((T, D), F32),
        compiler_params=_cparams("parallel"),
    )(a, w, res)


def _dsa_group_kernel(sl_ref, q_ref, k_ref, kh_ref, v_ref, vh_ref, o_ref, sm_ref, sd_ref, *, dil, rows):
    W = LANES
    i = pl.program_id(1)
    h = pl.program_id(2)
    scale = float(W) ** -0.5
    slope = sl_ref[h] * float(dil)
    qi = lax.broadcasted_iota(jnp.int32, (W, 2 * W), 0)
    kj = lax.broadcasted_iota(jnp.int32, (W, 2 * W), 1)
    steps = W + qi - kj
    band = (steps >= 0) & (steps <= W)
    first_lim = jnp.where(i == 0, W, 0)
    band_first = band & (kj >= first_lim)
    bias = (-slope) * steps.astype(F32)
    lane = lax.broadcasted_iota(jnp.int32, (W, LANES), 1)

    @pl.when(h == 0)
    def _():
        sm_ref[...] = jnp.zeros(sm_ref.shape, F32)
        sd_ref[...] = jnp.ones(sd_ref.shape, F32)

    for r in range(dil):
        cs = slice(r * LANES, (r + 1) * LANES)
        for n in range(rows // W):
            rs = slice(n * W, (n + 1) * W)
            q = q_ref[rs, cs]
            if n == 0:
                kk = jnp.concatenate([kh_ref[:, cs], k_ref[rs, cs]], axis=0)
                vv = jnp.concatenate([vh_ref[:, cs], v_ref[rs, cs]], axis=0)
                valid = band_first
            else:
                kk = k_ref[(n - 1) * W:(n + 1) * W, cs]
                vv = v_ref[(n - 1) * W:(n + 1) * W, cs]
                valid = band
            s = _dot_nt(q, kk) * scale + bias
            s = jnp.where(valid, s, -jnp.inf)
            m = jnp.max(s, axis=-1, keepdims=True)
            e = jnp.exp(s - m)
            den = jnp.sum(e, axis=-1, keepdims=True)
            o = _dot(e.astype(BF16), vv) / den
            o_ref[rs, cs] = o.astype(o_ref.dtype)
            sm_ref[rs, cs] = jnp.where(lane == h, m, sm_ref[rs, cs])
            sd_ref[rs, cs] = jnp.where(lane == h, den, sd_ref[rs, cs])


def _dsa_proj_kernel(x_ref, g_ref, w_ref, o_ref, h_scr, r_scr, *, dil):
    @pl.when(pl.program_id(1) == 0)
    def _():
        h_scr[...] = _rms(x_ref[...], g_ref[...]).astype(BF16)

    res = _dot(h_scr[...], w_ref[...])
    rows = r_scr.shape[1] // dil
    for c in range(o_ref.shape[0]):
        if dil == 1:
            o_ref[c] = res[:, c * LANES:(c + 1) * LANES].astype(o_ref.dtype)
            continue
        r_scr[c] = res[:, c * LANES:(c + 1) * LANES]
        for r in range(dil):
            o_ref[c, :, r * LANES:(r + 1) * LANES] = r_scr[c, pl.ds(r, rows, stride=dil), :].astype(o_ref.dtype)


def _dsa_proj(x, g, w, dil, tm=1024):
    T, D = x.shape
    N = w.shape[1]
    tn = DSA_HEADS * LANES
    return pl.pallas_call(
        functools.partial(_dsa_proj_kernel, dil=dil),
        grid=(T // tm, N // tn),
        in_specs=[pl.BlockSpec((tm, D), lambda i, j: (i, 0)),
                  pl.BlockSpec((1, D), lambda i, j: (0, 0)),
                  pl.BlockSpec((D, tn), lambda i, j: (0, j))],
        out_specs=pl.BlockSpec((DSA_HEADS, tm // dil, dil * LANES), lambda i, j: (j, i, 0)),
        out_shape=jax.ShapeDtypeStruct((N // LANES, T // dil, dil * LANES), BF16),
        scratch_shapes=[pltpu.VMEM((tm, D), BF16), pltpu.VMEM((DSA_HEADS, tm, LANES), F32)],
        compiler_params=_cparams("parallel", "arbitrary"),
    )(x, g.reshape(1, D), w)


def _dsa_group(qkv, slopes, dil, B, S):
    H = DSA_HEADS
    L = S // dil
    rows = DSA_SPAN // dil
    width = dil * LANES
    qv = qkv.reshape(qkv.shape[0], B, L, width)
    hb = rows // LANES
    base = 0

    def qmap(o):
        return lambda b, i, h, sl: (base + o * H + h, b, i, 0)

    def hmap(o):
        return lambda b, i, h, sl: (base + o * H + h, b, jnp.maximum(i * hb - 1, 0), 0)

    blk = (None, None, rows, width)
    hblk = (None, None, LANES, width)
    o, sm, sd = pl.pallas_call(
        functools.partial(_dsa_group_kernel, dil=dil, rows=rows),
        grid_spec=pltpu.PrefetchScalarGridSpec(
            num_scalar_prefetch=1,
            grid=(B, L // rows, H),
            in_specs=[pl.BlockSpec(blk, qmap(0)),
                      pl.BlockSpec(blk, qmap(1)), pl.BlockSpec(hblk, hmap(1)),
                      pl.BlockSpec(blk, qmap(2)), pl.BlockSpec(hblk, hmap(2))],
            out_specs=[pl.BlockSpec(blk, lambda b, i, h, sl: (h, b, i, 0)),
                       pl.BlockSpec((None, rows, width), lambda b, i, h, sl: (b, i, 0)),
                       pl.BlockSpec((None, rows, width), lambda b, i, h, sl: (b, i, 0))]),
        out_shape=[jax.ShapeDtypeStruct((H, B, L, width), BF16),
                   jax.ShapeDtypeStruct((B, L, width), F32),
                   jax.ShapeDtypeStruct((B, L, width), F32)],
        compiler_params=_cparams("parallel", "parallel", "arbitrary"),
    )(slopes, qv, qv, qv, qv, qv)
    return o.reshape(H, B * L, width), sm.reshape(B * L, width), sd.reshape(B * L, width)


def _dsa_out_kernel(o0, o1, o2, m0, m1, m2, d0, d1, d2, w_ref, x_ref, out_ref, o_scr, s_scr):
    tm = x_ref.shape[0]
    dils = [d for _, d in DSA_PATTERNS]

    def token_major(ref, scr, dil):
        if dil == 1:
            return ref[...].astype(F32)
        for r in range(dil):
            scr[pl.ds(r, tm // dil, stride=dil), :] = ref[:, r * LANES:(r + 1) * LANES].astype(F32)
        return scr[...]

    ms = [token_major(m, s_scr.at[2 * g], dils[g]) for g, m in enumerate((m0, m1, m2))]
    dn = [token_major(d, s_scr.at[2 * g + 1], dils[g]) for g, d in enumerate((d0, d1, d2))]
    mx = jnp.maximum(jnp.maximum(ms[0], ms[1]), ms[2])
    ws = [d * jnp.exp(m - mx) for d, m in zip(dn, ms)]
    tot = ws[0] + ws[1] + ws[2]
    ws = [w / tot for w in ws]
    parts = []
    for h in range(DSA_HEADS):
        acc = None
        for g, o in enumerate((o0, o1, o2)):
            term = ws[g][:, h:h + 1] * token_major(o.at[h], o_scr.at[g, h], dils[g])
            acc = term if acc is None else acc + term
        parts.append(acc.astype(BF16))
    a = jnp.concatenate(parts, axis=1)
    out_ref[...] = x_ref[...] + _dot(a, w_ref[...])


def _dsa_out(os_, ms, ds, w, x, tm=512):
    T, D = x.shape
    H = DSA_HEADS
    ng = len(DSA_PATTERNS)
    ospecs = [pl.BlockSpec((H, tm // d, d * LANES), lambda i: (0, i, 0)) for _, d in DSA_PATTERNS]
    sspecs = [pl.BlockSpec((tm // d, d * LANES), lambda i: (i, 0)) for _, d in DSA_PATTERNS]
    return pl.pallas_call(
        _dsa_out_kernel,
        grid=(T // tm,),
        in_specs=ospecs + sspecs + sspecs + [pl.BlockSpec((H * LANES, D), lambda i: (0, 0)),
                                             pl.BlockSpec((tm, D), lambda i: (i, 0))],
        out_specs=pl.BlockSpec((tm, D), lambda i: (i, 0)),
        out_shape=jax.ShapeDtypeStruct((T, D), F32),
        scratch_shapes=[pltpu.VMEM((ng, H, tm, LANES), F32), pltpu.VMEM((2 * ng, tm, LANES), F32)],
        compiler_params=_cparams("parallel"),
    )(*os_, *ms, *ds, w, x)


def _dilated_attention(x, g, w_in, w_out, B, S):
    ng = len(DSA_PATTERNS)
    gw = 3 * DSA_HEADS * LANES
    slopes = 2.0 ** (-8.0 * jnp.arange(1, ng * DSA_HEADS + 1, dtype=F32) / (ng * DSA_HEADS))
    os_, ms, ds = [], [], []
    for gi, (window, dil) in enumerate(DSA_PATTERNS):
        assert window // dil == LANES
        qkv = _dsa_proj(x, g, w_in[:, gi * gw:(gi + 1) * gw].astype(BF16), dil)
        o, m, d = _dsa_group(qkv, slopes[gi * DSA_HEADS:(gi + 1) * DSA_HEADS], dil, B, S)
        os_.append(o)
        ms.append(m)
        ds.append(d)
    return _dsa_out(os_, ms, ds, w_out.astype(BF16), x)


def _mla_qkv_kernel(lat_ref, qn_ref, kvn_ref, wq_ref, wqr_ref, wk_ref, wv_ref, cos_ref, sin_ref,
                    q_ref, k_ref, vt_ref):
    lat = lat_ref[...]
    cos = cos_ref[...]
    sin = sin_ref[...]
    hq = _rms(lat[:, :MLA_Q_LORA], qn_ref[...]).astype(BF16)
    hkv = _rms(lat[:, MLA_Q_LORA:MLA_Q_LORA + MLA_KV_LORA], kvn_ref[...]).astype(BF16)
    o1 = MLA_Q_LORA + MLA_KV_LORA
    kr = lat[:, o1:o1 + LANES] * cos + lat[:, o1 + LANES:o1 + 2 * LANES] * sin
    q = _dot(hq, wq_ref[...])
    qr = _dot(hq, wqr_ref[...])
    k = _dot(hkv, wk_ref[...])
    v = _dot(hkv, wv_ref[...])
    lane = lax.broadcasted_iota(jnp.int32, cos.shape, 1)
    scale = float(MLA_NOPE + MLA_ROPE) ** -0.5 * math.log2(math.e)
    cos_q = jnp.where(lane < MLA_NOPE, 1.0, cos) * scale
    sin_q = sin * scale
    for h in range(MLA_HEADS):
        cs = slice(h * LANES, (h + 1) * LANES)
        q_ref[h] = (q[:, cs] * cos_q + qr[:, cs] * sin_q).astype(BF16)
        k_ref[h] = (k[:, cs] + kr).astype(BF16)
        vt_ref[h] = jnp.where(lane == MLA_V, 1.0, v[:, cs]).T[:MLA_VT_ROWS].astype(BF16)


def _mla_attn_kernel(q_ref, k_ref, vt_ref, o_ref, m_scr, acc_scr, sa_scr, sb_scr, *, tq, tk):
    qi = pl.program_id(2)
    heads = range(q_ref.shape[0])
    m_scr[...] = jnp.full(m_scr.shape, -jnp.inf, F32)
    acc_scr[...] = jnp.zeros(acc_scr.shape, F32)

    def scores(j, buf):
        for h in heads:
            buf[h] = _dot_nt(k_ref[h, pl.ds(pl.multiple_of(j * tk, tk), tk), :], q_ref[h])

    def update(j, buf, masked):
        off = pl.multiple_of(j * tk, tk)
        for h in heads:
            st = buf[h]
            if masked:
                kpos = off + lax.broadcasted_iota(jnp.int32, (tk, tq), 0)
                qpos = qi * tq + lax.broadcasted_iota(jnp.int32, (tk, tq), 1)
                st = jnp.where(kpos <= qpos, st, -jnp.inf)
            m_prev = m_scr[h]
            m_new = jnp.maximum(m_prev, jnp.max(st, axis=0, keepdims=True))
            p = jnp.exp2(st - m_new).astype(BF16)
            acc_scr[h] = jnp.exp2(m_prev - m_new) * acc_scr[h] + _dot(vt_ref[h, :, pl.ds(off, tk)], p)
            m_scr[h] = m_new

    scores(0, sa_scr)

    def body(jj, c):
        j = 2 * jj
        scores(j + 1, sb_scr)
        update(j, sa_scr, False)
        scores(j + 2, sa_scr)
        update(j + 1, sb_scr, False)
        return c

    lax.fori_loop(0, qi // 2, body, 0)

    @pl.when(qi % 2 == 0)
    def _():
        update(qi, sa_scr, True)

    @pl.when(qi % 2 == 1)
    def _():
        scores(qi, sb_scr)
        update(qi - 1, sa_scr, False)
        update(qi, sb_scr, True)

    for h in heads:
        acc = acc_scr[h]
        acc = jnp.concatenate([acc, jnp.zeros((LANES - MLA_VT_ROWS, tq), F32)], axis=0).T
        o_ref[h] = (acc / acc[:, MLA_V:MLA_V + 1]).astype(o_ref.dtype)


def _rope_tables(S, half, theta, lane_off):
    inv = theta ** (-jnp.arange(half, dtype=F32) / half)
    ang = jnp.arange(S, dtype=F32)[:, None] * inv
    cos = jnp.zeros((S, LANES), F32).at[:, lane_off:lane_off + 2 * half].set(jnp.tile(jnp.cos(ang), (1, 2)))
    sin = jnp.zeros((S, LANES), F32).at[:, lane_off:lane_off + 2 * half].set(jnp.tile(jnp.sin(ang), (1, 2)))
    return cos, sin


def _rot_half_cols(w, half):
    return jnp.concatenate([-w[..., half:], w[..., :half]], axis=-1)


def _pad_heads(w, heads, width):
    K = w.shape[0]
    w = w.reshape(K, heads, width)
    return jnp.pad(w, ((0, 0), (0, 0), (0, LANES - width))).reshape(K, heads * LANES)


def _mla(x, g, w_in, q_norm, w_qb, kv_norm, w_kvb, w_out, B, S, tm=512, tq=512, tk=512, hp=4):
    T, D = x.shape
    H = MLA_HEADS
    half = MLA_ROPE // 2
    o1 = MLA_Q_LORA + MLA_KV_LORA
    w_r = w_in[:, o1:]
    tile_r = jnp.zeros((D, LANES), F32).at[:, MLA_NOPE:MLA_NOPE + MLA_ROPE].set(w_r)
    tile_rr = jnp.zeros((D, LANES), F32).at[:, MLA_NOPE:MLA_NOPE + MLA_ROPE].set(_rot_half_cols(w_r, half))
    w_lat = jnp.concatenate([w_in[:, :o1], tile_r, tile_rr], axis=1).astype(BF16)
    lat = _norm_linear(x, g, w_lat, F32, tn=w_lat.shape[1])

    wq3 = w_qb.reshape(MLA_Q_LORA, H, MLA_NOPE + MLA_ROPE)
    wq_rot = jnp.concatenate([jnp.zeros_like(wq3[..., :MLA_NOPE]), _rot_half_cols(wq3[..., MLA_NOPE:], half)], axis=-1)
    wq = _pad_heads(w_qb, H, MLA_NOPE + MLA_ROPE).astype(BF16)
    wqr = _pad_heads(wq_rot.reshape(MLA_Q_LORA, -1), H, MLA_NOPE + MLA_ROPE).astype(BF16)
    wkv3 = w_kvb.reshape(MLA_KV_LORA, H, MLA_NOPE + MLA_V)
    wk = _pad_heads(wkv3[..., :MLA_NOPE].reshape(MLA_KV_LORA, -1), H, MLA_NOPE).astype(BF16)
    wv = _pad_heads(wkv3[..., MLA_NOPE:].reshape(MLA_KV_LORA, -1), H, MLA_V).astype(BF16)
    cos, sin = _rope_tables(S, half, MLA_THETA, MLA_NOPE)

    nS = S // tm
    full = lambda shp: pl.BlockSpec(shp, lambda i: (0,) * len(shp))
    hm_spec = pl.BlockSpec((H, tm, LANES), lambda i: (0, i, 0))
    hm_shape = jax.ShapeDtypeStruct((H, T, LANES), BF16)
    q, k, v = pl.pallas_call(
        _mla_qkv_kernel,
        grid=(T // tm,),
        in_specs=[pl.BlockSpec((tm, lat.shape[1]), lambda i: (i, 0)),
                  full((1, MLA_Q_LORA)), full((1, MLA_KV_LORA)),
                  full(wq.shape), full(wqr.shape), full(wk.shape), full(wv.shape),
                  pl.BlockSpec((tm, LANES), lambda i: (i % nS, 0)),
                  pl.BlockSpec((tm, LANES), lambda i: (i % nS, 0))],
        out_specs=[hm_spec, hm_spec, pl.BlockSpec((H, MLA_VT_ROWS, tm), lambda i: (0, 0, i))],
        out_shape=[hm_shape, hm_shape, jax.ShapeDtypeStruct((H, MLA_VT_ROWS, T), BF16)],
        compiler_params=_cparams("parallel"),
    )(lat, q_norm.reshape(1, -1), kv_norm.reshape(1, -1), wq, wqr, wk, wv, cos, sin)

    q4, k4 = (t.reshape(H, B, S, LANES) for t in (q, k))
    assert tq == tk and H % hp == 0
    o = pl.pallas_call(
        functools.partial(_mla_attn_kernel, tq=tq, tk=tk),
        grid=(B, H // hp, S // tq),
        in_specs=[pl.BlockSpec((hp, None, tq, LANES), lambda b, h, qi: (h, b, qi, 0)),
                  pl.BlockSpec((hp, None, S, LANES), lambda b, h, qi: (h, b, 0, 0), pipeline_mode=pl.Buffered(1)),
                  pl.BlockSpec((hp, MLA_VT_ROWS, S), lambda b, h, qi: (h, 0, b), pipeline_mode=pl.Buffered(1))],
        out_specs=pl.BlockSpec((hp, None, tq, LANES), lambda b, h, qi: (h, b, qi, 0)),
        out_shape=jax.ShapeDtypeStruct((H, B, S, LANES), BF16),
        scratch_shapes=[pltpu.VMEM((hp, 1, tq), F32), pltpu.VMEM((hp, MLA_VT_ROWS, tq), F32),
                        pltpu.VMEM((hp, tk, tq), F32), pltpu.VMEM((hp, tk, tq), F32)],
        compiler_params=_cparams("parallel", "parallel", "arbitrary"),
    )(q4, k4, v)

    w_o = jnp.pad(w_out.reshape(H, MLA_V, D), ((0, 0), (0, LANES - MLA_V), (0, 0))).reshape(H * LANES, D)
    return _linear_res(o.reshape(H, T, LANES), w_o.astype(BF16), x, head_major=True)


def _ret_kernel(q_ref, k_ref, v_ref, g_ref, cos_ref, sin_ref, gnw_ref, y_ref, state_scr, *, C):
    @pl.when(pl.program_id(1) == 0)
    def _():
        state_scr[...] = jnp.zeros(state_scr.shape, F32)

    half = RET_QK // 2
    cos = cos_ref[...]
    sin = sin_ref[...]
    ii = lax.broadcasted_iota(jnp.int32, (C, C), 0)
    jj = lax.broadcasted_iota(jnp.int32, (C, C), 1)
    diff = (ii - jj).astype(F32)
    causal = ii >= jj
    idx = lax.broadcasted_iota(jnp.int32, (C, 1), 0).astype(F32)

    def rope(t):
        t1, t2 = t[:, :half], t[:, half:]
        return jnp.concatenate([t1 * cos - t2 * sin, t1 * sin + t2 * cos], axis=1)

    for h in range(RET_HEADS):
        lg = math.log1p(-(2.0 ** (-5.0 - h)))
        intra = jnp.exp(jnp.where(causal, lg * diff, -jnp.inf))
        q_dec = jnp.exp(lg * (idx + 1.0))
        k_dec = jnp.exp(lg * (C - 1.0 - idx))
        c_dec = math.exp(lg * C)
        q = rope(q_ref[:, h * RET_QK:(h + 1) * RET_QK].astype(F32))
        k = rope(k_ref[:, h * RET_QK:(h + 1) * RET_QK].astype(F32)) * (RET_QK ** -0.5)
        v = v_ref[:, h * RET_V:(h + 1) * RET_V]
        st = state_scr[h]
        a = _dot_nt(q.astype(BF16), k.astype(BF16)) * intra
        o = _dot(a.astype(BF16), v) + _dot((q * q_dec).astype(BF16), st.astype(BF16))
        state_scr[h] = st * c_dec + _dot((k * k_dec).T.astype(BF16), v)
        mu = jnp.mean(o, axis=-1, keepdims=True)
        oc = o - mu
        var = jnp.mean(oc * oc, axis=-1, keepdims=True)
        on = oc * lax.rsqrt(var + EPS) * gnw_ref[:, h * RET_V:(h + 1) * RET_V]
        gt = g_ref[:, h * RET_V:(h + 1) * RET_V].astype(F32)
        y_ref[:, h * RET_V:(h + 1) * RET_V] = (gt * jax.nn.sigmoid(gt) * on).astype(y_ref.dtype)


def _retention(x, g, w_in, gn_w, w_out, B, S, C=256):
    T, D = x.shape
    nqk, nv = RET_HEADS * RET_QK, RET_HEADS * RET_V
    proj = _norm_linear(x, g, w_in.astype(BF16), BF16)
    half = RET_QK // 2
    ang = jnp.arange(S, dtype=F32)[:, None] * (RET_THETA ** (-jnp.arange(half, dtype=F32) / half))
    cos, sin = jnp.cos(ang), jnp.sin(ang)
    nc = S // C
    y = pl.pallas_call(
        functools.partial(_ret_kernel, C=C),
        grid=(B, nc),
        in_specs=[pl.BlockSpec((C, nqk), lambda b, c: (b * nc + c, 0)),
                  pl.BlockSpec((C, nqk), lambda b, c: (b * nc + c, 1)),
                  pl.BlockSpec((C, nv), lambda b, c: (b * nc + c, (2 * nqk) // nv)),
                  pl.BlockSpec((C, nv), lambda b, c: (b * nc + c, (2 * nqk) // nv + 1)),
                  pl.BlockSpec((C, RET_QK // 2), lambda b, c: (c, 0)),
                  pl.BlockSpec((C, RET_QK // 2), lambda b, c: (c, 0)),
                  pl.BlockSpec((1, nv), lambda b, c: (0, 0))],
        out_specs=pl.BlockSpec((C, nv), lambda b, c: (b * nc + c, 0)),
        out_shape=jax.ShapeDtypeStruct((T, nv), BF16),
        scratch_shapes=[pltpu.VMEM((RET_HEADS, RET_QK, RET_V), F32)],
        compiler_params=_cparams("parallel", "arbitrary"),
    )(proj, proj, proj, proj, cos, sin, gn_w.reshape(1, nv))
    return _linear_res(y, w_out.astype(BF16), x)


def _split3(a):
    a1 = a.astype(BF16)
    r = a - a1.astype(F32)
    a2 = r.astype(BF16)
    a3 = (r - a2.astype(F32)).astype(BF16)
    return a1, a2, a3


SSD_COLB = 1024


def _ssd_kernel(*refs, C):
    nz, nx = SSD_D_INNER // SSD_COLB, SSD_CONV_DIM // SSD_COLB
    z_refs, x_refs = refs[:nz], refs[nz:nz + nx]
    (dt_ref, cw_ref, cb_ref, dtb_ref, alog_ref, dsk_ref, nw_ref, y_ref,
     xe_scr, act_scr, st_scr, y_scr) = refs[nz + nx:]
    G, N, P = SSD_GROUPS, SSD_STATE, SSD_P
    DI = SSD_D_INNER
    pairs_per_group = (SSD_HEADS // G) // 2
    tiles_per_colb = SSD_COLB // LANES

    HALO = 8

    @pl.when(pl.program_id(1) == 0)
    def _():
        xe_scr[0:HALO, :] = jnp.zeros((HALO, xe_scr.shape[1]), F32)
        st_scr[...] = jnp.zeros(st_scr.shape, F32)

    for kx in range(nx):
        xe_scr[HALO:HALO + C, kx * SSD_COLB:(kx + 1) * SSD_COLB] = x_refs[kx][...].astype(F32)
    CB = 512
    for cb in range(SSD_CONV_DIM // CB):
        cs = slice(cb * CB, (cb + 1) * CB)
        acc = jnp.broadcast_to(cb_ref[:, cs], (C, CB))
        for kk in range(SSD_CONV):
            off = HALO - (SSD_CONV - 1) + kk
            acc = acc + cw_ref[kk:kk + 1, cs] * xe_scr[off:off + C, cs]
        act_scr[:, cs] = (acc * jax.nn.sigmoid(acc)).astype(BF16)
    xe_scr[0:HALO, :] = xe_scr[C:C + HALO, :]

    dtr = dt_ref[...] + dtb_ref[...]
    dt = jnp.maximum(dtr, 0.0) + jnp.log1p(jnp.exp(-jnp.abs(dtr)))
    a = dt * (-jnp.exp(alog_ref[...]))
    ii = lax.broadcasted_iota(jnp.int32, (C, C), 0)
    jj = lax.broadcasted_iota(jnp.int32, (C, C), 1)
    causal = ii >= jj
    tril = jnp.where(causal, 1.0, 0.0).astype(BF16)
    a1, a2, a3 = _split3(a)
    a_cs = _dot(tril, a1) + _dot(tril, a2) + _dot(tril, a3)
    a_end = a_cs[C - 1:C, :]
    w_end = jnp.exp(a_end - a_cs) * dt
    ea = jnp.exp(a_cs)
    eend = jnp.exp(a_end)
    a_cs_t = a_cs.T
    dt_t = dt.T
    lane = lax.broadcasted_iota(jnp.int32, (C, LANES), 1)
    lo = lane < P

    def pair(t, ha):
        return jnp.where(lo[:t.shape[0]], t[:, ha:ha + 1], t[:, ha + 1:ha + 2])

    for g in range(G):
        bm = act_scr[:, DI + g * N:DI + (g + 1) * N]
        cm = act_scr[:, DI + G * N + g * N:DI + G * N + (g + 1) * N]
        cbm = _dot_nt(cm, bm)
        bm_t = bm.astype(F32).T.astype(BF16)
        ssq = jnp.zeros((C, 1), F32)
        for pj in range(pairs_per_group):
            j = g * pairs_per_group + pj
            ha = 2 * j
            cs = slice(j * LANES, (j + 1) * LANES)
            xp = act_scr[:, cs]
            ys = []
            for hh in (ha, ha + 1):
                seg = a_cs[:, hh:hh + 1] - a_cs_t[hh:hh + 1, :]
                dec = jnp.exp(jnp.where(causal, seg, -jnp.inf))
                mm = cbm * dec * dt_t[hh:hh + 1, :]
                ys.append(_dot(mm.astype(BF16), xp))
            y = jnp.where(lo, ys[0], ys[1])
            st = st_scr[:, cs]
            y = y + pair(ea, ha) * _dot(cm, st.astype(BF16))
            xpf = xp.astype(F32)
            xw = (xpf * pair(w_end, ha)).astype(BF16)
            st_scr[:, cs] = st * pair(eend, ha) + _dot(bm_t, xw)
            y = y + pair(dsk_ref[...], ha) * xpf
            jz = j % tiles_per_colb
            zt = z_refs[j // tiles_per_colb][:, jz * LANES:(jz + 1) * LANES].astype(F32)
            y = y * (zt * jax.nn.sigmoid(zt))
            ssq = ssq + jnp.sum(y * y, axis=-1, keepdims=True)
            y_scr[:, cs] = y
        gw = pairs_per_group * LANES
        gs = slice(g * gw, (g + 1) * gw)
        rs = lax.rsqrt(ssq / float(gw) + EPS)
        y_ref[:, gs] = (y_scr[:, gs] * rs * nw_ref[:, gs]).astype(y_ref.dtype)


def _pad_lanes(v):
    return jnp.pad(v.astype(F32), (0, LANES - v.shape[0])).reshape(1, LANES)


def _ssd(x, g, w_in, conv_w, conv_b, dt_bias, A_log, D_skip, norm_w, w_out, B, S, C=256):
    T, D = x.shape
    DI, CD = SSD_D_INNER, SSD_CONV_DIM
    w_main = w_in[:, :DI + CD].astype(BF16)
    w_dt = jnp.pad(w_in[:, DI + CD:], ((0, 0), (0, LANES - SSD_HEADS))).astype(BF16)
    proj = _norm_linear(x, g, w_main, BF16)
    dt = _norm_linear(x, g, w_dt, F32)
    nc = S // C
    full = lambda shp: pl.BlockSpec(shp, lambda b, c: (0,) * len(shp))
    ncolb = (DI + CD) // SSD_COLB

    def colb(kc):
        return pl.BlockSpec((C, SSD_COLB), lambda b, c: (b * nc + c, kc))

    y = pl.pallas_call(
        functools.partial(_ssd_kernel, C=C),
        grid=(B, nc),
        in_specs=[colb(kc) for kc in range(ncolb)] + [
                  pl.BlockSpec((C, LANES), lambda b, c: (b * nc + c, 0)),
                  full((SSD_CONV, CD)), full((1, CD)), full((1, LANES)), full((1, LANES)), full((1, LANES)),
                  full((1, DI))],
        out_specs=pl.BlockSpec((C, DI), lambda b, c: (b * nc + c, 0)),
        out_shape=jax.ShapeDtypeStruct((T, DI), BF16),
        scratch_shapes=[pltpu.VMEM((C + 8, CD), F32), pltpu.VMEM((C, CD), BF16),
                        pltpu.VMEM((SSD_STATE, DI), F32), pltpu.VMEM((C, DI), F32)],
        compiler_params=_cparams("parallel", "arbitrary"),
    )(*([proj] * ncolb), dt, conv_w, conv_b.reshape(1, CD), _pad_lanes(dt_bias), _pad_lanes(A_log),
      _pad_lanes(D_skip), norm_w.reshape(1, DI))
    return _linear_res(y, w_out.astype(BF16), x)


ROW_TILE = 8


def _rows_to_tiles(ref, val):
    n = val.shape[0]
    for s in range(ROW_TILE):
        ref[pl.ds(s, n, stride=ROW_TILE), :] = val[:, s * LANES:(s + 1) * LANES]


def _tiles_to_rows(ref, n):
    return jnp.concatenate([ref[pl.ds(s, n, stride=ROW_TILE), :] for s in range(ROW_TILE)], axis=1)


def _router_kernel(x_ref, g_ref, w1_ref, w2_ref, b_ref, h_ref, route_ref, cnt_ref, dense_ref, carry_scr):
    E = MOE_EXPERTS

    @pl.when(pl.program_id(0) == 0)
    def _():
        carry_scr[...] = jnp.zeros(carry_scr.shape, F32)

    h = _rms(x_ref[...], g_ref[...])
    _rows_to_tiles(h_ref, h)
    tm = h.shape[0]
    h1 = h.astype(BF16)
    h2 = (h - h1.astype(F32)).astype(BF16)
    logits = _dot(h1, w1_ref[...]) + _dot(h1, w2_ref[...]) + _dot(h2, w1_ref[...]) + b_ref[...]
    lane = lax.broadcasted_iota(jnp.int32, (tm, LANES), 1)
    lanef = lane.astype(F32)
    big = float(LANES)
    glog = jnp.where((lane >= E) & (lane < E + MOE_GROUPS), logits, -jnp.inf)
    gmax = jnp.max(glog, axis=-1, keepdims=True)
    gval = 1.0 / jnp.sum(jnp.exp(glog - gmax), axis=-1, keepdims=True)
    gidx = jnp.min(jnp.where(glog == gmax, lanef, big), axis=-1, keepdims=True) - float(E)
    in_group = (lane < E) & ((lane // MOE_EPG).astype(F32) == gidx)
    el = jnp.where(in_group, logits, -jnp.inf)
    v1 = jnp.max(el, axis=-1, keepdims=True)
    i1 = jnp.min(jnp.where(el == v1, lanef, big), axis=-1, keepdims=True)
    el2 = jnp.where(lanef == i1, -jnp.inf, el)
    v2 = jnp.max(el2, axis=-1, keepdims=True)
    i2 = jnp.min(jnp.where(el2 == v2, lanef, big), axis=-1, keepdims=True)
    e2 = jnp.exp(v2 - v1)
    wa = gval / (1.0 + e2)
    wb = gval * e2 / (1.0 + e2)
    oh1 = jnp.where(lanef == i1, 1.0, 0.0)
    oh2 = jnp.where(lanef == i2, 1.0, 0.0)
    ohs = oh1 + oh2
    ii = lax.broadcasted_iota(jnp.int32, (tm, tm), 0)
    jj = lax.broadcasted_iota(jnp.int32, (tm, tm), 1)
    stril = jnp.where(ii > jj, 1.0, 0.0).astype(BF16)
    before = _dot(stril, ohs.astype(BF16)) + carry_scr[0:1, :]
    r1 = jnp.sum(before * oh1, axis=-1, keepdims=True)
    r2 = jnp.sum(before * oh2, axis=-1, keepdims=True)
    carry = carry_scr[0:1, :] + jnp.sum(ohs, axis=0, keepdims=True)
    carry_scr[...] = jnp.broadcast_to(carry, carry_scr.shape)
    cnt_ref[...] = jnp.broadcast_to(carry, cnt_ref.shape)
    route = jnp.zeros((tm, LANES), F32)
    for ln, val in enumerate((i1, i2, wa, wb, r1, r2)):
        route = jnp.where(lane == ln, val, route)
    route_ref[...] = route
    eye = (lax.broadcasted_iota(jnp.int32, (LANES, LANES), 0) == lax.broadcasted_iota(jnp.int32, (LANES, LANES), 1))
    rows = []
    for val in (i1, i2, r1, r2):
        vb = jnp.broadcast_to(val, (tm, LANES))
        for r in range(tm // LANES):
            rows.append(jnp.sum(jnp.where(eye, vb[r * LANES:(r + 1) * LANES], 0.0), axis=0, keepdims=True))
    dense_ref[...] = jnp.concatenate(rows, axis=0)


def _dispatch_kernel(pos_ref, lo_ref, hi_ref, h_ref, xg_hbm, zero_scr, sem, zsem):
    ntok = pos_ref.shape[0] // MOE_TOPK
    base = pl.program_id(0) * TOK_TILE

    @pl.when(pl.program_id(0) == 0)
    def _():
        zero_scr[...] = jnp.zeros(zero_scr.shape, F32)
        blk = zero_scr.shape[0]

        def zero_copy(r, n):
            return pltpu.make_async_copy(zero_scr.at[pl.ds(0, n)], xg_hbm.at[pl.ds(pl.multiple_of(r * n, n), n)], zsem)

        def loop(lo, hi, n, wait):
            def body(r, c):
                cp = zero_copy(r, n)
                cp.wait() if wait else cp.start()
                return c
            lax.fori_loop(lo, hi, body, 0)

        tail_lo = hi_ref[MOE_EXPERTS - 1] * ROW_TILE // blk
        tail_hi = xg_hbm.shape[0] // blk
        for wait in (False, True):
            for e in range(MOE_EXPERTS):
                loop(lo_ref[e], hi_ref[e], ROW_TILE, wait)
            loop(tail_lo, tail_hi, blk, wait)

    def copy(t, k):
        row = pl.multiple_of(pos_ref[k * ntok + base + t] * ROW_TILE, ROW_TILE)
        return pltpu.make_async_copy(h_ref.at[pl.ds(t * ROW_TILE, ROW_TILE)], xg_hbm.at[pl.ds(row, ROW_TILE)], sem)

    for t in range(TOK_TILE):
        for k in range(MOE_TOPK):
            copy(t, k).start(priority=k)
    same_size = pltpu.make_async_copy(h_ref.at[pl.ds(0, ROW_TILE)], xg_hbm.at[pl.ds(0, ROW_TILE)], sem)
    for _ in range(TOK_TILE * MOE_TOPK):
        same_size.wait()


def _expert_kernel(be_ref, nu_ref, x_ref, wg_ref, wu_ref, wd_ref, o_ref, wg_scr, wu_scr, wd_scr):
    i = pl.program_id(0)
    used = i < nu_ref[0]
    n = x_ref.shape[0] // ROW_TILE

    @pl.when(used & ((i == 0) | (be_ref[i] != be_ref[jnp.maximum(i - 1, 0)])))
    def _():
        wg_scr[...] = wg_ref[...].astype(BF16)
        wu_scr[...] = wu_ref[...].astype(BF16)
        wd_scr[...] = wd_ref[...].astype(BF16)

    @pl.when(used)
    def _():
        xb = _tiles_to_rows(x_ref, n).astype(BF16)
        hg = _dot(xb, wg_scr[...])
        hu = _dot(xb, wu_scr[...])
        _rows_to_tiles(o_ref, _dot((hg * jax.nn.sigmoid(hg) * hu).astype(BF16), wd_scr[...]))

    @pl.when(jnp.logical_not(used))
    def _():
        o_ref[...] = jnp.zeros(o_ref.shape, F32)


def _ple_kernel(pos_ref, x_ref, rt_ref, p_ref, g_ref, gf_ref, wg_ref, wp_ref, yb_hbm, o_ref,
                buf_a, buf_b, sems, *, final):
    tp = x_ref.shape[0]
    i = pl.program_id(0)
    last = pl.num_programs(0) - 1
    slot = i % 2
    bufs = (buf_a, buf_b)

    def copy(tile, sl, t, k):
        row = pl.multiple_of(pos_ref[k * (pos_ref.shape[0] // MOE_TOPK) + tile * tp + t] * ROW_TILE, ROW_TILE)
        dst = bufs[k].at[sl, pl.ds(pl.multiple_of(t * ROW_TILE, ROW_TILE), ROW_TILE)]
        return pltpu.make_async_copy(yb_hbm.at[pl.ds(row, ROW_TILE)], dst, sems.at[sl])

    def wait_slot(sl):
        for k in range(MOE_TOPK):
            same_size = pltpu.make_async_copy(yb_hbm.at[pl.ds(0, ROW_TILE)], bufs[k].at[sl, pl.ds(0, ROW_TILE)],
                                              sems.at[sl])
            for _ in range(tp):
                same_size.wait()

    @pl.when(i == 0)
    def _():
        def first(t, c):
            for k in range(MOE_TOPK):
                copy(0, 0, t, k).start(priority=k)
            return c
        lax.fori_loop(0, tp, first, 0, unroll=8)

    wait_slot(slot)
    nxt = jnp.minimum(i + 1, last)
    for t in range(tp):
        for k in range(MOE_TOPK):
            copy(nxt, 1 - slot, t, k).start(priority=k)

    ple = _dot(p_ref[...].astype(BF16), wp_ref[...])
    rt = rt_ref[...]
    ya = _tiles_to_rows(buf_a.at[slot], tp)
    yb = _tiles_to_rows(buf_b.at[slot], tp)
    x1 = x_ref[...] + rt[:, 2:3] * ya + rt[:, 3:4] * yb
    gate = jax.nn.sigmoid(_dot(_rms(x1, g_ref[...]).astype(BF16), wg_ref[...]))
    out = x1 + ple * gate
    if final:
        out = _rms(out, gf_ref[...])
    o_ref[...] = out

    @pl.when(i == last)
    def _():
        wait_slot(1 - slot)


def _moe_ple(x, layer, p_all, norm_moe, w_group, b_group, w_expert, b_expert, w_gate, w_up, w_down,
             norm_ple, ple_w, ple_w_gate, norm_final, final, tm=512):
    T, D = x.shape
    E, BLK = MOE_EXPERTS, MOE_BLOCK
    A = T * MOE_TOPK
    nb = A // BLK + E
    P = nb * BLK

    w_r = jnp.zeros((D, LANES), F32).at[:, :E].set(w_expert).at[:, E:E + MOE_GROUPS].set(w_group)
    b_r = jnp.zeros((1, LANES), F32).at[0, :E].set(b_expert).at[0, E:E + MOE_GROUPS].set(b_group)
    w_r1 = w_r.astype(BF16)
    w_r2 = (w_r - w_r1.astype(F32)).astype(BF16)
    full = lambda shp: pl.BlockSpec(shp, lambda i: (0,) * len(shp))
    nsub = tm // LANES
    h, route, cnt, dense = pl.pallas_call(
        _router_kernel,
        grid=(T // tm,),
        in_specs=[pl.BlockSpec((tm, D), lambda i: (i, 0)), full((1, D)), full((D, LANES)), full((D, LANES)),
                  full((1, LANES))],
        out_specs=[pl.BlockSpec((tm * ROW_TILE, LANES), lambda i: (i, 0)),
                   pl.BlockSpec((tm, LANES), lambda i: (i, 0)), full((8, LANES)),
                   pl.BlockSpec((2 * MOE_TOPK * nsub, LANES), lambda i: (i, 0))],
        out_shape=[jax.ShapeDtypeStruct((T * ROW_TILE, LANES), F32), jax.ShapeDtypeStruct((T, LANES), F32),
                   jax.ShapeDtypeStruct((8, LANES), F32),
                   jax.ShapeDtypeStruct((T // tm * 2 * MOE_TOPK * nsub, LANES), F32)],
        scratch_shapes=[pltpu.VMEM((8, LANES), F32)],
        compiler_params=_cparams("arbitrary"),
    )(x, norm_moe.reshape(1, D), w_r1, w_r2, b_r)

    counts = cnt[0, :E].astype(jnp.int32)
    padded = (counts + BLK - 1) // BLK * BLK
    pad_end = jnp.cumsum(padded)
    pad_start = pad_end - padded
    dense = dense.reshape(T // tm, 2 * MOE_TOPK, nsub * LANES).astype(jnp.int32)
    def slot(k):
        e, acc = dense[:, k], dense[:, MOE_TOPK + k]
        for j in range(E):
            acc = acc + jnp.where(e == j, pad_start[j], 0)
        return acc.reshape(T)

    pos = jnp.concatenate([slot(k) for k in range(MOE_TOPK)])
    blk_start = jnp.arange(nb, dtype=jnp.int32) * BLK
    blk_expert = jnp.minimum(jnp.sum((pad_end[None, :] <= blk_start[:, None]).astype(jnp.int32), axis=1), E - 1)
    n_used = (pad_end[-1:] // BLK).astype(jnp.int32)

    assert D == ROW_TILE * LANES
    xg = pl.pallas_call(
        _dispatch_kernel,
        grid_spec=pltpu.PrefetchScalarGridSpec(
            num_scalar_prefetch=3, grid=(T // TOK_TILE,),
            in_specs=[pl.BlockSpec((TOK_TILE * ROW_TILE, LANES), lambda i, ps, lo, hi: (i, 0))],
            out_specs=pl.BlockSpec(memory_space=pl.ANY),
            scratch_shapes=[pltpu.VMEM((BLK * ROW_TILE, LANES), F32), pltpu.SemaphoreType.DMA,
                            pltpu.SemaphoreType.DMA]),
        out_shape=jax.ShapeDtypeStruct((P * ROW_TILE, LANES), F32),
        compiler_params=_cparams("arbitrary"),
    )(pos, pad_start + counts, pad_end, h)

    def wmap(i, be, nu):
        return (layer, be[jnp.minimum(i, nu[0] - 1)], 0, 0)

    def xmap(i, be, nu):
        return (jnp.minimum(i, nu[0] - 1), 0)

    yb = pl.pallas_call(
        _expert_kernel,
        grid_spec=pltpu.PrefetchScalarGridSpec(
            num_scalar_prefetch=2, grid=(nb,),
            in_specs=[pl.BlockSpec((BLK * ROW_TILE, LANES), xmap),
                      pl.BlockSpec((None, None, D, MOE_FF), wmap), pl.BlockSpec((None, None, D, MOE_FF), wmap),
                      pl.BlockSpec((None, None, MOE_FF, D), wmap)],
            out_specs=pl.BlockSpec((BLK * ROW_TILE, LANES), lambda i, be, nu: (i, 0)),
            scratch_shapes=[pltpu.VMEM((D, MOE_FF), BF16), pltpu.VMEM((D, MOE_FF), BF16),
                            pltpu.VMEM((MOE_FF, D), BF16)]),
        out_shape=jax.ShapeDtypeStruct((P * ROW_TILE, LANES), F32),
        compiler_params=_cparams("arbitrary"),
    )(blk_expert, n_used, xg, w_gate, w_up, w_down)

    tp = TOK_TILE
    PD = p_all.shape[-1]
    return pl.pallas_call(
        functools.partial(_ple_kernel, final=final),
        grid_spec=pltpu.PrefetchScalarGridSpec(
            num_scalar_prefetch=1, grid=(T // tp,),
            in_specs=[pl.BlockSpec((tp, D), lambda i, ps: (i, 0)),
                      pl.BlockSpec((tp, LANES), lambda i, ps: (i, 0)),
                      pl.BlockSpec((None, tp, PD), lambda i, ps: (layer, i, 0)),
                      pl.BlockSpec((1, D), lambda i, ps: (0, 0)), pl.BlockSpec((1, D), lambda i, ps: (0, 0)),
                      pl.BlockSpec((D, D), lambda i, ps: (0, 0)), pl.BlockSpec((PD, D), lambda i, ps: (0, 0)),
                      pl.BlockSpec(memory_space=pl.ANY)],
            out_specs=pl.BlockSpec((tp, D), lambda i, ps: (i, 0)),
            scratch_shapes=[pltpu.VMEM((2, tp * ROW_TILE, LANES), F32), pltpu.VMEM((2, tp * ROW_TILE, LANES), F32),
                            pltpu.SemaphoreType.DMA((2,))]),
        out_shape=jax.ShapeDtypeStruct((T, D), F32),
        compiler_params=_cparams("arbitrary"),
    )(pos, x, route, p_all, norm_ple.reshape(1, D), norm_final.reshape(1, D),
      ple_w_gate.astype(BF16), ple_w.astype(BF16), yb)


def kernel(x, p, norm_mix, norm_moe, norm_ple, norm_final, dsa_w_in, dsa_w_out, mla_w_in, mla_q_norm, mla_w_qb,
           mla_kv_norm, mla_w_kvb, mla_w_out, ret_w_in, ret_gn_w, ret_w_out, ssd_w_in, ssd_conv_w, ssd_conv_b,
           ssd_dt_bias, ssd_A_log, ssd_D, ssd_norm_w, ssd_w_out, moe_w_group, moe_b_group, moe_w_expert,
           moe_b_expert, moe_w_gate, moe_w_up, moe_w_down, ple_w, ple_w_gate):
    B, S, D = x.shape
    depth = p.shape[0]
    xt = x.reshape(B * S, D)
    for i in range(depth):
        kind = i % 4
        if kind == 0:
            xt = _dilated_attention(xt, norm_mix[i], dsa_w_in, dsa_w_out, B, S)
        elif kind == 1:
            xt = _mla(xt, norm_mix[i], mla_w_in, mla_q_norm, mla_w_qb, mla_kv_norm, mla_w_kvb, mla_w_out, B, S)
        elif kind == 2:
            xt = _retention(xt, norm_mix[i], ret_w_in, ret_gn_w, ret_w_out, B, S)
        else:
            xt = _ssd(xt, norm_mix[i], ssd_w_in, ssd_conv_w, ssd_conv_b, ssd_dt_bias, ssd_A_log, ssd_D,
                      ssd_norm_w, ssd_w_out, B, S)
        xt = _moe_ple(xt, i, p.reshape(depth, B * S, -1), norm_moe[i], moe_w_group[i], moe_b_group[i],
                      moe_w_expert[i], moe_b_expert[i], moe_w_gate, moe_w_up, moe_w_down, norm_ple[i], ple_w[i],
                      ple_w_gate[i], norm_final, final=(i == depth - 1))
    return xt.reshape(B, S, D)
```
